```python
import math
import jax
import jax.numpy as jnp
from jax import lax
import numpy as np


D_MODEL = 1024
BATCH = 4
SEQ = 4096
DEPTH = 2
DEC_BATCH = 32
DEC_SEQ = 8
PAST_LEN = 8192
PAGE_SIZE = 128

HEAD_DIM = 64
N_HEADS_A = 8
N_HEADS_B = 8
D_A = N_HEADS_A * HEAD_DIM
D_B = N_HEADS_B * HEAD_DIM
IDX_HEADS = 8
IDX_DIM = 64
IDX_TOPK = 256
MOBA_BLOCK = 256
MOBA_TOPK = 3
N_BUCKETS = 32
MAX_DISTANCE = 128
N_EXPERTS = 32
TOP_K = 4
D_EXPERT = D_MODEL
SWIGLU_LIMIT = 7.0
SWIGLU_ALPHA = 1.702
PLE_DIM = 256
LN_EPS = 1e-5
ALPHA_DN = (2 * DEPTH) ** 0.25
BETA_DN = (8 * DEPTH) ** -0.25
ATTN_SCALE = HEAD_DIM ** -0.5
IN_SIZES = (D_A, D_A, D_A, IDX_HEADS * IDX_DIM, IDX_DIM, IDX_HEADS, D_B, D_B, D_B, 2 * D_MODEL)
D_IN = sum(IN_SIZES)
IDX_Q_BLOCK = 128
MOBA_Q_BLOCK = 32
MOE_TOKEN_BLOCK = 512

kernel_name = "hybrid_dsa_moba_moe_decoder_step"


def layer_norm(x, g, b):
    xf = x.astype(jnp.float32)
    mu = jnp.mean(xf, axis=-1, keepdims=True)
    var = jnp.mean(jnp.square(xf - mu), axis=-1, keepdims=True)
    return ((xf - mu) * lax.rsqrt(var + LN_EPS)).astype(x.dtype) * g + b


def rel_bucket(dist):
    d = jnp.maximum(dist, 0)
    max_exact = N_BUCKETS // 2
    ratio = jnp.log(jnp.maximum(d, 1).astype(jnp.float32) / max_exact) / math.log(MAX_DISTANCE / max_exact)
    large = jnp.minimum(max_exact + (ratio * (N_BUCKETS - max_exact)).astype(jnp.int32), N_BUCKETS - 1)
    return jnp.where(d < max_exact, d, large)


def project_in(x, w_in, kidx_g, kidx_b):
    B, T, _ = x.shape
    h = jnp.einsum('btd,de->bte', x, w_in)
    offs = np.cumsum(IN_SIZES)[:-1].tolist()
    q_a, k_a, v_a, q_i, k_i, w_i, q_b, k_b, v_b, g = jnp.split(h, offs, axis=-1)
    return {
        'q_a': q_a.reshape(B, T, N_HEADS_A, HEAD_DIM),
        'k_a': k_a.reshape(B, T, N_HEADS_A, HEAD_DIM),
        'v_a': v_a.reshape(B, T, N_HEADS_A, HEAD_DIM),
        'q_i': q_i.reshape(B, T, IDX_HEADS, IDX_DIM),
        'k_i': layer_norm(k_i, kidx_g, kidx_b),
        'w_i': w_i * IDX_HEADS ** -0.5,
        'q_b': q_b.reshape(B, T, N_HEADS_B, HEAD_DIM),
        'k_b': k_b.reshape(B, T, N_HEADS_B, HEAD_DIM),
        'v_b': v_b.reshape(B, T, N_HEADS_B, HEAD_DIM),
        'gates': jax.nn.sigmoid(g).reshape(B, T, 2, D_MODEL),
    }


def dsa_attend(q, q_i, w_i, t_pos, k_idx_all, fetch, rb):
    n_keys = k_idx_all.shape[1]
    n_top = min(IDX_TOPK, n_keys // 4)
    s = jax.nn.relu(jnp.einsum('bqhd,bsd->bqhs', q_i, k_idx_all) * IDX_DIM ** -0.5)
    score = jnp.einsum('bqhs,bqh->bqs', s, w_i).astype(jnp.float32)
    causal = jnp.arange(n_keys)[None, :] <= t_pos[:, None]
    score = jnp.where(causal[None], score, -jnp.inf)
    _, idx = lax.top_k(score, n_top)
    valid = idx <= t_pos[None, :, None]
    k_rows, v_rows = fetch(idx)
    logits = jnp.einsum('bqhd,bqkhd->bqhk', q, k_rows).astype(jnp.float32) * ATTN_SCALE
    bias = rb[rel_bucket(t_pos[None, :, None] - idx)]
    logits = logits + jnp.moveaxis(bias, -1, 2).astype(jnp.float32)
    logits = jnp.where(valid[:, :, None, :], logits, -jnp.inf)
    p = jax.nn.softmax(logits, axis=-1).astype(v_rows.dtype)
    return jnp.einsum('bqhk,bqkhd->bqhd', p, v_rows)


def moba_attend(q, t_pos, k_means, fetch, rb):
    B, Q, H, _ = q.shape
    nf = k_means.shape[1]
    n_sel = min(MOBA_TOPK, nf)
    c = t_pos // MOBA_BLOCK
    offs = jnp.arange(MOBA_BLOCK)
    pos_own = jnp.broadcast_to((c[:, None] * MOBA_BLOCK + offs)[None, :, None, :], (B, Q, H, MOBA_BLOCK))
    valid_own = pos_own <= t_pos[None, :, None, None]
    if n_sel > 0:
        g = jnp.einsum('bqhd,bjhd->bqhj', q, k_means).astype(jnp.float32)
        past_blk = jnp.arange(nf)[None, :] < c[:, None]
        g = jnp.where(past_blk[None, :, None, :], g, -jnp.inf)
        _, sel = lax.top_k(g, n_sel)
        pos_sel = (sel[..., None] * MOBA_BLOCK + offs).reshape(B, Q, H, n_sel * MOBA_BLOCK)
        valid_sel = jnp.broadcast_to((sel < c[None, :, None, None])[..., None], sel.shape + (MOBA_BLOCK,)).reshape(B, Q, H, n_sel * MOBA_BLOCK)
        pos = jnp.concatenate([pos_sel, pos_own], axis=-1)
        valid = jnp.concatenate([valid_sel, valid_own], axis=-1)
    else:
        pos, valid = pos_own, valid_own
    k_rows, v_rows = fetch(pos)
    logits = jnp.einsum('bqhd,bqhkd->bqhk', q, k_rows).astype(jnp.float32) * ATTN_SCALE
    bias = rb[rel_bucket(t_pos[None, :, None, None] - pos), jnp.arange(H)[None, None, :, None]]
    logits = jnp.where(valid, logits + bias.astype(jnp.float32), -jnp.inf)
    p = jax.nn.softmax(logits, axis=-1).astype(v_rows.dtype)
    return jnp.einsum('bqhk,bqhkd->bqhd', p, v_rows)


def block_means(k_all):
    B, L, H, Dh = k_all.shape
    nf = L // MOBA_BLOCK
    kf = k_all[:, :nf * MOBA_BLOCK].astype(jnp.float32).reshape(B, nf, MOBA_BLOCK, H, Dh)
    return jnp.mean(kf, axis=2).astype(k_all.dtype)


def to_blocks(a, qb):
    return a.reshape((a.shape[0], a.shape[1] // qb, qb) + a.shape[2:]).swapaxes(0, 1)


def from_blocks(a):
    a = a.swapaxes(0, 1)
    return a.reshape((a.shape[0], a.shape[1] * a.shape[2]) + a.shape[3:])


def prompt_mixers(pr, rb_a, rb_b):
    q_a, k_a, v_a = pr['q_a'], pr['k_a'], pr['v_a']
    q_b, k_b, v_b = pr['q_b'], pr['k_b'], pr['v_b']
    B, S = q_a.shape[:2]
    t_pos = jnp.arange(S, dtype=jnp.int32)
    b3 = jnp.arange(B)[:, None, None]

    def fetch_a(idx):
        return k_a[b3, idx], v_a[b3, idx]

    def dsa_step(xs):
        q, qi, wi, t = xs
        return dsa_attend(q, qi, wi, t, pr['k_i'], fetch_a, rb_a)

    o_a = from_blocks(lax.map(dsa_step, (to_blocks(q_a, IDX_Q_BLOCK), to_blocks(pr['q_i'], IDX_Q_BLOCK),
                                         to_blocks(pr['w_i'], IDX_Q_BLOCK), t_pos.reshape(-1, IDX_Q_BLOCK))))

    means = block_means(k_b)
    b4 = jnp.arange(B)[:, None, None, None]
    h4 = jnp.arange(N_HEADS_B)[None, None, :, None]

    def fetch_b(pos):
        pc = jnp.minimum(pos, S - 1)
        return k_b[b4, pc, h4], v_b[b4, pc, h4]

    def moba_step(xs):
        q, t = xs
        return moba_attend(q, t, means, fetch_b, rb_b)

    o_b = from_blocks(lax.map(moba_step, (to_blocks(q_b, MOBA_Q_BLOCK), t_pos.reshape(-1, MOBA_Q_BLOCK))))
    return o_a, o_b


def gather_paged(pool, new, page_table, b_idx, pos, h_idx=None):
    in_past = pos < PAST_LEN
    pp = jnp.minimum(pos, PAST_LEN - 1)
    phys = page_table[b_idx, pp // PAGE_SIZE]
    off = pp % PAGE_SIZE
    pn = jnp.clip(pos - PAST_LEN, 0, new.shape[1] - 1)
    if h_idx is None:
        past, cur = pool[phys, off], new[b_idx, pn]
    else:
        past, cur = pool[phys, off, h_idx], new[b_idx, pn, h_idx]
    mask = in_past.reshape(in_past.shape + (1,) * (past.ndim - in_past.ndim))
    return jnp.where(mask, past, cur)


def sample_mixers(sm, ck_a, cv_a, ck_i, ck_b, cv_b, page_table, rb_a, rb_b):
    DB, T = sm['q_a'].shape[:2]
    past = page_table.shape[1] * PAGE_SIZE
    t_pos = PAST_LEN + jnp.arange(T, dtype=jnp.int32)
    k_i_all = jnp.concatenate([ck_i[page_table].reshape(DB, past, IDX_DIM), sm['k_i']], axis=1)
    b3 = jnp.arange(DB)[:, None, None]

    def fetch_a(idx):
        return (gather_paged(ck_a, sm['k_a'], page_table, b3, idx),
                gather_paged(cv_a, sm['v_a'], page_table, b3, idx))

    o_a = dsa_attend(sm['q_a'], sm['q_i'], sm['w_i'], t_pos, k_i_all, fetch_a, rb_a)

    k_b_all = jnp.concatenate([ck_b[page_table].reshape(DB, past, N_HEADS_B, HEAD_DIM), sm['k_b']], axis=1)
    means = block_means(k_b_all)
    b4 = jnp.arange(DB)[:, None, None, None]
    h4 = jnp.arange(N_HEADS_B)[None, None, :, None]

    def fetch_b(pos):
        return (gather_paged(ck_b, sm['k_b'], page_table, b4, pos, h4),
                gather_paged(cv_b, sm['v_b'], page_table, b4, pos, h4))

    o_b = moba_attend(sm['q_b'], t_pos, means, fetch_b, rb_b)
    return o_a, o_b


def attn_merge(x, o_a, o_b, gates, w_ba, w_bb, w_o, g, b):
    B, T, _ = x.shape
    br_a = jnp.einsum('btk,kd->btd', o_a.reshape(B, T, D_A), w_ba)
    br_b = jnp.einsum('btk,kd->btd', o_b.reshape(B, T, D_B), w_bb)
    y = jnp.einsum('btd,de->bte', gates[:, :, 0] * br_a + gates[:, :, 1] * br_b, w_o)
    return layer_norm(ALPHA_DN * x + y, g, b)


def moe(x, w_r, b_r, w_gu, b_gu, w_dn, b_dn):
    B, T, D = x.shape
    n = B * T
    xf = x.reshape(n, D)
    logits = (xf @ w_r + b_r).astype(jnp.float32)
    top_v, top_i = lax.top_k(logits, TOP_K)
    gate = jax.nn.softmax(top_v, axis=-1)
    comb = jnp.einsum('nk,nke->ne', gate, jax.nn.one_hot(top_i, N_EXPERTS, dtype=jnp.float32)).astype(x.dtype)
    nb = -(-n // MOE_TOKEN_BLOCK)
    pad = nb * MOE_TOKEN_BLOCK - n
    xb = jnp.pad(xf, ((0, pad), (0, 0))).reshape(nb, MOE_TOKEN_BLOCK, D)
    cb = jnp.pad(comb, ((0, pad), (0, 0))).reshape(nb, MOE_TOKEN_BLOCK, N_EXPERTS)

    def expert_block(xs):
        xt, ct = xs
        gu = jnp.einsum('td,edf->tef', xt, w_gu) + b_gu
        gt = jnp.minimum(gu[..., :D_EXPERT], SWIGLU_LIMIT)
        up = jnp.clip(gu[..., D_EXPERT:], -SWIGLU_LIMIT, SWIGLU_LIMIT)
        h = (up + 1) * gt * jax.nn.sigmoid(SWIGLU_ALPHA * gt) * ct[..., None]
        return jnp.einsum('tef,efd->td', h, w_dn) + ct @ b_dn

    y = lax.map(expert_block, (xb, cb)).reshape(nb * MOE_TOKEN_BLOCK, D)[:n]
    return y.reshape(B, T, D)


def channel_sublayer(x1, p_emb, w_r, b_r, w_gu, b_gu, w_dn, b_dn, w_pg, w_pp, g, b):
    y = moe(x1, w_r, b_r, w_gu, b_gu, w_dn, b_dn)
    ple = jax.nn.sigmoid(jnp.einsum('btd,de->bte', x1, w_pg)) * jnp.einsum('btp,pd->btd', p_emb, w_pp)
    return layer_norm(ALPHA_DN * x1 + y + ple, g, b)


def setup_inputs(seed: int = 0) -> dict:
    key = jax.random.key(seed)
    ks = jax.random.split(key, 32)
    n_pages = PAST_LEN // PAGE_SIZE
    n_pool = (DEC_BATCH * n_pages * 5) // 4

    def nrm(k, shape, s):
        return jax.random.normal(k, shape, jnp.float32) * s

    page_table = jax.random.permutation(ks[9], n_pool)[:DEC_BATCH * n_pages].reshape(DEC_BATCH, n_pages).astype(jnp.int32)
    return {
        'x_prompt': nrm(ks[0], (BATCH, SEQ, D_MODEL), 1.0),
        'x_sample': nrm(ks[1], (DEC_BATCH, DEC_SEQ, D_MODEL), 1.0),
        'cache_k_a': nrm(ks[2], (DEPTH, n_pool, PAGE_SIZE, N_HEADS_A, HEAD_DIM), 1.0),
        'cache_v_a': nrm(ks[3], (DEPTH, n_pool, PAGE_SIZE, N_HEADS_A, HEAD_DIM), 1.0),
        'cache_kidx': nrm(ks[4], (DEPTH, n_pool, PAGE_SIZE, IDX_DIM), 1.0),
        'cache_k_b': nrm(ks[5], (DEPTH, n_pool, PAGE_SIZE, N_HEADS_B, HEAD_DIM), 1.0),
        'cache_v_b': nrm(ks[6], (DEPTH, n_pool, PAGE_SIZE, N_HEADS_B, HEAD_DIM), 1.0),
        'page_table': page_table,
        'p_prompt': nrm(ks[7], (DEPTH, BATCH, SEQ, PLE_DIM), 1.0),
        'p_sample': nrm(ks[8], (DEPTH, DEC_BATCH, DEC_SEQ, PLE_DIM), 1.0),
        'rel_bias': nrm(ks[10], (N_BUCKETS, N_HEADS_A + N_HEADS_B), 0.3),
        'w_in': nrm(ks[11], (DEPTH, D_MODEL, D_IN), D_MODEL ** -0.5),
        'kidx_ln_g': 1.0 + nrm(ks[12], (DEPTH, IDX_DIM), 0.01),
        'kidx_ln_b': nrm(ks[13], (DEPTH, IDX_DIM), 0.01),
        'w_branch_a': nrm(ks[14], (DEPTH, D_A, D_MODEL), D_A ** -0.5),
        'w_branch_b': nrm(ks[15], (DEPTH, D_B, D_MODEL), D_B ** -0.5),
        'w_out': nrm(ks[16], (DEPTH, D_MODEL, D_MODEL), D_MODEL ** -0.5 * BETA_DN),
        'ln1_g': 1.0 + nrm(ks[17], (DEPTH, D_MODEL), 0.01),
        'ln1_b': nrm(ks[18], (DEPTH, D_MODEL), 0.01),
        'w_router': nrm(ks[19], (DEPTH, D_MODEL, N_EXPERTS), D_MODEL ** -0.5),
        'b_router': nrm(ks[20], (DEPTH, N_EXPERTS), 0.01),
        'w_gate_up': nrm(ks[21], (DEPTH, N_EXPERTS, D_MODEL, 2 * D_EXPERT), D_MODEL ** -0.5),
        'b_gate_up': nrm(ks[22], (DEPTH, N_EXPERTS, 2 * D_EXPERT), 0.01),
        'w_down': nrm(ks[23], (DEPTH, N_EXPERTS, D_EXPERT, D_MODEL), D_EXPERT ** -0.5 * BETA_DN),
        'b_down': nrm(ks[24], (DEPTH, N_EXPERTS, D_MODEL), 0.01),
        'w_ple_gate': nrm(ks[25], (DEPTH, D_MODEL, D_MODEL), D_MODEL ** -0.5),
        'w_ple_proj': nrm(ks[26], (DEPTH, PLE_DIM, D_MODEL), PLE_DIM ** -0.5 * BETA_DN),
        'ln2_g': 1.0 + nrm(ks[27], (DEPTH, D_MODEL), 0.01),
        'ln2_b': nrm(ks[28], (DEPTH, D_MODEL), 0.01),
    }


def reference(x_prompt, x_sample, cache_k_a, cache_v_a, cache_kidx, cache_k_b, cache_v_b, page_table,
              p_prompt, p_sample, rel_bias, w_in, kidx_ln_g, kidx_ln_b, w_branch_a, w_branch_b, w_out,
              ln1_g, ln1_b, w_router, b_router, w_gate_up, b_gate_up, w_down, b_down,
              w_ple_gate, w_ple_proj, ln2_g, ln2_b):
    rb_a = rel_bias[:, :N_HEADS_A]
    rb_b = rel_bias[:, N_HEADS_A:]
    xp, xs = x_prompt, x_sample
    rows_p, rows_s = [], []
    for li in range(DEPTH):
        pr = project_in(xp, w_in[li], kidx_ln_g[li], kidx_ln_b[li])
        o_a, o_b = prompt_mixers(pr, rb_a, rb_b)
        xp1 = attn_merge(xp, o_a, o_b, pr['gates'], w_branch_a[li], w_branch_b[li], w_out[li], ln1_g[li], ln1_b[li])
        xp = channel_sublayer(xp1, p_prompt[li], w_router[li], b_router[li], w_gate_up[li], b_gate_up[li],
                              w_down[li], b_down[li], w_ple_gate[li], w_ple_proj[li], ln2_g[li], ln2_b[li])
        rows_p.append((pr['k_a'], pr['v_a'], pr['k_i'], pr['k_b'], pr['v_b']))
        sm = project_in(xs, w_in[li], kidx_ln_g[li], kidx_ln_b[li])
        o_a, o_b = sample_mixers(sm, cache_k_a[li], cache_v_a[li], cache_kidx[li], cache_k_b[li], cache_v_b[li],
                                 page_table, rb_a, rb_b)
        xs1 = attn_merge(xs, o_a, o_b, sm['gates'], w_branch_a[li], w_branch_b[li], w_out[li], ln1_g[li], ln1_b[li])
        xs = channel_sublayer(xs1, p_sample[li], w_router[li], b_router[li], w_gate_up[li], b_gate_up[li],
                              w_down[li], b_down[li], w_ple_gate[li], w_ple_proj[li], ln2_g[li], ln2_b[li])
        rows_s.append((sm['k_a'], sm['v_a'], sm['k_i'], sm['k_b'], sm['v_b']))
    k_a_prompt = jnp.stack([r[0] for r in rows_p])
    v_a_prompt = jnp.stack([r[1] for r in rows_p])
    kidx_prompt = jnp.stack([r[2] for r in rows_p])
    k_b_prompt = jnp.stack([r[3] for r in rows_p])
    v_b_prompt = jnp.stack([r[4] for r in rows_p])
    k_a_sample = jnp.stack([r[0] for r in rows_s])
    v_a_sample = jnp.stack([r[1] for r in rows_s])
    kidx_sample = jnp.stack([r[2] for r in rows_s])
    k_b_sample = jnp.stack([r[3] for r in rows_s])
    v_b_sample = jnp.stack([r[4] for r in rows_s])
    return (xp, xs, k_a_prompt, v_a_prompt, kidx_prompt, k_b_prompt, v_b_prompt,
            k_a_sample, v_a_sample, kidx_sample, k_b_sample, v_b_sample)
```

```python
import functools
import math

import numpy as np
import jax
import jax.numpy as jnp
from jax import lax
from jax.experimental import pallas as pl
from jax.experimental.pallas import tpu as pltpu

F32 = jnp.float32
BF16 = jnp.bfloat16
I32 = jnp.int32

D_MODEL = 1024
HEAD_DIM = 64
N_HEADS = 8
D_ATT = N_HEADS * HEAD_DIM
IDX_HEADS = 8
IDX_DIM = 64
IDX_TOPK = 256
MOBA_BLOCK = 256
MOBA_TOPK = 3
N_BUCKETS = 32
MAX_DISTANCE = 128
N_EXPERTS = 32
TOP_K = 4
D_EXPERT = D_MODEL
SWIGLU_LIMIT = 7.0
SWIGLU_ALPHA = 1.702
PLE_DIM = 256
LN_EPS = 1e-5
PAGE_SIZE = 128
DEPTH = 2
ALPHA_DN = (2 * DEPTH) ** 0.25
ATTN_SCALE = HEAD_DIM ** -0.5
IDX_SCALE = IDX_DIM ** -0.5
IN_SIZES = (D_ATT, D_ATT, D_ATT, IDX_HEADS * IDX_DIM, IDX_DIM, IDX_HEADS, D_ATT, D_ATT, D_ATT, 2 * D_MODEL)
IN_OFFS = tuple(int(v) for v in np.cumsum((0,) + IN_SIZES))

INT_MIN = -(2 ** 31)
NEG_BIG = -1e30
VMEM_LIMIT_BYTES = 56 * 1024 * 1024


def _bucket_of(d):
    d = max(d, 0)
    max_exact = N_BUCKETS // 2
    if d < max_exact:
        return d
    ratio = math.log(d / max_exact) / math.log(MAX_DISTANCE / max_exact)
    return min(max_exact + int(ratio * (N_BUCKETS - max_exact)), N_BUCKETS - 1)


_BUCKET_HI = tuple(max(d for d in range(4 * MAX_DISTANCE) if _bucket_of(d) == b) for b in range(N_BUCKETS - 1))
FAR_DIST = _BUCKET_HI[-1] + 1


def _cparams(sem):
    return pltpu.CompilerParams(dimension_semantics=sem, vmem_limit_bytes=VMEM_LIMIT_BYTES)


def _split(x):
    hi = x.astype(BF16)
    lo = (x - hi.astype(F32)).astype(BF16)
    return hi, lo


_NN = (((1,), (0,)), ((), ()))
_NT = (((1,), (1,)), ((), ()))


def _dot(a, b, dims=_NN):
    return lax.dot_general(a, b, dims, preferred_element_type=F32)


def _dot3(a, b, dims=_NN):
    ah, al = _split(a)
    bh, bl = _split(b)
    return _dot(al, bh, dims) + _dot(ah, bl, dims) + _dot(ah, bh, dims)


def _layer_norm(x, g, b):
    mu = jnp.mean(x, axis=-1, keepdims=True)
    xc = x - mu
    var = jnp.mean(xc * xc, axis=-1, keepdims=True)
    return xc * lax.rsqrt(var + LN_EPS) * g + b


def _order_key(s):
    s = jnp.where(s == 0.0, 0.0, s)
    u = pltpu.bitcast(s, I32)
    return u ^ (jnp.right_shift(u, 31) & 0x7FFFFFFF)


def _bias_chain(d, rb_of):
    val = rb_of(N_BUCKETS - 1)
    for b in range(N_BUCKETS - 2, -1, -1):
        val = jnp.where(d <= _BUCKET_HI[b], rb_of(b), val)
    return val


def _mm_body(x_ref, w_ref, o_ref, *, passes, act):
    x = x_ref[...]
    w = w_ref[...]
    y = _dot(x.astype(BF16), w.astype(BF16)) if passes == 1 else _dot3(x, w)
    if act == "sigmoid":
        y = jax.nn.sigmoid(y)
    o_ref[...] = y


def _mm(x, w, *, passes=1, act=None):
    M, K = x.shape
    N = w.shape[1]
    tm = min(M, 512)
    tn = min(N, 512)
    return pl.pallas_call(
        functools.partial(_mm_body, passes=passes, act=act),
        grid=(M // tm, N // tn),
        in_specs=[pl.BlockSpec((tm, K), lambda i, j: (i, 0)),
                  pl.BlockSpec((K, tn), lambda i, j: (0, j))],
        out_specs=pl.BlockSpec((tm, tn), lambda i, j: (i, j)),
        out_shape=jax.ShapeDtypeStruct((M, N), F32),
        compiler_params=_cparams(("parallel", "parallel")),
        name="proj_mm",
    )(x, w)


def _proj_idx_body(x_ref, wq_ref, wk_ref, ww_ref, g_ref, b_ref, qi_ref, ki_ref, kcat_ref, wi_ref):
    x = x_ref[...]
    qi_ref[...] = _dot3(x, wq_ref[...])
    k = _layer_norm(_dot3(x, wk_ref[...]), g_ref[...], b_ref[...])
    ki_ref[...] = k
    kh, kl = _split(k)
    kcat_ref[...] = jnp.concatenate([kh, kl, kh, jnp.zeros_like(kh)], axis=-1)
    wi_ref[...] = _dot3(x, ww_ref[...]) * (IDX_HEADS ** -0.5)


def _proj_idx(x, wq, wk, ww, g, b):
    M, K = x.shape
    tm = min(M, 512)
    full = lambda a: pl.BlockSpec(a.shape, lambda i: (0,) * a.ndim)
    row = lambda n: pl.BlockSpec((tm, n), lambda i: (i, 0))
    return pl.pallas_call(
        _proj_idx_body,
        grid=(M // tm,),
        in_specs=[row(K), full(wq), full(wk), full(ww), full(g), full(b)],
        out_specs=[row(IDX_HEADS * IDX_DIM), row(IDX_DIM), row(4 * IDX_DIM), row(IDX_HEADS)],
        out_shape=[jax.ShapeDtypeStruct((M, IDX_HEADS * IDX_DIM), F32),
                   jax.ShapeDtypeStruct((M, IDX_DIM), F32),
                   jax.ShapeDtypeStruct((M, 4 * IDX_DIM), BF16),
                   jax.ShapeDtypeStruct((M, IDX_HEADS), F32)],
        compiler_params=_cparams(("parallel",)),
        name="proj_idx",
    )(x, wq, wk, ww, g, b)


def _bias_tiles_body(rb_ref, o_ref, *, T):
    rel = pl.program_id(0)
    h = pl.program_id(1)
    row = lax.broadcasted_iota(I32, (T, T), 0)
    col = lax.broadcasted_iota(I32, (T, T), 1)
    d = row - col + rel * T
    o_ref[...] = _bias_chain(d, lambda b: rb_ref[b, h])


def _bias_tiles(rb, T):
    H = rb.shape[1]
    return pl.pallas_call(
        functools.partial(_bias_tiles_body, T=T),
        grid=(2, H),
        in_specs=[pl.BlockSpec(memory_space=pltpu.SMEM)],
        out_specs=pl.BlockSpec((None, None, T, T), lambda r, h: (r, h, 0, 0)),
        out_shape=jax.ShapeDtypeStruct((2, H, T, T), F32),
        compiler_params=_cparams(("parallel", "parallel")),
        name="bias_tiles",
    )(rb)


def _attn_update(h, s, mask, vT, m_ref, l_ref, acc_ref):
    s = jnp.where(mask, s, NEG_BIG)
    m_old = m_ref[h]
    m_new = jnp.maximum(m_old, jnp.max(s, axis=-1, keepdims=True))
    p = jnp.where(mask, jnp.exp(s - m_new), 0.0)
    alpha = jnp.exp(m_old - m_new)
    l_ref[h] = alpha * l_ref[h] + jnp.sum(p, axis=-1, keepdims=True)
    acc_ref[h] = alpha * acc_ref[h] + _dot(p.astype(BF16), vT, _NT)
    m_ref[h] = m_new


def _attn_init(m_ref, l_ref, acc_ref):
    m_ref[...] = jnp.full(m_ref.shape, NEG_BIG, F32)
    l_ref[...] = jnp.zeros(l_ref.shape, F32)
    acc_ref[...] = jnp.zeros(acc_ref.shape, F32)


def _attn_finish(o_ref, l_ref, acc_ref):
    for h in range(N_HEADS):
        o_ref[:, h * HEAD_DIM:(h + 1) * HEAD_DIM] = acc_ref[h] / l_ref[h]


def _dsa_prompt_body(rbf_ref, bt_ref, qa_ref, kT_ref, vT_ref, qi_ref, wi_ref, kcat_ref, o_ref,
                     key_ref, m_ref, l_ref, acc_ref, *, T, n_top, idx_bits):
    i = pl.program_id(1)
    H = N_HEADS
    row = lax.broadcasted_iota(I32, (T, T), 0)
    col = lax.broadcasted_iota(I32, (T, T), 1)

    wi = wi_ref[...] * IDX_SCALE
    qcat = []
    for h in range(IDX_HEADS):
        qh, ql = _split(qi_ref[h])
        qcat.append(jnp.concatenate([qh, qh, ql, jnp.zeros_like(qh)], axis=-1))

    def score_chunk(kc, carry):
        kcat = kcat_ref[pl.ds(pl.multiple_of(kc * T, T), T), :]
        s = jnp.zeros((T, T), F32)
        for h in range(IDX_HEADS):
            s = s + jnp.maximum(_dot(qcat[h], kcat, _NT), 0.0) * wi[:, h:h + 1]
        key = _order_key(s)
        key_ref[kc] = jnp.where((kc == i) & (col > row), INT_MIN, key)
        return carry

    lax.fori_loop(0, i + 1, score_chunk, 0)

    def count(pred):
        def body(kc, acc):
            hit = pred(key_ref[kc], kc * T + col)
            return acc + jnp.sum(hit.astype(I32), axis=-1, keepdims=True)
        return lax.fori_loop(0, i + 1, body, jnp.zeros((T, 1), I32))

    def thr_bit(bi, prefix):
        cand = prefix | jnp.left_shift(jnp.int32(1), 31 - bi)
        cs = cand ^ INT_MIN
        return jnp.where(count(lambda k, g: k >= cs) >= n_top, cand, prefix)

    thr = lax.fori_loop(0, 32, thr_bit, jnp.zeros((T, 1), I32)) ^ INT_MIN
    need = n_top - count(lambda k, g: k > thr)
    n_tie = count(lambda k, g: k == thr)

    def tie_search():
        def bit(bi, j0):
            cand = j0 | jnp.left_shift(jnp.int32(1), idx_bits - 1 - bi)
            c = count(lambda k, g: (k == thr) & (g < cand))
            return jnp.where(c < need, cand, j0)
        return lax.fori_loop(0, idx_bits, bit, jnp.zeros((T, 1), I32))

    any_excess = jnp.max(n_tie - need) > 0
    jcut = lax.cond(any_excess, tie_search, lambda: jnp.full((T, 1), 2 ** 30, I32))

    _attn_init(m_ref, l_ref, acc_ref)
    q = [(qa_ref[h] * ATTN_SCALE).astype(BF16) for h in range(H)]

    def attn_chunk(kc, carry):
        k = key_ref[kc]
        g = kc * T + col
        sel = ((k > thr) | ((k == thr) & (g <= jcut))) & (g <= i * T + row)
        rel = i - kc
        near = jnp.minimum(rel, 1)
        for h in range(H):
            s = _dot(q[h], kT_ref[h, kc])
            s = s + jnp.where(rel >= 2, rbf_ref[h], bt_ref[near, h])
            _attn_update(h, s, sel, vT_ref[h, kc], m_ref, l_ref, acc_ref)
        return carry

    lax.fori_loop(0, i + 1, attn_chunk, 0)
    _attn_finish(o_ref, l_ref, acc_ref)


def _attn_scratch(T):
    return [pltpu.VMEM((N_HEADS, T, 1), F32), pltpu.VMEM((N_HEADS, T, 1), F32),
            pltpu.VMEM((N_HEADS, T, HEAD_DIM), F32)]


def _dsa_prompt(rb_far, btiles, qa, kT, vT, qi, wi, kcat, *, T):
    B, H, S, _ = qa.shape
    nc = S // T
    n_top = min(IDX_TOPK, S // 4)
    body = functools.partial(_dsa_prompt_body, T=T, n_top=n_top, idx_bits=max(1, (S - 1).bit_length()))
    return pl.pallas_call(
        body,
        grid=(B, nc),
        in_specs=[pl.BlockSpec(memory_space=pltpu.SMEM),
                  pl.BlockSpec((2, H, T, T), lambda b, i: (0, 0, 0, 0)),
                  pl.BlockSpec((None, H, T, HEAD_DIM), lambda b, i: (b, 0, i, 0)),
                  pl.BlockSpec((None, H, nc, HEAD_DIM, T), lambda b, i: (b, 0, 0, 0, 0)),
                  pl.BlockSpec((None, H, nc, HEAD_DIM, T), lambda b, i: (b, 0, 0, 0, 0)),
                  pl.BlockSpec((None, IDX_HEADS, T, IDX_DIM), lambda b, i: (b, 0, i, 0)),
                  pl.BlockSpec((None, T, IDX_HEADS), lambda b, i: (b, i, 0)),
                  pl.BlockSpec((None, S, 4 * IDX_DIM), lambda b, i: (b, 0, 0))],
        out_specs=pl.BlockSpec((None, T, D_ATT), lambda b, i: (b, i, 0)),
        out_shape=jax.ShapeDtypeStruct((B, S, D_ATT), F32),
        scratch_shapes=[pltpu.VMEM((nc, T, T), I32)] + _attn_scratch(T),
        compiler_params=_cparams(("parallel", "arbitrary")),
        name="dsa_prompt",
    )(rb_far, btiles, qa, kT, vT, qi, wi, kcat)


def _block_means_body(k_ref, o_ref):
    o_ref[...] = jnp.sum(k_ref[...], axis=0, keepdims=True) * (1.0 / MOBA_BLOCK)


def _block_means(k):
    B, L, D = k.shape
    nf = L // MOBA_BLOCK
    return pl.pallas_call(
        _block_means_body,
        grid=(B, nf),
        in_specs=[pl.BlockSpec((None, MOBA_BLOCK, D), lambda b, j: (b, j, 0))],
        out_specs=pl.BlockSpec((None, None, 1, D), lambda b, j: (b, j, 0, 0)),
        out_shape=jax.ShapeDtypeStruct((B, nf, 1, D), F32),
        compiler_params=_cparams(("parallel", "parallel")),
        name="block_means",
    )(k)


def _moba_prompt_body(rbf_ref, bt_ref, qb_ref, kT_ref, vT_ref, qf_ref, mbd_ref, o_ref,
                      m_ref, l_ref, acc_ref, *, T, n_sel, n_slots):
    i = pl.program_id(1)
    H = N_HEADS
    W = n_slots * H
    row = lax.broadcasted_iota(I32, (T, T), 0)
    col = lax.broadcasted_iota(I32, (T, T), 1)

    g = _dot3(qf_ref[...], mbd_ref[...])
    blk = jnp.right_shift(lax.broadcasted_iota(I32, (T, W), 1), int(math.log2(H)))
    g = jnp.where(blk < i, g, -jnp.inf)
    rank = jnp.zeros((T, W), I32)
    for r in range(1, n_slots):
        other = pltpu.roll(g, r * H, axis=1)
        rank = rank + jnp.where(blk >= r, (other >= g).astype(I32), (other > g).astype(I32))
    chosen = ((rank < n_sel) & (blk < i)).astype(F32)

    _attn_init(m_ref, l_ref, acc_ref)
    q = [(qb_ref[h] * ATTN_SCALE).astype(BF16) for h in range(H)]
    causal = (col <= row).astype(F32)

    def attn_chunk(kc, carry):
        heads = pltpu.roll(chosen, lax.rem(W - kc * H, W), axis=1)
        rel = i - kc
        near = jnp.minimum(rel, 1)
        for h in range(H):
            mask = jnp.where(rel == 0, causal, heads[:, h:h + 1]) > 0.5
            s = _dot(q[h], kT_ref[h, kc])
            s = s + jnp.where(rel >= 2, rbf_ref[h], bt_ref[near, h])
            _attn_update(h, s, mask, vT_ref[h, kc], m_ref, l_ref, acc_ref)
        return carry

    lax.fori_loop(0, i + 1, attn_chunk, 0)
    _attn_finish(o_ref, l_ref, acc_ref)


def _moba_prompt(rb_far, btiles, qb, kT, vT, qf, mbd):
    B, H, S, _ = qb.shape
    T = MOBA_BLOCK
    nc = S // T
    n_slots = mbd.shape[-1] // H
    body = functools.partial(_moba_prompt_body, T=T, n_sel=min(MOBA_TOPK, nc), n_slots=n_slots)
    return pl.pallas_call(
        body,
        grid=(B, nc),
        in_specs=[pl.BlockSpec(memory_space=pltpu.SMEM),
                  pl.BlockSpec((2, H, T, T), lambda b, i: (0, 0, 0, 0)),
                  pl.BlockSpec((None, H, T, HEAD_DIM), lambda b, i: (b, 0, i, 0)),
                  pl.BlockSpec((None, H, nc, HEAD_DIM, T), lambda b, i: (b, 0, 0, 0, 0)),
                  pl.BlockSpec((None, H, nc, HEAD_DIM, T), lambda b, i: (b, 0, 0, 0, 0)),
                  pl.BlockSpec((None, T, D_ATT), lambda b, i: (b, i, 0)),
                  pl.BlockSpec((None, D_ATT, n_slots * H), lambda b, i: (b, 0, 0))],
        out_specs=pl.BlockSpec((None, T, D_ATT), lambda b, i: (b, i, 0)),
        out_shape=jax.ShapeDtypeStruct((B, S, D_ATT), F32),
        scratch_shapes=_attn_scratch(T),
        compiler_params=_cparams(("parallel", "arbitrary")),
        name="moba_prompt",
    )(rb_far, btiles, qb, kT, vT, qf, mbd)


def _merge_body(x_ref, oa_ref, ob_ref, g_ref, p_ref, wba_ref, wbb_ref, wo_ref, g1_ref, b1_ref,
                wr_ref, br_ref, wpg_ref, wpp_ref, x1_ref, comb_ref, ple_ref):
    bra = _dot(oa_ref[...].astype(BF16), wba_ref[...])
    brb = _dot(ob_ref[...].astype(BF16), wbb_ref[...])
    gates = g_ref[...]
    mix = gates[:, :D_MODEL] * bra + gates[:, D_MODEL:] * brb
    y = _dot(mix.astype(BF16), wo_ref[...])
    x1 = _layer_norm(ALPHA_DN * x_ref[...] + y, g1_ref[...], b1_ref[...])
    x1_ref[...] = x1

    logits = _dot3(x1, wr_ref[...]) + br_ref[...]
    lane = lax.broadcasted_iota(I32, logits.shape, 1)
    work = logits
    kept = jnp.zeros(logits.shape, jnp.bool_)
    for _ in range(TOP_K):
        mx = jnp.max(work, axis=-1, keepdims=True)
        first = jnp.min(jnp.where(work == mx, lane, N_EXPERTS), axis=-1, keepdims=True)
        hit = lane == first
        kept = kept | hit
        work = jnp.where(hit, -jnp.inf, work)
    top = jnp.max(logits, axis=-1, keepdims=True)
    e = jnp.where(kept, jnp.exp(logits - top), 0.0)
    comb_ref[...] = e / jnp.sum(e, axis=-1, keepdims=True)

    x1b = x1.astype(BF16)
    ple_ref[...] = jax.nn.sigmoid(_dot(x1b, wpg_ref[...])) * _dot(p_ref[...].astype(BF16), wpp_ref[...])


def _merge(x, oa, ob, gates, p, wba, wbb, wo, g1, b1, wr, br, wpg, wpp):
    M = x.shape[0]
    tm = min(M, 512)
    full = lambda a: pl.BlockSpec(a.shape, lambda i: (0,) * a.ndim)
    row = lambda n: pl.BlockSpec((tm, n), lambda i: (i, 0))
    return pl.pallas_call(
        _merge_body,
        grid=(M // tm,),
        in_specs=[row(D_MODEL), row(D_ATT), row(D_ATT), row(2 * D_MODEL), row(PLE_DIM),
                  full(wba), full(wbb), full(wo), full(g1), full(b1), full(wr), full(br), full(wpg), full(wpp)],
        out_specs=[row(D_MODEL), row(N_EXPERTS), row(D_MODEL)],
        out_shape=[jax.ShapeDtypeStruct((M, D_MODEL), F32),
                   jax.ShapeDtypeStruct((M, N_EXPERTS), F32),
                   jax.ShapeDtypeStruct((M, D_MODEL), F32)],
        compiler_params=_cparams(("parallel",)),
        name="merge",
    )(x, oa, ob, gates, p, wba, wbb, wo, g1, b1, wr, br, wpg, wpp)


MOE_FC = 512


def _moe_body(x1_ref, comb_ref, ple_ref, wg_ref, wu_ref, bg_ref, bu_ref, wd_ref, bd_ref, g2_ref, b2_ref,
              o_ref, acc_ref):
    e = pl.program_id(1)
    f = pl.program_id(2)

    @pl.when((e == 0) & (f == 0))
    def _():
        acc_ref[...] = jnp.zeros(acc_ref.shape, F32)

    comb = comb_ref[...]
    lane = lax.broadcasted_iota(I32, comb.shape, 1)
    ce = jnp.sum(jnp.where(lane == e, comb, 0.0), axis=-1, keepdims=True)
    xb = x1_ref[...].astype(BF16)
    gt = jnp.minimum(_dot(xb, wg_ref[...].astype(BF16)) + bg_ref[...], SWIGLU_LIMIT)
    up = jnp.clip(_dot(xb, wu_ref[...].astype(BF16)) + bu_ref[...], -SWIGLU_LIMIT, SWIGLU_LIMIT)
    hid = (up + 1.0) * gt * jax.nn.sigmoid(SWIGLU_ALPHA * gt) * ce
    acc_ref[...] += _dot(hid.astype(BF16), wd_ref[...].astype(BF16))

    @pl.when((e == pl.num_programs(1) - 1) & (f == pl.num_programs(2) - 1))
    def _():
        y = acc_ref[...] + _dot3(comb, bd_ref[...])
        o_ref[...] = _layer_norm(ALPHA_DN * x1_ref[...] + y + ple_ref[...], g2_ref[...], b2_ref[...])


def _moe(x1, comb, ple, w_gu, b_gu, w_dn, b_dn, g2, b2, li):
    M = x1.shape[0]
    tm = min(M, 512)
    E = w_gu.shape[1]
    nf = D_EXPERT // MOE_FC
    b_gu4 = b_gu.reshape(b_gu.shape[0], E, 1, 2 * D_EXPERT)
    row = lambda n: pl.BlockSpec((tm, n), lambda i, e, f: (i, 0))
    return pl.pallas_call(
        _moe_body,
        grid=(M // tm, E, nf),
        in_specs=[row(D_MODEL), row(N_EXPERTS), row(D_MODEL),
                  pl.BlockSpec((None, None, D_MODEL, MOE_FC), lambda i, e, f: (li, e, 0, f)),
                  pl.BlockSpec((None, None, D_MODEL, MOE_FC), lambda i, e, f: (li, e, 0, nf + f)),
                  pl.BlockSpec((None, None, 1, MOE_FC), lambda i, e, f: (li, e, 0, f)),
                  pl.BlockSpec((None, None, 1, MOE_FC), lambda i, e, f: (li, e, 0, nf + f)),
                  pl.BlockSpec((None, None, MOE_FC, D_MODEL), lambda i, e, f: (li, e, f, 0)),
                  pl.BlockSpec((None, E, D_MODEL), lambda i, e, f: (li, 0, 0)),
                  pl.BlockSpec((1, D_MODEL), lambda i, e, f: (0, 0)),
                  pl.BlockSpec((1, D_MODEL), lambda i, e, f: (0, 0))],
        out_specs=row(D_MODEL),
        out_shape=jax.ShapeDtypeStruct((M, D_MODEL), F32),
        scratch_shapes=[pltpu.VMEM((tm, D_MODEL), F32)],
        compiler_params=_cparams(("parallel", "arbitrary", "arbitrary")),
        name="moe",
    )(x1, comb, ple, w_gu, w_gu, b_gu4, b_gu4, w_dn, b_dn, g2, b2)


SEL_PAGES = 16
ATT_PAGES = 8


def _page_specs(n, width, li):
    def spec(u):
        return pl.BlockSpec((None, None, PAGE_SIZE, width),
                            lambda b, c, pt: (li, pt[b, c * n + u], 0, 0))
    return [spec(u) for u in range(n)]


def _smp_dsa_select_body(pt_ref, qi_ref, wi_ref, knew_ref, *rest, n_pages, n_top, idx_bits):
    pages = rest[:SEL_PAGES]
    sel_ref = rest[SEL_PAGES]
    key_ref = rest[SEL_PAGES + 1]
    c = pl.program_id(1)
    T = qi_ref.shape[0] // IDX_HEADS
    qi = qi_ref[...]
    wi = wi_ref[...] * IDX_SCALE

    def page_score(kp):
        s = jnp.maximum(_dot3(qi, kp, _NT), 0.0) * wi
        return jnp.sum(s.reshape(IDX_HEADS, T, PAGE_SIZE), axis=0)

    for u in range(SEL_PAGES):
        key_ref[c * SEL_PAGES + u] = _order_key(page_score(pages[u][...]))

    @pl.when(c == pl.num_programs(1) - 1)
    def _():
        qrow = lax.broadcasted_iota(I32, (T, PAGE_SIZE), 0)
        lane = lax.broadcasted_iota(I32, (T, PAGE_SIZE), 1)
        key_ref[n_pages] = jnp.where(lane <= qrow, _order_key(page_score(knew_ref[...])), INT_MIN)
        keys = key_ref[...]
        shape = keys.shape
        gidx = lax.broadcasted_iota(I32, shape, 0) * PAGE_SIZE + lax.broadcasted_iota(I32, shape, 2)
        valid = (lax.broadcasted_iota(I32, shape, 0) < n_pages) | \
                (lax.broadcasted_iota(I32, shape, 2) <= lax.broadcasted_iota(I32, shape, 1))

        def count(hit):
            per_lane = jnp.sum(hit.astype(I32), axis=0)
            return jnp.sum(per_lane, axis=-1, keepdims=True)[None]

        def thr_bit(bi, prefix):
            cand = prefix | jnp.left_shift(jnp.int32(1), 31 - bi)
            cs = cand ^ INT_MIN
            return jnp.where(count(keys >= cs) >= n_top, cand, prefix)

        thr = lax.fori_loop(0, 32, thr_bit, jnp.zeros((1, T, 1), I32)) ^ INT_MIN
        need = n_top - count(keys > thr)

        def tie_bit(bi, j0):
            cand = j0 | jnp.left_shift(jnp.int32(1), idx_bits - 1 - bi)
            return jnp.where(count((keys == thr) & (gidx < cand)) < need, cand, j0)

        jcut = lax.fori_loop(0, idx_bits, tie_bit, jnp.zeros((1, T, 1), I32))
        sel = ((keys > thr) | ((keys == thr) & (gidx <= jcut))) & valid
        sel_ref[...] = sel.astype(F32)


def _smp_dsa_select(page_table, qi_rows, wi_rows, knew, cache_kidx, li):
    DB, P = page_table.shape
    T = qi_rows.shape[1] // IDX_HEADS
    L = P * PAGE_SIZE + T
    body = functools.partial(_smp_dsa_select_body, n_pages=P, n_top=min(IDX_TOPK, L // 4),
                             idx_bits=max(1, ((P + 1) * PAGE_SIZE - 1).bit_length()))
    grid_spec = pltpu.PrefetchScalarGridSpec(
        num_scalar_prefetch=1,
        grid=(DB, P // SEL_PAGES),
        in_specs=[pl.BlockSpec((None, IDX_HEADS * T, IDX_DIM), lambda b, c, pt: (b, 0, 0)),
                  pl.BlockSpec((None, IDX_HEADS * T, 1), lambda b, c, pt: (b, 0, 0)),
                  pl.BlockSpec((None, PAGE_SIZE, IDX_DIM), lambda b, c, pt: (b, 0, 0))]
                 + _page_specs(SEL_PAGES, IDX_DIM, li),
        out_specs=pl.BlockSpec((None, P + 1, T, PAGE_SIZE), lambda b, c, pt: (b, 0, 0, 0)),
        scratch_shapes=[pltpu.VMEM((P + 1, T, PAGE_SIZE), I32)],
    )
    return pl.pallas_call(
        body,
        grid_spec=grid_spec,
        out_shape=jax.ShapeDtypeStruct((DB, P + 1, T, PAGE_SIZE), F32),
        compiler_params=_cparams(("parallel", "arbitrary")),
        name="smp_dsa_select",
    )(page_table, qi_rows, wi_rows, knew, *([cache_kidx] * SEL_PAGES))


def _smp_moba_select_body(pt_ref, qbd_ref, *rest, n_blocks, n_sel):
    pages = rest[:SEL_PAGES]
    sel_ref = rest[SEL_PAGES]
    mean_ref = rest[SEL_PAGES + 1]
    c = pl.program_id(1)
    per_step = SEL_PAGES * PAGE_SIZE // MOBA_BLOCK
    per_block = MOBA_BLOCK // PAGE_SIZE

    @pl.when(c == 0)
    def _():
        mean_ref[...] = jnp.zeros(mean_ref.shape, F32)

    for j in range(per_step):
        tot = pages[j * per_block][...]
        for u in range(1, per_block):
            tot = tot + pages[j * per_block + u][...]
        mean_ref[c, j:j + 1, :] = jnp.sum(tot, axis=0, keepdims=True) * (1.0 / MOBA_BLOCK)

    @pl.when(c == pl.num_programs(1) - 1)
    def _():
        means = mean_ref[...].reshape(-1, mean_ref.shape[-1])
        g = _dot3(qbd_ref[...], means, _NT)
        W = g.shape[1]
        lane = lax.broadcasted_iota(I32, g.shape, 1)
        g = jnp.where(lane < n_blocks, g, -jnp.inf)
        rank = jnp.zeros(g.shape, I32)
        for r in range(1, n_blocks):
            lower = pltpu.roll(g, r, axis=1)
            upper = pltpu.roll(g, W - r, axis=1)
            rank = rank + (lower >= g).astype(I32) + (upper > g).astype(I32)
        sel_ref[...] = ((rank < n_sel) & (lane < n_blocks)).astype(F32)


def _smp_moba_select(page_table, qbd, cache_k, li):
    DB, P = page_table.shape
    n_blocks = P * PAGE_SIZE // MOBA_BLOCK
    per_step = SEL_PAGES * PAGE_SIZE // MOBA_BLOCK
    n_steps = P // SEL_PAGES
    W = 128
    assert n_blocks < W and per_step == 8
    R = qbd.shape[1]
    body = functools.partial(_smp_moba_select_body, n_blocks=n_blocks, n_sel=min(MOBA_TOPK, n_blocks))
    grid_spec = pltpu.PrefetchScalarGridSpec(
        num_scalar_prefetch=1,
        grid=(DB, n_steps),
        in_specs=[pl.BlockSpec((None, R, D_ATT), lambda b, c, pt: (b, 0, 0))]
                 + _page_specs(SEL_PAGES, D_ATT, li),
        out_specs=pl.BlockSpec((None, R, W), lambda b, c, pt: (b, 0, 0)),
        scratch_shapes=[pltpu.VMEM((W // per_step, per_step, D_ATT), F32)],
    )
    return pl.pallas_call(
        body,
        grid_spec=grid_spec,
        out_shape=jax.ShapeDtypeStruct((DB, R, W), F32),
        compiler_params=_cparams(("parallel", "arbitrary")),
        name="smp_moba_select",
    )(page_table, qbd, *([cache_k] * SEL_PAGES))


def _smp_attn_body(pt_ref, qbd_ref, rbr_ref, knew_ref, vnew_ref, sel_ref, selnew_ref, *rest, mode, n_pages):
    kp = rest[:ATT_PAGES]
    vp = rest[ATT_PAGES:2 * ATT_PAGES]
    o_ref, m_ref, l_ref, acc_ref = rest[2 * ATT_PAGES:]
    c = pl.program_id(1)
    R = qbd_ref.shape[0]
    T = R // N_HEADS
    q = (qbd_ref[...] * ATTN_SCALE).astype(BF16)
    rbr = rbr_ref[...]
    rowq = lax.broadcasted_iota(I32, (R, PAGE_SIZE), 0) & (T - 1)
    lane = lax.broadcasted_iota(I32, (R, PAGE_SIZE), 1)

    @pl.when(c == 0)
    def _():
        m_ref[...] = jnp.full(m_ref.shape, NEG_BIG, F32)
        l_ref[...] = jnp.zeros(l_ref.shape, F32)
        acc_ref[...] = jnp.zeros(acc_ref.shape, F32)

    def bias_for(dist):
        return _bias_chain(dist, lambda b: rbr[:, b:b + 1])

    def update(s, mask, v):
        s = jnp.where(mask, s, NEG_BIG)
        m_old = m_ref[...]
        m_new = jnp.maximum(m_old, jnp.max(s, axis=-1, keepdims=True))
        p = jnp.where(mask, jnp.exp(s - m_new), 0.0)
        alpha = jnp.exp(m_old - m_new)
        l_ref[...] = alpha * l_ref[...] + jnp.sum(p, axis=-1, keepdims=True)
        acc_ref[...] = alpha * acc_ref[...] + _dot(p.astype(BF16), v)
        m_ref[...] = m_new

    logits, masks, vals = [], [], []
    for u in range(ATT_PAGES):
        page = c * ATT_PAGES + u
        s = _dot(q, kp[u][...].astype(BF16), _NT)
        dist = (n_pages - page) * PAGE_SIZE + rowq - lane
        if u == ATT_PAGES - 1:
            bias = lax.cond(c == pl.num_programs(1) - 1, lambda: bias_for(dist),
                            lambda: jnp.broadcast_to(rbr[:, N_BUCKETS - 1:], (R, PAGE_SIZE)))
        else:
            bias = rbr[:, N_BUCKETS - 1:]
        logits.append(s + bias)
        if mode == "dsa":
            masks.append(jnp.tile(sel_ref[u], (N_HEADS, 1)) > 0.5)
        else:
            blk = page // (MOBA_BLOCK // PAGE_SIZE)
            sel = sel_ref[...]
            pick = jnp.sum(jnp.where(lax.broadcasted_iota(I32, sel.shape, 1) == blk, sel, 0.0),
                           axis=-1, keepdims=True)
            masks.append(jnp.broadcast_to(pick > 0.5, (R, PAGE_SIZE)))
        vals.append(vp[u][...].astype(BF16))
    update(jnp.concatenate(logits, axis=1), jnp.concatenate(masks, axis=1), jnp.concatenate(vals, axis=0))

    @pl.when(c == pl.num_programs(1) - 1)
    def _():
        s = _dot(q, knew_ref[...].astype(BF16), _NT) + bias_for(rowq - lane)
        mask = lane <= rowq
        if mode == "dsa":
            mask = mask & (jnp.tile(selnew_ref[...], (N_HEADS, 1)) > 0.5)
        update(s, mask, vnew_ref[...].astype(BF16))
        out = acc_ref[...] / l_ref[...]
        for h in range(N_HEADS):
            o_ref[:, h * HEAD_DIM:(h + 1) * HEAD_DIM] = out[h * T:(h + 1) * T, h * HEAD_DIM:(h + 1) * HEAD_DIM]


def _smp_attn(page_table, qbd, rbr, knew, vnew, sel, cache_k, cache_v, li, *, mode):
    DB, P = page_table.shape
    R = qbd.shape[1]
    T = R // N_HEADS
    if mode == "dsa":
        sel_specs = [pl.BlockSpec((None, ATT_PAGES, T, PAGE_SIZE), lambda b, c, pt: (b, c, 0, 0)),
                     pl.BlockSpec((None, None, T, PAGE_SIZE), lambda b, c, pt: (b, P, 0, 0))]
    else:
        sel_specs = [pl.BlockSpec((None, R, sel.shape[-1]), lambda b, c, pt: (b, 0, 0)),
                     pl.BlockSpec((None, R, sel.shape[-1]), lambda b, c, pt: (b, 0, 0))]
    grid_spec = pltpu.PrefetchScalarGridSpec(
        num_scalar_prefetch=1,
        grid=(DB, P // ATT_PAGES),
        in_specs=[pl.BlockSpec((None, R, D_ATT), lambda b, c, pt: (b, 0, 0)),
                  pl.BlockSpec(rbr.shape, lambda b, c, pt: (0, 0)),
                  pl.BlockSpec((None, PAGE_SIZE, D_ATT), lambda b, c, pt: (b, 0, 0)),
                  pl.BlockSpec((None, PAGE_SIZE, D_ATT), lambda b, c, pt: (b, 0, 0))]
                 + sel_specs + _page_specs(ATT_PAGES, D_ATT, li) + _page_specs(ATT_PAGES, D_ATT, li),
        out_specs=pl.BlockSpec((None, T, D_ATT), lambda b, c, pt: (b, 0, 0)),
        scratch_shapes=[pltpu.VMEM((R, 1), F32), pltpu.VMEM((R, 1), F32), pltpu.VMEM((R, D_ATT), F32)],
    )
    return pl.pallas_call(
        functools.partial(_smp_attn_body, mode=mode, n_pages=P),
        grid_spec=grid_spec,
        out_shape=jax.ShapeDtypeStruct((DB, T, D_ATT), F32),
        compiler_params=_cparams(("parallel", "arbitrary")),
        name="smp_attn_" + mode,
    )(page_table, qbd, rbr, knew, vnew, sel, sel, *([cache_k] * ATT_PAGES), *([cache_v] * ATT_PAGES))


def _project(x2, w, g, b):
    o = IN_OFFS
    qkv_a = _mm(x2, w[:, o[0]:o[3]])
    qi, ki, kcat, wi = _proj_idx(x2, w[:, o[3]:o[4]], w[:, o[4]:o[5]], w[:, o[5]:o[6]], g[None], b[None])
    qk_b = _mm(x2, w[:, o[6]:o[8]], passes=3)
    v_b = _mm(x2, w[:, o[8]:o[9]])
    gates = _mm(x2, w[:, o[9]:o[10]], act="sigmoid")
    return dict(q_a=qkv_a[:, :D_ATT], k_a=qkv_a[:, D_ATT:2 * D_ATT], v_a=qkv_a[:, 2 * D_ATT:],
                q_i=qi, k_i=ki, kcat=kcat, w_i=wi, q_b=qk_b[:, :D_ATT], k_b=qk_b[:, D_ATT:], v_b=v_b, gates=gates)


def _heads_major(a, B, S):
    return a.reshape(B, S, N_HEADS, HEAD_DIM).transpose(0, 2, 1, 3)


def _chunked_T(a, B, S, T):
    return a.astype(BF16).reshape(B, S // T, T, N_HEADS, HEAD_DIM).transpose(0, 3, 1, 4, 2)


def _block_diag_rows(q):
    DB, T, H, Dh = q.shape
    eye = jnp.eye(H, dtype=q.dtype)
    return jnp.einsum("bthd,hg->bhtgd", q, eye).reshape(DB, H * T, H * Dh)


def _prompt_mixers(pr, B, S, rb_a, rb_b, T_dsa):
    bt_a = _bias_tiles(rb_a, T_dsa)
    o_a = _dsa_prompt(rb_a[N_BUCKETS - 1], bt_a, _heads_major(pr["q_a"], B, S),
                      _chunked_T(pr["k_a"], B, S, T_dsa), _chunked_T(pr["v_a"], B, S, T_dsa),
                      _heads_major(pr["q_i"], B, S), pr["w_i"].reshape(B, S, IDX_HEADS),
                      pr["kcat"].reshape(B, S, 4 * IDX_DIM), T=T_dsa)
    T = MOBA_BLOCK
    nf = S // T
    n_slots = 128 // N_HEADS
    assert nf <= n_slots
    means = _block_means(pr["k_b"].reshape(B, S, D_ATT)).reshape(B, nf, N_HEADS, HEAD_DIM)
    eye = jnp.eye(N_HEADS, dtype=F32)
    mbd = jnp.einsum("bjhd,hg->bhdjg", means, eye)
    mbd = jnp.pad(mbd, ((0, 0), (0, 0), (0, 0), (0, n_slots - nf), (0, 0))).reshape(B, D_ATT, n_slots * N_HEADS)
    bt_b = _bias_tiles(rb_b, T)
    o_b = _moba_prompt(rb_b[N_BUCKETS - 1], bt_b, _heads_major(pr["q_b"], B, S),
                       _chunked_T(pr["k_b"], B, S, T), _chunked_T(pr["v_b"], B, S, T),
                       pr["q_b"].reshape(B, S, D_ATT), mbd)
    return o_a.reshape(B * S, D_ATT), o_b.reshape(B * S, D_ATT)


def _sample_mixers(sm, DB, T, caches, page_table, rb_a, rb_b, li):
    ck_a, cv_a, ck_i, ck_b, cv_b = caches
    pad_rows = lambda a: jnp.pad(a.reshape(DB, T, -1), ((0, 0), (0, PAGE_SIZE - T), (0, 0)))
    qi_rows = _heads_major(sm["q_i"], DB, T).reshape(DB, IDX_HEADS * T, IDX_DIM)
    wi_rows = sm["w_i"].reshape(DB, T, IDX_HEADS).transpose(0, 2, 1).reshape(DB, IDX_HEADS * T, 1)
    sel_a = _smp_dsa_select(page_table, qi_rows, wi_rows, pad_rows(sm["k_i"]), ck_i, li)
    rbr_a = jnp.repeat(rb_a.T, T, axis=0)
    rbr_b = jnp.repeat(rb_b.T, T, axis=0)
    qbd_a = _block_diag_rows(sm["q_a"].reshape(DB, T, N_HEADS, HEAD_DIM))
    o_a = _smp_attn(page_table, qbd_a, rbr_a, pad_rows(sm["k_a"]), pad_rows(sm["v_a"]), sel_a,
                    ck_a, cv_a, li, mode="dsa")
    qbd_b = _block_diag_rows(sm["q_b"].reshape(DB, T, N_HEADS, HEAD_DIM))
    sel_b = _smp_moba_select(page_table, qbd_b, ck_b, li)
    o_b = _smp_attn(page_table, qbd_b, rbr_b, pad_rows(sm["k_b"]), pad_rows(sm["v_b"]), sel_b,
                    ck_b, cv_b, li, mode="moba")
    return o_a.reshape(DB * T, D_ATT), o_b.reshape(DB * T, D_ATT)


def kernel(x_prompt, x_sample, cache_k_a, cache_v_a, cache_kidx, cache_k_b, cache_v_b, page_table, p_prompt, p_sample, rel_bias, w_in, kidx_ln_g, kidx_ln_b, w_branch_a, w_branch_b, w_out, ln1_g, ln1_b, w_router, b_router, w_gate_up, b_gate_up, w_down, b_down, w_ple_gate, w_ple_proj, ln2_g, ln2_b):
    B, S, D = x_prompt.shape
    DB, T, _ = x_sample.shape
    depth = w_in.shape[0]
    n_pool = cache_k_a.shape[1]
    rb_a = rel_bias[:, :N_HEADS]
    rb_b = rel_bias[:, N_HEADS:]
    flat = lambda c: c.reshape(depth, n_pool, PAGE_SIZE, -1)
    caches = (flat(cache_k_a), flat(cache_v_a), flat(cache_kidx), flat(cache_k_b), flat(cache_v_b))
    T_dsa = min(256, S)

    xp = x_prompt.reshape(B * S, D)
    xs = x_sample.reshape(DB * T, D)
    rows_p, rows_s = [], []
    for li in range(depth):
        bf = lambda a: a[li].astype(BF16)
        merge_w = (bf(w_branch_a), bf(w_branch_b), bf(w_out), ln1_g[li][None], ln1_b[li][None],
                   w_router[li], b_router[li][None], bf(w_ple_gate), bf(w_ple_proj))
        moe_w = (w_gate_up, b_gate_up, w_down, b_down, ln2_g[li][None], ln2_b[li][None], li)

        pr = _project(xp, w_in[li], kidx_ln_g[li], kidx_ln_b[li])
        o_a, o_b = _prompt_mixers(pr, B, S, rb_a, rb_b, T_dsa)
        x1, comb, ple = _merge(xp, o_a, o_b, pr["gates"], p_prompt[li].reshape(B * S, -1), *merge_w)
        xp = _moe(x1, comb, ple, *moe_w)
        rows_p.append(pr)

        sm = _project(xs, w_in[li], kidx_ln_g[li], kidx_ln_b[li])
        o_a, o_b = _sample_mixers(sm, DB, T, [c for c in caches], page_table, rb_a, rb_b, li)
        x1, comb, ple = _merge(xs, o_a, o_b, sm["gates"], p_sample[li].reshape(DB * T, -1), *merge_w)
        xs = _moe(x1, comb, ple, *moe_w)
        rows_s.append(sm)

    def stack(rows, name, lead, tail):
        return jnp.stack([r[name].reshape(lead + tail) for r in rows])

    hd = (N_HEADS, HEAD_DIM)
    outs = [xp.reshape(B, S, D), xs.reshape(DB, T, D)]
    for rows, lead in ((rows_p, (B, S)), (rows_s, (DB, T))):
        outs += [stack(rows, "k_a", lead, hd), stack(rows, "v_a", lead, hd), stack(rows, "k_i", lead, (IDX_DIM,)),
                 stack(rows, "k_b", lead, hd), stack(rows, "v_b", lead, hd)]
    return tuple(outs)
```

```python
import functools
import math

import numpy as np
import jax
import jax.numpy as jnp
from jax import lax
from jax.experimental import pallas as pl
from jax.experimental.pallas import tpu as pltpu

F32 = jnp.float32
BF16 = jnp.bfloat16
I32 = jnp.int32

D_MODEL = 1024
HEAD_DIM = 64
N_HEADS = 8
D_ATT = N_HEADS * HEAD_DIM
IDX_HEADS = 8
IDX_DIM = 64
IDX_TOPK = 256
MOBA_BLOCK = 256
MOBA_TOPK = 3
N_BUCKETS = 32
MAX_DISTANCE = 128
N_EXPERTS = 32
TOP_K = 4
D_EXPERT = D_MODEL
SWIGLU_LIMIT = 7.0
SWIGLU_ALPHA = 1.702
PLE_DIM = 256
LN_EPS = 1e-5
PAGE_SIZE = 128
DEPTH = 2
ALPHA_DN = (2 * DEPTH) ** 0.25
ATTN_SCALE = HEAD_DIM ** -0.5
IDX_SCALE = IDX_DIM ** -0.5
IN_SIZES = (D_ATT, D_ATT, D_ATT, IDX_HEADS * IDX_DIM, IDX_DIM, IDX_HEADS, D_ATT, D_ATT, D_ATT, 2 * D_MODEL)
IN_OFFS = tuple(int(v) for v in np.cumsum((0,) + IN_SIZES))

INT_MIN = -(2 ** 31)
NEG_BIG = -1e30
VMEM_LIMIT_BYTES = 56 * 1024 * 1024


def _bucket_of(d):
    d = max(d, 0)
    max_exact = N_BUCKETS // 2
    if d < max_exact:
        return d
    ratio = math.log(d / max_exact) / math.log(MAX_DISTANCE / max_exact)
    return min(max_exact + int(ratio * (N_BUCKETS - max_exact)), N_BUCKETS - 1)


_BUCKET_HI = tuple(max(d for d in range(4 * MAX_DISTANCE) if _bucket_of(d) == b) for b in range(N_BUCKETS - 1))
FAR_DIST = _BUCKET_HI[-1] + 1


def _cparams(sem):
    return pltpu.CompilerParams(dimension_semantics=sem, vmem_limit_bytes=VMEM_LIMIT_BYTES)


def _split(x):
    hi = x.astype(BF16)
    lo = (x - hi.astype(F32)).astype(BF16)
    return hi, lo


_NN = (((1,), (0,)), ((), ()))
_NT = (((1,), (1,)), ((), ()))


def _dot(a, b, dims=_NN):
    return lax.dot_general(a, b, dims, preferred_element_type=F32)


def _dot3(a, b, dims=_NN):
    ah, al = _split(a)
    bh, bl = _split(b)
    return _dot(al, bh, dims) + _dot(ah, bl, dims) + _dot(ah, bh, dims)


def _layer_norm(x, g, b):
    mu = jnp.mean(x, axis=-1, keepdims=True)
    xc = x - mu
    var = jnp.mean(xc * xc, axis=-1, keepdims=True)
    return xc * lax.rsqrt(var + LN_EPS) * g + b


def _order_key(s):
    s = jnp.where(s == 0.0, 0.0, s)
    u = pltpu.bitcast(s, I32)
    return u ^ (jnp.right_shift(u, 31) & 0x7FFFFFFF)


def _bias_chain(d, rb_of):
    val = rb_of(N_BUCKETS - 1)
    for b in range(N_BUCKETS - 2, -1, -1):
        val = jnp.where(d <= _BUCKET_HI[b], rb_of(b), val)
    return val


def _mm_body(x_ref, w_ref, o_ref, *maybe_o16_ref, passes, act):
    x = x_ref[...]
    w = w_ref[...]
    y = _dot(x.astype(BF16), w.astype(BF16)) if passes == 1 else _dot3(x, w)
    if act == "sigmoid":
        y = jax.nn.sigmoid(y)
    o_ref[...] = y
    for o16_ref in maybe_o16_ref:
        o16_ref[...] = y.astype(BF16)


def _mm(x, w, *, passes=1, act=None, also_bf16=False):
    M, K = x.shape
    N = w.shape[1]
    tm = min(M, 512)
    tn = min(N, 512)
    spec = pl.BlockSpec((tm, tn), lambda i, j: (i, j))
    shape = jax.ShapeDtypeStruct((M, N), F32)
    return pl.pallas_call(
        functools.partial(_mm_body, passes=passes, act=act),
        grid=(M // tm, N // tn),
        in_specs=[pl.BlockSpec((tm, K), lambda i, j: (i, 0)),
                  pl.BlockSpec((K, tn), lambda i, j: (0, j))],
        out_specs=[spec, spec] if also_bf16 else spec,
        out_shape=[shape, jax.ShapeDtypeStruct((M, N), BF16)] if also_bf16 else shape,
        compiler_params=_cparams(("parallel", "parallel")),
        name="proj_mm",
    )(x, w)


def _mmT_body(wT_ref, x_ref, o_ref, *, passes, scales):
    wT = wT_ref[...]
    x = x_ref[...]
    y = _dot(wT.astype(BF16), x.astype(BF16), _NT) if passes == 1 else _dot3(wT, x, _NT)
    for s in scales:
        y = y * s
    o_ref[...] = y.astype(o_ref.dtype)


def _mmT(wT, x, *, passes=1, scales=(), out_dtype=F32):
    M, K = x.shape
    N = wT.shape[0]
    tm = min(M, 512)
    tn = min(N, 512)
    return pl.pallas_call(
        functools.partial(_mmT_body, passes=passes, scales=scales),
        grid=(M // tm, N // tn),
        in_specs=[pl.BlockSpec((tn, K), lambda i, j: (j, 0)),
                  pl.BlockSpec((tm, K), lambda i, j: (i, 0))],
        out_specs=pl.BlockSpec((tn, tm), lambda i, j: (j, i)),
        out_shape=jax.ShapeDtypeStruct((N, M), out_dtype),
        compiler_params=_cparams(("parallel", "parallel")),
        name="proj_mmT",
    )(wT, x)


def _proj_kidx_body(x_ref, wk_ref, g_ref, b_ref, ki_ref, kcat_ref):
    k = _layer_norm(_dot3(x_ref[...], wk_ref[...]), g_ref[...], b_ref[...])
    ki_ref[...] = k
    kh, kl = _split(k)
    kcat_ref[...] = jnp.concatenate([kh, kl, kh, jnp.zeros_like(kh)], axis=-1)


def _proj_kidx(x, wk, g, b):
    M, K = x.shape
    tm = min(M, 512)
    full = lambda a: pl.BlockSpec(a.shape, lambda i: (0,) * a.ndim)
    row = lambda n: pl.BlockSpec((tm, n), lambda i: (i, 0))
    return pl.pallas_call(
        _proj_kidx_body,
        grid=(M // tm,),
        in_specs=[row(K), full(wk), full(g), full(b)],
        out_specs=[row(IDX_DIM), row(4 * IDX_DIM)],
        out_shape=[jax.ShapeDtypeStruct((M, IDX_DIM), F32),
                   jax.ShapeDtypeStruct((M, 4 * IDX_DIM), BF16)],
        compiler_params=_cparams(("parallel",)),
        name="proj_kidx",
    )(x, wk, g, b)


def _proj_idx_body(x_ref, wq_ref, wk_ref, ww_ref, g_ref, b_ref, qi_ref, ki_ref, kcat_ref, wi_ref):
    x = x_ref[...]
    qi_ref[...] = _dot3(x, wq_ref[...])
    k = _layer_norm(_dot3(x, wk_ref[...]), g_ref[...], b_ref[...])
    ki_ref[...] = k
    kh, kl = _split(k)
    kcat_ref[...] = jnp.concatenate([kh, kl, kh, jnp.zeros_like(kh)], axis=-1)
    wi_ref[...] = _dot3(x, ww_ref[...]) * (IDX_HEADS ** -0.5)


def _proj_idx(x, wq, wk, ww, g, b):
    M, K = x.shape
    tm = min(M, 512)
    full = lambda a: pl.BlockSpec(a.shape, lambda i: (0,) * a.ndim)
    row = lambda n: pl.BlockSpec((tm, n), lambda i: (i, 0))
    return pl.pallas_call(
        _proj_idx_body,
        grid=(M // tm,),
        in_specs=[row(K), full(wq), full(wk), full(ww), full(g), full(b)],
        out_specs=[row(IDX_HEADS * IDX_DIM), row(IDX_DIM), row(4 * IDX_DIM), row(IDX_HEADS)],
        out_shape=[jax.ShapeDtypeStruct((M, IDX_HEADS * IDX_DIM), F32),
                   jax.ShapeDtypeStruct((M, IDX_DIM), F32),
                   jax.ShapeDtypeStruct((M, 4 * IDX_DIM), BF16),
                   jax.ShapeDtypeStruct((M, IDX_HEADS), F32)],
        compiler_params=_cparams(("parallel",)),
        name="proj_idx",
    )(x, wq, wk, ww, g, b)


def _bias_tiles_body(rb_ref, o_ref, *, T):
    rel = pl.program_id(0)
    h = pl.program_id(1)
    key = lax.broadcasted_iota(I32, (T, T), 0)
    qry = lax.broadcasted_iota(I32, (T, T), 1)
    d = qry - key + rel * T
    o_ref[...] = _bias_chain(d, lambda b: rb_ref[b, h])


def _bias_tiles(rb, T):
    H = rb.shape[1]
    return pl.pallas_call(
        functools.partial(_bias_tiles_body, T=T),
        grid=(2, H),
        in_specs=[pl.BlockSpec(memory_space=pltpu.SMEM)],
        out_specs=pl.BlockSpec((None, None, T, T), lambda r, h: (r, h, 0, 0)),
        out_shape=jax.ShapeDtypeStruct((2, H, T, T), F32),
        compiler_params=_cparams(("parallel", "parallel")),
        name="bias_tiles",
    )(rb)


def _attn_update(logits_of, pen_of, vT_of, m_ref, l_ref, acc_ref):
    heads = range(N_HEADS)
    s = [logits_of(h) + pen_of(h) for h in heads]
    m_old = [m_ref[h] for h in heads]
    m_new = [jnp.maximum(m_old[h], jnp.max(s[h], axis=0, keepdims=True)) for h in heads]
    p = [jnp.exp(s[h] - m_new[h]) for h in heads]
    alpha = [jnp.exp(m_old[h] - m_new[h]) for h in heads]
    for h in heads:
        l_ref[h] = alpha[h] * l_ref[h] + jnp.sum(p[h], axis=0, keepdims=True)
        m_ref[h] = m_new[h]
    pv = [_dot(vT_of(h), p[h].astype(BF16)) for h in heads]
    for h in heads:
        acc_ref[h] = alpha[h] * acc_ref[h] + pv[h]


def _attn_init(m_ref, l_ref, acc_ref):
    m_ref[...] = jnp.full(m_ref.shape, NEG_BIG, F32)
    l_ref[...] = jnp.zeros(l_ref.shape, F32)
    acc_ref[...] = jnp.zeros(acc_ref.shape, F32)


def _attn_finish(o_ref, l_ref, acc_ref):
    outT = jnp.concatenate([acc_ref[h] / l_ref[h] for h in range(N_HEADS)], axis=0)
    o_ref[...] = outT.T


def _padded_heads(qT_of):
    out = []
    for h in range(N_HEADS):
        q = qT_of(h)
        z = jnp.zeros_like(q)
        out.append(jnp.concatenate([q, z] if h % 2 == 0 else [z, q], axis=0))
    return out


def _head_pair(k, h):
    lo = (h // 2) * 2 * HEAD_DIM
    return k[:, lo:lo + 2 * HEAD_DIM]


def _dsa_prompt_body(rbf_ref, bt_ref, qa_ref, k_ref, vT_ref, qi_ref, wi_ref, kcat_ref, o_ref,
                     key_ref, m_ref, l_ref, acc_ref, *, T, n_top, idx_bits):
    i = pl.program_id(1)
    H = N_HEADS
    kpos = lax.broadcasted_iota(I32, (T, T), 0)
    qpos = lax.broadcasted_iota(I32, (T, T), 1)

    wi = wi_ref[...]
    qcat = []
    for h in range(IDX_HEADS):
        qh, ql = _split(qi_ref[h * IDX_DIM:(h + 1) * IDX_DIM, :])
        qcat.append(jnp.concatenate([qh, qh, ql, jnp.zeros_like(qh)], axis=0))

    def score_chunk(kc, carry):
        kcat = kcat_ref[pl.ds(pl.multiple_of(kc * T, T), T), :]
        s = jnp.zeros((T, T), F32)
        for h in range(IDX_HEADS):
            s = s + jnp.maximum(_dot(kcat, qcat[h]), 0.0) * wi[h:h + 1, :]
        key = _order_key(s)
        key_ref[kc] = jnp.where((kc == i) & (kpos > qpos), INT_MIN, key)
        return carry

    lax.fori_loop(0, i + 1, score_chunk, 0)

    def count(pred):
        def body(kc, acc):
            hit = jnp.where(pred(key_ref[kc], kc * T + kpos), 1.0, 0.0)
            return acc + jnp.sum(hit.reshape(T // 8, 8, T), axis=0)
        acc = lax.fori_loop(0, i + 1, body, jnp.zeros((8, T), F32))
        return jnp.sum(acc, axis=0, keepdims=True)

    def thr_bit(bi, prefix):
        cand = prefix | jnp.left_shift(jnp.int32(1), 31 - bi)
        cs = cand ^ INT_MIN
        return jnp.where(count(lambda k, g: k >= cs) >= n_top, cand, prefix)

    thr = lax.fori_loop(0, 32, thr_bit, jnp.zeros((1, T), I32)) ^ INT_MIN
    need = n_top - count(lambda k, g: k > thr)
    n_tie = count(lambda k, g: k == thr)

    def tie_search():
        def bit(bi, j0):
            cand = j0 | jnp.left_shift(jnp.int32(1), idx_bits - 1 - bi)
            c = count(lambda k, g: (k == thr) & (g < cand))
            return jnp.where(c < need, cand, j0)
        return lax.fori_loop(0, idx_bits, bit, jnp.zeros((1, T), I32))

    any_excess = jnp.max(n_tie - need) > 0.0
    jcut = lax.cond(any_excess, tie_search, lambda: jnp.full((1, T), 2 ** 30, I32))

    _attn_init(m_ref, l_ref, acc_ref)
    q = _padded_heads(lambda h: qa_ref[h * HEAD_DIM:(h + 1) * HEAD_DIM, :])

    def attn_chunk(kc, near):
        kk = k_ref[pl.ds(pl.multiple_of(kc * T, T), T), :]
        keys = key_ref[kc]
        g = kc * T + kpos
        sel = (keys > thr) | ((keys == thr) & (g <= jcut))
        if near:
            sel = sel & (g <= i * T + qpos)
        pen = jnp.where(sel, 0.0, NEG_BIG)
        _attn_update(
            lambda h: _dot(_head_pair(kk, h), q[h]) + (bt_ref[i - kc, h] if near else rbf_ref[h]),
            lambda h: pen,
            lambda h: vT_ref[h * HEAD_DIM:(h + 1) * HEAD_DIM, pl.ds(pl.multiple_of(kc * T, T), T)],
            m_ref, l_ref, acc_ref)

    n_far = jnp.maximum(i - 1, 0)
    lax.fori_loop(0, n_far, lambda kc, c: (attn_chunk(kc, False), c)[1], 0)
    lax.fori_loop(n_far, i + 1, lambda kc, c: (attn_chunk(kc, True), c)[1], 0)
    _attn_finish(o_ref, l_ref, acc_ref)


def _attn_scratch(T):
    return [pltpu.VMEM((N_HEADS, 1, T), F32), pltpu.VMEM((N_HEADS, 1, T), F32),
            pltpu.VMEM((N_HEADS, HEAD_DIM, T), F32)]


def _dsa_prompt(rb_far, btiles, qaT, k16, vT, qiT, wiT, kcat, *, B, T):
    M = k16.shape[0]
    S = M // B
    H = N_HEADS
    nc = S // T
    n_top = min(IDX_TOPK, S // 4)
    body = functools.partial(_dsa_prompt_body, T=T, n_top=n_top, idx_bits=max(1, (S - 1).bit_length()))
    tile = lambda rows: pl.BlockSpec((rows, T), lambda b, i: (0, b * nc + i))
    return pl.pallas_call(
        body,
        grid=(B, nc),
        in_specs=[pl.BlockSpec(memory_space=pltpu.SMEM),
                  pl.BlockSpec((2, H, T, T), lambda b, i: (0, 0, 0, 0)),
                  tile(D_ATT),
                  pl.BlockSpec((S, D_ATT), lambda b, i: (b, 0)),
                  pl.BlockSpec((D_ATT, S), lambda b, i: (0, b)),
                  tile(IDX_HEADS * IDX_DIM),
                  tile(IDX_HEADS),
                  pl.BlockSpec((S, 4 * IDX_DIM), lambda b, i: (b, 0))],
        out_specs=pl.BlockSpec((T, D_ATT), lambda b, i: (b * nc + i, 0)),
        out_shape=jax.ShapeDtypeStruct((M, D_ATT), F32),
        scratch_shapes=[pltpu.VMEM((nc, T, T), I32)] + _attn_scratch(T),
        compiler_params=_cparams(("parallel", "arbitrary")),
        name="dsa_prompt",
    )(rb_far, btiles, qaT, k16, vT, qiT, wiT, kcat)


def _block_means_body(k_ref, o_ref):
    o_ref[...] = jnp.sum(k_ref[...], axis=0, keepdims=True) * (1.0 / MOBA_BLOCK)


def _block_means(k):
    B, L, D = k.shape
    nf = L // MOBA_BLOCK
    return pl.pallas_call(
        _block_means_body,
        grid=(B, nf),
        in_specs=[pl.BlockSpec((None, MOBA_BLOCK, D), lambda b, j: (b, j, 0))],
        out_specs=pl.BlockSpec((None, None, 1, D), lambda b, j: (b, j, 0, 0)),
        out_shape=jax.ShapeDtypeStruct((B, nf, 1, D), F32),
        compiler_params=_cparams(("parallel", "parallel")),
        name="block_means",
    )(k)


def _moba_prompt_body(rbf_ref, bt_ref, qbT_ref, k_ref, vT_ref, mbdT_ref, o_ref,
                      ch_ref, m_ref, l_ref, acc_ref, *, T, n_sel, n_slots):
    i = pl.program_id(1)
    H = N_HEADS
    W = n_slots * H
    kpos = lax.broadcasted_iota(I32, (T, T), 0)
    qpos = lax.broadcasted_iota(I32, (T, T), 1)
    qT = qbT_ref[...]

    g = _dot3(mbdT_ref[...], qT)
    blk = jnp.right_shift(lax.broadcasted_iota(I32, (W, T), 0), int(math.log2(H)))
    g = jnp.where(blk < i, g, -jnp.inf)
    rank = jnp.zeros((W, T), F32)
    for r in range(1, n_slots):
        other = pltpu.roll(g, r * H, axis=0)
        rank = rank + jnp.where(blk >= r, jnp.where(other >= g, 1.0, 0.0), jnp.where(other > g, 1.0, 0.0))
    ch_ref[...] = jnp.where((rank < n_sel) & (blk < i), 0.0, NEG_BIG)

    _attn_init(m_ref, l_ref, acc_ref)
    q = _padded_heads(lambda h: (qT[h * HEAD_DIM:(h + 1) * HEAD_DIM, :] * ATTN_SCALE).astype(BF16))
    causal_pen = jnp.where(kpos <= qpos, 0.0, NEG_BIG)

    def attn_chunk(kc, near):
        kk = k_ref[pl.ds(pl.multiple_of(kc * T, T), T), :]

        def pen_of(h):
            pen = ch_ref[pl.ds(kc * H + h, 1), :]
            return jnp.where(kc == i, causal_pen, pen) if near else pen

        _attn_update(
            lambda h: _dot(_head_pair(kk, h), q[h]) + (bt_ref[i - kc, h] if near else rbf_ref[h]),
            pen_of,
            lambda h: vT_ref[h * HEAD_DIM:(h + 1) * HEAD_DIM, pl.ds(pl.multiple_of(kc * T, T), T)],
            m_ref, l_ref, acc_ref)

    n_far = jnp.maximum(i - 1, 0)
    lax.fori_loop(0, n_far, lambda kc, c: (attn_chunk(kc, False), c)[1], 0)
    lax.fori_loop(n_far, i + 1, lambda kc, c: (attn_chunk(kc, True), c)[1], 0)
    _attn_finish(o_ref, l_ref, acc_ref)


def _moba_prompt(rb_far, btiles, qbT, k16, vT, mbdT, *, B):
    M = k16.shape[0]
    S = M // B
    H = N_HEADS
    T = MOBA_BLOCK
    nc = S // T
    W = mbdT.shape[1]
    body = functools.partial(_moba_prompt_body, T=T, n_sel=min(MOBA_TOPK, nc), n_slots=W // H)
    return pl.pallas_call(
        body,
        grid=(B, nc),
        in_specs=[pl.BlockSpec(memory_space=pltpu.SMEM),
                  pl.BlockSpec((2, H, T, T), lambda b, i: (0, 0, 0, 0)),
                  pl.BlockSpec((D_ATT, T), lambda b, i: (0, b * nc + i)),
                  pl.BlockSpec((S, D_ATT), lambda b, i: (b, 0)),
                  pl.BlockSpec((D_ATT, S), lambda b, i: (0, b)),
                  pl.BlockSpec((None, W, D_ATT), lambda b, i: (b, 0, 0))],
        out_specs=pl.BlockSpec((T, D_ATT), lambda b, i: (b * nc + i, 0)),
        out_shape=jax.ShapeDtypeStruct((M, D_ATT), F32),
        scratch_shapes=[pltpu.VMEM((W, T), F32)] + _attn_scratch(T),
        compiler_params=_cparams(("parallel", "arbitrary")),
        name="moba_prompt",
    )(rb_far, btiles, qbT, k16, vT, mbdT)


def _merge_body(x_ref, oa_ref, ob_ref, g_ref, p_ref, wba_ref, wbb_ref, wo_ref, g1_ref, b1_ref,
                wr_ref, br_ref, wpg_ref, wpp_ref, x1_ref, comb_ref, ple_ref):
    bra = _dot(oa_ref[...].astype(BF16), wba_ref[...])
    brb = _dot(ob_ref[...].astype(BF16), wbb_ref[...])
    gates = g_ref[...]
    mix = gates[:, :D_MODEL] * bra + gates[:, D_MODEL:] * brb
    y = _dot(mix.astype(BF16), wo_ref[...])
    x1 = _layer_norm(ALPHA_DN * x_ref[...] + y, g1_ref[...], b1_ref[...])
    x1_ref[...] = x1

    logits = _dot3(x1, wr_ref[...]) + br_ref[...]
    lane = lax.broadcasted_iota(I32, logits.shape, 1)
    work = logits
    kept = jnp.zeros(logits.shape, jnp.bool_)
    for _ in range(TOP_K):
        mx = jnp.max(work, axis=-1, keepdims=True)
        first = jnp.min(jnp.where(work == mx, lane, N_EXPERTS), axis=-1, keepdims=True)
        hit = lane == first
        kept = kept | hit
        work = jnp.where(hit, -jnp.inf, work)
    top = jnp.max(logits, axis=-1, keepdims=True)
    e = jnp.where(kept, jnp.exp(logits - top), 0.0)
    comb_ref[...] = e / jnp.sum(e, axis=-1, keepdims=True)

    x1b = x1.astype(BF16)
    ple_ref[...] = jax.nn.sigmoid(_dot(x1b, wpg_ref[...])) * _dot(p_ref[...].astype(BF16), wpp_ref[...])


def _merge(x, oa, ob, gates, p, wba, wbb, wo, g1, b1, wr, br, wpg, wpp):
    M = x.shape[0]
    tm = min(M, 512)
    full = lambda a: pl.BlockSpec(a.shape, lambda i: (0,) * a.ndim)
    row = lambda n: pl.BlockSpec((tm, n), lambda i: (i, 0))
    return pl.pallas_call(
        _merge_body,
        grid=(M // tm,),
        in_specs=[row(D_MODEL), row(D_ATT), row(D_ATT), row(2 * D_MODEL), row(PLE_DIM),
                  full(wba), full(wbb), full(wo), full(g1), full(b1), full(wr), full(br), full(wpg), full(wpp)],
        out_specs=[row(D_MODEL), row(N_EXPERTS), row(D_MODEL)],
        out_shape=[jax.ShapeDtypeStruct((M, D_MODEL), F32),
                   jax.ShapeDtypeStruct((M, N_EXPERTS), F32),
                   jax.ShapeDtypeStruct((M, D_MODEL), F32)],
        compiler_params=_cparams(("parallel",)),
        name="merge",
    )(x, oa, ob, gates, p, wba, wbb, wo, g1, b1, wr, br, wpg, wpp)


MOE_FC = 512


def _moe_body(x1_ref, comb_ref, ple_ref, wg_ref, wu_ref, bg_ref, bu_ref, wd_ref, bd_ref, g2_ref, b2_ref,
              o_ref, acc_ref):
    e = pl.program_id(1)
    f = pl.program_id(2)

    @pl.when((e == 0) & (f == 0))
    def _():
        acc_ref[...] = jnp.zeros(acc_ref.shape, F32)

    comb = comb_ref[...]
    lane = lax.broadcasted_iota(I32, comb.shape, 1)
    ce = jnp.sum(jnp.where(lane == e, comb, 0.0), axis=-1, keepdims=True)
    xb = x1_ref[...].astype(BF16)
    gt = jnp.minimum(_dot(xb, wg_ref[...].astype(BF16)) + bg_ref[...], SWIGLU_LIMIT)
    up = jnp.clip(_dot(xb, wu_ref[...].astype(BF16)) + bu_ref[...], -SWIGLU_LIMIT, SWIGLU_LIMIT)
    hid = (up + 1.0) * gt * jax.nn.sigmoid(SWIGLU_ALPHA * gt) * ce
    acc_ref[...] += _dot(hid.astype(BF16), wd_ref[...].astype(BF16))

    @pl.when((e == pl.num_programs(1) - 1) & (f == pl.num_programs(2) - 1))
    def _():
        y = acc_ref[...] + _dot3(comb, bd_ref[...])
        o_ref[...] = _layer_norm(ALPHA_DN * x1_ref[...] + y + ple_ref[...], g2_ref[...], b2_ref[...])


def _moe(x1, comb, ple, w_gu, b_gu, w_dn, b_dn, g2, b2, li):
    M = x1.shape[0]
    tm = min(M, 512)
    E = w_gu.shape[1]
    nf = D_EXPERT // MOE_FC
    b_gu4 = b_gu.reshape(b_gu.shape[0], E, 1, 2 * D_EXPERT)
    row = lambda n: pl.BlockSpec((tm, n), lambda i, e, f: (i, 0))
    return pl.pallas_call(
        _moe_body,
        grid=(M // tm, E, nf),
        in_specs=[row(D_MODEL), row(N_EXPERTS), row(D_MODEL),
                  pl.BlockSpec((None, None, D_MODEL, MOE_FC), lambda i, e, f: (li, e, 0, f)),
                  pl.BlockSpec((None, None, D_MODEL, MOE_FC), lambda i, e, f: (li, e, 0, nf + f)),
                  pl.BlockSpec((None, None, 1, MOE_FC), lambda i, e, f: (li, e, 0, f)),
                  pl.BlockSpec((None, None, 1, MOE_FC), lambda i, e, f: (li, e, 0, nf + f)),
                  pl.BlockSpec((None, None, MOE_FC, D_MODEL), lambda i, e, f: (li, e, f, 0)),
                  pl.BlockSpec((None, E, D_MODEL), lambda i, e, f: (li, 0, 0)),
                  pl.BlockSpec((1, D_MODEL), lambda i, e, f: (0, 0)),
                  pl.BlockSpec((1, D_MODEL), lambda i, e, f: (0, 0))],
        out_specs=row(D_MODEL),
        out_shape=jax.ShapeDtypeStruct((M, D_MODEL), F32),
        scratch_shapes=[pltpu.VMEM((tm, D_MODEL), F32)],
        compiler_params=_cparams(("parallel", "arbitrary", "arbitrary")),
        name="moe",
    )(x1, comb, ple, w_gu, w_gu, b_gu4, b_gu4, w_dn, b_dn, g2, b2)


SEL_PAGES = 16
ATT_PAGES = 8


def _page_specs(n, width, li):
    def spec(u):
        return pl.BlockSpec((None, None, PAGE_SIZE, width),
                            lambda b, c, pt: (li, pt[b, c * n + u], 0, 0))
    return [spec(u) for u in range(n)]


def _smp_dsa_select_body(pt_ref, qi_ref, wi_ref, knew_ref, *rest, n_pages, n_top, idx_bits):
    pages = rest[:SEL_PAGES]
    sel_ref = rest[SEL_PAGES]
    key_ref = rest[SEL_PAGES + 1]
    c = pl.program_id(1)
    T = qi_ref.shape[0] // IDX_HEADS
    qi = qi_ref[...]
    wi = wi_ref[...] * IDX_SCALE

    def page_score(kp):
        s = jnp.maximum(_dot3(qi, kp, _NT), 0.0) * wi
        return jnp.sum(s.reshape(IDX_HEADS, T, PAGE_SIZE), axis=0)

    for u in range(SEL_PAGES):
        key_ref[c * SEL_PAGES + u] = _order_key(page_score(pages[u][...]))

    @pl.when(c == pl.num_programs(1) - 1)
    def _():
        qrow = lax.broadcasted_iota(I32, (T, PAGE_SIZE), 0)
        lane = lax.broadcasted_iota(I32, (T, PAGE_SIZE), 1)
        key_ref[n_pages] = jnp.where(lane <= qrow, _order_key(page_score(knew_ref[...])), INT_MIN)
        keys = key_ref[...]
        shape = keys.shape
        gidx = lax.broadcasted_iota(I32, shape, 0) * PAGE_SIZE + lax.broadcasted_iota(I32, shape, 2)
        valid = (lax.broadcasted_iota(I32, shape, 0) < n_pages) | \
                (lax.broadcasted_iota(I32, shape, 2) <= lax.broadcasted_iota(I32, shape, 1))

        def count(hit):
            per_lane = jnp.sum(hit.astype(I32), axis=0)
            return jnp.sum(per_lane, axis=-1, keepdims=True)[None]

        def thr_bit(bi, prefix):
            cand = prefix | jnp.left_shift(jnp.int32(1), 31 - bi)
            cs = cand ^ INT_MIN
            return jnp.where(count(keys >= cs) >= n_top, cand, prefix)

        thr = lax.fori_loop(0, 32, thr_bit, jnp.zeros((1, T, 1), I32)) ^ INT_MIN
        need = n_top - count(keys > thr)

        def tie_bit(bi, j0):
            cand = j0 | jnp.left_shift(jnp.int32(1), idx_bits - 1 - bi)
            return jnp.where(count((keys == thr) & (gidx < cand)) < need, cand, j0)

        jcut = lax.fori_loop(0, idx_bits, tie_bit, jnp.zeros((1, T, 1), I32))
        sel = ((keys > thr) | ((keys == thr) & (gidx <= jcut))) & valid
        sel_ref[...] = sel.astype(F32)


def _smp_dsa_select(page_table, qi_rows, wi_rows, knew, cache_kidx, li):
    DB, P = page_table.shape
    T = qi_rows.shape[1] // IDX_HEADS
    L = P * PAGE_SIZE + T
    body = functools.partial(_smp_dsa_select_body, n_pages=P, n_top=min(IDX_TOPK, L // 4),
                             idx_bits=max(1, ((P + 1) * PAGE_SIZE - 1).bit_length()))
    grid_spec = pltpu.PrefetchScalarGridSpec(
        num_scalar_prefetch=1,
        grid=(DB, P // SEL_PAGES),
        in_specs=[pl.BlockSpec((None, IDX_HEADS * T, IDX_DIM), lambda b, c, pt: (b, 0, 0)),
                  pl.BlockSpec((None, IDX_HEADS * T, 1), lambda b, c, pt: (b, 0, 0)),
                  pl.BlockSpec((None, PAGE_SIZE, IDX_DIM), lambda b, c, pt: (b, 0, 0))]
                 + _page_specs(SEL_PAGES, IDX_DIM, li),
        out_specs=pl.BlockSpec((None, P + 1, T, PAGE_SIZE), lambda b, c, pt: (b, 0, 0, 0)),
        scratch_shapes=[pltpu.VMEM((P + 1, T, PAGE_SIZE), I32)],
    )
    return pl.pallas_call(
        body,
        grid_spec=grid_spec,
        out_shape=jax.ShapeDtypeStruct((DB, P + 1, T, PAGE_SIZE), F32),
        compiler_params=_cparams(("parallel", "arbitrary")),
        name="smp_dsa_select",
    )(page_table, qi_rows, wi_rows, knew, *([cache_kidx] * SEL_PAGES))


def _smp_moba_select_body(pt_ref, qbd_ref, *rest, n_blocks, n_sel):
    pages = rest[:SEL_PAGES]
    sel_ref = rest[SEL_PAGES]
    mean_ref = rest[SEL_PAGES + 1]
    c = pl.program_id(1)
    per_step = SEL_PAGES * PAGE_SIZE // MOBA_BLOCK
    per_block = MOBA_BLOCK // PAGE_SIZE

    @pl.when(c == 0)
    def _():
        mean_ref[...] = jnp.zeros(mean_ref.shape, F32)

    for j in range(per_step):
        tot = pages[j * per_block][...]
        for u in range(1, per_block):
            tot = tot + pages[j * per_block + u][...]
        mean_ref[c, j:j + 1, :] = jnp.sum(tot, axis=0, keepdims=True) * (1.0 / MOBA_BLOCK)

    @pl.when(c == pl.num_programs(1) - 1)
    def _():
        means = mean_ref[...].reshape(-1, mean_ref.shape[-1])
        g = _dot3(qbd_ref[...], means, _NT)
        W = g.shape[1]
        lane = lax.broadcasted_iota(I32, g.shape, 1)
        g = jnp.where(lane < n_blocks, g, -jnp.inf)
        rank = jnp.zeros(g.shape, I32)
        for r in range(1, n_blocks):
            lower = pltpu.roll(g, r, axis=1)
            upper = pltpu.roll(g, W - r, axis=1)
            rank = rank + (lower >= g).astype(I32) + (upper > g).astype(I32)
        sel_ref[...] = ((rank < n_sel) & (lane < n_blocks)).astype(F32)


def _smp_moba_select(page_table, qbd, cache_k, li):
    DB, P = page_table.shape
    n_blocks = P * PAGE_SIZE // MOBA_BLOCK
    per_step = SEL_PAGES * PAGE_SIZE // MOBA_BLOCK
    n_steps = P // SEL_PAGES
    W = 128
    assert n_blocks < W and per_step == 8
    R = qbd.shape[1]
    body = functools.partial(_smp_moba_select_body, n_blocks=n_blocks, n_sel=min(MOBA_TOPK, n_blocks))
    grid_spec = pltpu.PrefetchScalarGridSpec(
        num_scalar_prefetch=1,
        grid=(DB, n_steps),
        in_specs=[pl.BlockSpec((None, R, D_ATT), lambda b, c, pt: (b, 0, 0))]
                 + _page_specs(SEL_PAGES, D_ATT, li),
        out_specs=pl.BlockSpec((None, R, W), lambda b, c, pt: (b, 0, 0)),
        scratch_shapes=[pltpu.VMEM((W // per_step, per_step, D_ATT), F32)],
    )
    return pl.pallas_call(
        body,
        grid_spec=grid_spec,
        out_shape=jax.ShapeDtypeStruct((DB, R, W), F32),
        compiler_params=_cparams(("parallel", "arbitrary")),
        name="smp_moba_select",
    )(page_table, qbd, *([cache_k] * SEL_PAGES))


def _smp_attn_body(pt_ref, qbd_ref, rbr_ref, knew_ref, vnew_ref, sel_ref, selnew_ref, *rest, mode, n_pages):
    kp = rest[:ATT_PAGES]
    vp = rest[ATT_PAGES:2 * ATT_PAGES]
    o_ref, m_ref, l_ref, acc_ref = rest[2 * ATT_PAGES:]
    c = pl.program_id(1)
    R = qbd_ref.shape[0]
    T = R // N_HEADS
    q = (qbd_ref[...] * ATTN_SCALE).astype(BF16)
    rbr = rbr_ref[...]
    rowq = lax.broadcasted_iota(I32, (R, PAGE_SIZE), 0) & (T - 1)
    lane = lax.broadcasted_iota(I32, (R, PAGE_SIZE), 1)

    @pl.when(c == 0)
    def _():
        m_ref[...] = jnp.full(m_ref.shape, NEG_BIG, F32)
        l_ref[...] = jnp.zeros(l_ref.shape, F32)
        acc_ref[...] = jnp.zeros(acc_ref.shape, F32)

    def bias_for(dist):
        return _bias_chain(dist, lambda b: rbr[:, b:b + 1])

    def update(s, mask, v):
        s = jnp.where(mask, s, NEG_BIG)
        m_old = m_ref[...]
        m_new = jnp.maximum(m_old, jnp.max(s, axis=-1, keepdims=True))
        p = jnp.where(mask, jnp.exp(s - m_new), 0.0)
        alpha = jnp.exp(m_old - m_new)
        l_ref[...] = alpha * l_ref[...] + jnp.sum(p, axis=-1, keepdims=True)
        acc_ref[...] = alpha * acc_ref[...] + _dot(p.astype(BF16), v)
        m_ref[...] = m_new

    logits, masks, vals = [], [], []
    for u in range(ATT_PAGES):
        page = c * ATT_PAGES + u
        s = _dot(q, kp[u][...].astype(BF16), _NT)
        dist = (n_pages - page) * PAGE_SIZE + rowq - lane
        if u == ATT_PAGES - 1:
            bias = lax.cond(c == pl.num_programs(1) - 1, lambda: bias_for(dist),
                            lambda: jnp.broadcast_to(rbr[:, N_BUCKETS - 1:], (R, PAGE_SIZE)))
        else:
            bias = rbr[:, N_BUCKETS - 1:]
        logits.append(s + bias)
        if mode == "dsa":
            masks.append(jnp.tile(sel_ref[u], (N_HEADS, 1)) > 0.5)
        else:
            blk = page // (MOBA_BLOCK // PAGE_SIZE)
            sel = sel_ref[...]
            pick = jnp.sum(jnp.where(lax.broadcasted_iota(I32, sel.shape, 1) == blk, sel, 0.0),
                           axis=-1, keepdims=True)
            masks.append(jnp.broadcast_to(pick > 0.5, (R, PAGE_SIZE)))
        vals.append(vp[u][...].astype(BF16))
    update(jnp.concatenate(logits, axis=1), jnp.concatenate(masks, axis=1), jnp.concatenate(vals, axis=0))

    @pl.when(c == pl.num_programs(1) - 1)
    def _():
        s = _dot(q, knew_ref[...].astype(BF16), _NT) + bias_for(rowq - lane)
        mask = lane <= rowq
        if mode == "dsa":
            mask = mask & (jnp.tile(selnew_ref[...], (N_HEADS, 1)) > 0.5)
        update(s, mask, vnew_ref[...].astype(BF16))
        out = acc_ref[...] / l_ref[...]
        for h in range(N_HEADS):
            o_ref[:, h * HEAD_DIM:(h + 1) * HEAD_DIM] = out[h * T:(h + 1) * T, h * HEAD_DIM:(h + 1) * HEAD_DIM]


def _smp_attn(page_table, qbd, rbr, knew, vnew, sel, cache_k, cache_v, li, *, mode):
    DB, P = page_table.shape
    R = qbd.shape[1]
    T = R // N_HEADS
    if mode == "dsa":
        sel_specs = [pl.BlockSpec((None, ATT_PAGES, T, PAGE_SIZE), lambda b, c, pt: (b, c, 0, 0)),
                     pl.BlockSpec((None, None, T, PAGE_SIZE), lambda b, c, pt: (b, P, 0, 0))]
    else:
        sel_specs = [pl.BlockSpec((None, R, sel.shape[-1]), lambda b, c, pt: (b, 0, 0)),
                     pl.BlockSpec((None, R, sel.shape[-1]), lambda b, c, pt: (b, 0, 0))]
    grid_spec = pltpu.PrefetchScalarGridSpec(
        num_scalar_prefetch=1,
        grid=(DB, P // ATT_PAGES),
        in_specs=[pl.BlockSpec((None, R, D_ATT), lambda b, c, pt: (b, 0, 0)),
                  pl.BlockSpec(rbr.shape, lambda b, c, pt: (0, 0)),
                  pl.BlockSpec((None, PAGE_SIZE, D_ATT), lambda b, c, pt: (b, 0, 0)),
                  pl.BlockSpec((None, PAGE_SIZE, D_ATT), lambda b, c, pt: (b, 0, 0))]
                 + sel_specs + _page_specs(ATT_PAGES, D_ATT, li) + _page_specs(ATT_PAGES, D_ATT, li),
        out_specs=pl.BlockSpec((None, T, D_ATT), lambda b, c, pt: (b, 0, 0)),
        scratch_shapes=[pltpu.VMEM((R, 1), F32), pltpu.VMEM((R, 1), F32), pltpu.VMEM((R, D_ATT), F32)],
    )
    return pl.pallas_call(
        functools.partial(_smp_attn_body, mode=mode, n_pages=P),
        grid_spec=grid_spec,
        out_shape=jax.ShapeDtypeStruct((DB, T, D_ATT), F32),
        compiler_params=_cparams(("parallel", "arbitrary")),
        name="smp_attn_" + mode,
    )(page_table, qbd, rbr, knew, vnew, sel, sel, *([cache_k] * ATT_PAGES), *([cache_v] * ATT_PAGES))


def _project(x2, w, g, b):
    o = IN_OFFS
    qkv_a = _mm(x2, w[:, o[0]:o[3]])
    qi, ki, kcat, wi = _proj_idx(x2, w[:, o[3]:o[4]], w[:, o[4]:o[5]], w[:, o[5]:o[6]], g[None], b[None])
    qk_b = _mm(x2, w[:, o[6]:o[8]], passes=3)
    v_b = _mm(x2, w[:, o[8]:o[9]])
    gates = _mm(x2, w[:, o[9]:o[10]], act="sigmoid")
    return dict(q_a=qkv_a[:, :D_ATT], k_a=qkv_a[:, D_ATT:2 * D_ATT], v_a=qkv_a[:, 2 * D_ATT:],
                q_i=qi, k_i=ki, kcat=kcat, w_i=wi, q_b=qk_b[:, :D_ATT], k_b=qk_b[:, D_ATT:], v_b=v_b, gates=gates)


def _project_prompt(x2, w, g, b):
    o = IN_OFFS
    wT = lambda lo, hi: w[:, lo:hi].T
    k_a, k_a16 = _mm(x2, w[:, o[1]:o[2]], also_bf16=True)
    k_i, kcat = _proj_kidx(x2, w[:, o[4]:o[5]], g[None], b[None])
    k_b, k_b16 = _mm(x2, w[:, o[7]:o[8]], passes=3, also_bf16=True)
    return dict(
        q_aT=_mmT(wT(o[0], o[1]), x2, scales=(ATTN_SCALE,), out_dtype=BF16),
        k_a=k_a, k_a16=k_a16,
        v_a=_mm(x2, w[:, o[2]:o[3]]), v_aT=_mmT(wT(o[2], o[3]), x2, out_dtype=BF16),
        q_iT=_mmT(wT(o[3], o[4]), x2, passes=3),
        k_i=k_i, kcat=kcat,
        w_iT=_mmT(wT(o[5], o[6]), x2, passes=3, scales=(IDX_HEADS ** -0.5, IDX_SCALE)),
        q_bT=_mmT(wT(o[6], o[7]), x2, passes=3),
        k_b=k_b, k_b16=k_b16,
        v_b=_mm(x2, w[:, o[8]:o[9]]), v_bT=_mmT(wT(o[8], o[9]), x2, out_dtype=BF16),
        gates=_mm(x2, w[:, o[9]:o[10]], act="sigmoid"))


def _heads_major(a, B, S):
    return a.reshape(B, S, N_HEADS, HEAD_DIM).transpose(0, 2, 1, 3)


def _block_diag_rows(q):
    DB, T, H, Dh = q.shape
    eye = jnp.eye(H, dtype=q.dtype)
    return jnp.einsum("bthd,hg->bhtgd", q, eye).reshape(DB, H * T, H * Dh)


def _prompt_mixers(pr, B, S, rb_a, rb_b, T_dsa):
    bt_a = _bias_tiles(rb_a, T_dsa)
    o_a = _dsa_prompt(rb_a[N_BUCKETS - 1], bt_a, pr["q_aT"], pr["k_a16"], pr["v_aT"],
                      pr["q_iT"], pr["w_iT"], pr["kcat"], B=B, T=T_dsa)
    T = MOBA_BLOCK
    nf = S // T
    n_slots = 128 // N_HEADS
    assert nf <= n_slots
    means = _block_means(pr["k_b"].reshape(B, S, D_ATT)).reshape(B, nf, N_HEADS, HEAD_DIM)
    eye = jnp.eye(N_HEADS, dtype=F32)
    mbdT = jnp.einsum("bjhd,hg->bjghd", means, eye)
    mbdT = jnp.pad(mbdT, ((0, 0), (0, n_slots - nf), (0, 0), (0, 0), (0, 0))).reshape(B, n_slots * N_HEADS, D_ATT)
    bt_b = _bias_tiles(rb_b, T)
    o_b = _moba_prompt(rb_b[N_BUCKETS - 1], bt_b, pr["q_bT"], pr["k_b16"], pr["v_bT"], mbdT, B=B)
    return o_a, o_b


def _sample_mixers(sm, DB, T, caches, page_table, rb_a, rb_b, li):
    ck_a, cv_a, ck_i, ck_b, cv_b = caches
    pad_rows = lambda a: jnp.pad(a.reshape(DB, T, -1), ((0, 0), (0, PAGE_SIZE - T), (0, 0)))
    qi_rows = _heads_major(sm["q_i"], DB, T).reshape(DB, IDX_HEADS * T, IDX_DIM)
    wi_rows = sm["w_i"].reshape(DB, T, IDX_HEADS).transpose(0, 2, 1).reshape(DB, IDX_HEADS * T, 1)
    sel_a = _smp_dsa_select(page_table, qi_rows, wi_rows, pad_rows(sm["k_i"]), ck_i, li)
    rbr_a = jnp.repeat(rb_a.T, T, axis=0)
    rbr_b = jnp.repeat(rb_b.T, T, axis=0)
    qbd_a = _block_diag_rows(sm["q_a"].reshape(DB, T, N_HEADS, HEAD_DIM))
    o_a = _smp_attn(page_table, qbd_a, rbr_a, pad_rows(sm["k_a"]), pad_rows(sm["v_a"]), sel_a,
                    ck_a, cv_a, li, mode="dsa")
    qbd_b = _block_diag_rows(sm["q_b"].reshape(DB, T, N_HEADS, HEAD_DIM))
    sel_b = _smp_moba_select(page_table, qbd_b, ck_b, li)
    o_b = _smp_attn(page_table, qbd_b, rbr_b, pad_rows(sm["k_b"]), pad_rows(sm["v_b"]), sel_b,
                    ck_b, cv_b, li, mode="moba")
    return o_a.reshape(DB * T, D_ATT), o_b.reshape(DB * T, D_ATT)


def kernel(x_prompt, x_sample, cache_k_a, cache_v_a, cache_kidx, cache_k_b, cache_v_b, page_table, p_prompt, p_sample, rel_bias, w_in, kidx_ln_g, kidx_ln_b, w_branch_a, w_branch_b, w_out, ln1_g, ln1_b, w_router, b_router, w_gate_up, b_gate_up, w_down, b_down, w_ple_gate, w_ple_proj, ln2_g, ln2_b):
    B, S, D = x_prompt.shape
    DB, T, _ = x_sample.shape
    depth = w_in.shape[0]
    n_pool = cache_k_a.shape[1]
    rb_a = rel_bias[:, :N_HEADS]
    rb_b = rel_bias[:, N_HEADS:]
    flat = lambda c: c.reshape(depth, n_pool, PAGE_SIZE, -1)
    caches = (flat(cache_k_a), flat(cache_v_a), flat(cache_kidx), flat(cache_k_b), flat(cache_v_b))
    T_dsa = min(256, S)

    xp = x_prompt.reshape(B * S, D)
    xs = x_sample.reshape(DB * T, D)
    rows_p, rows_s = [], []
    for li in range(depth):
        bf = lambda a: a[li].astype(BF16)
        merge_w = (bf(w_branch_a), bf(w_branch_b), bf(w_out), ln1_g[li][None], ln1_b[li][None],
                   w_router[li], b_router[li][None], bf(w_ple_gate), bf(w_ple_proj))
        moe_w = (w_gate_up, b_gate_up, w_down, b_down, ln2_g[li][None], ln2_b[li][None], li)

        pr = _project_prompt(xp, w_in[li], kidx_ln_g[li], kidx_ln_b[li])
        o_a, o_b = _prompt_mixers(pr, B, S, rb_a, rb_b, T_dsa)
        x1, comb, ple = _merge(xp, o_a, o_b, pr["gates"], p_prompt[li].reshape(B * S, -1), *merge_w)
        xp = _moe(x1, comb, ple, *moe_w)
        rows_p.append(pr)

        sm = _project(xs, w_in[li], kidx_ln_g[li], kidx_ln_b[li])
        o_a, o_b = _sample_mixers(sm, DB, T, [c for c in caches], page_table, rb_a, rb_b, li)
        x1, comb, ple = _merge(xs, o_a, o_b, sm["gates"], p_sample[li].reshape(DB * T, -1), *merge_w)
        xs = _moe(x1, comb, ple, *moe_w)
        rows_s.append(sm)

    def stack(rows, name, lead, tail):
        return jnp.stack([r[name].reshape(lead + tail) for r in rows])

    hd = (N_HEADS, HEAD_DIM)
    outs = [xp.reshape(B, S, D), xs.reshape(DB, T, D)]
    for rows, lead in ((rows_p, (B, S)), (rows_s, (DB, T))):
        outs += [stack(rows, "k_a", lead, hd), stack(rows, "v_a", lead, hd), stack(rows, "k_i", lead, (IDX_DIM,)),
                 stack(rows, "k_b", lead, hd), stack(rows, "v_b", lead, hd)]
    return tuple(outs)
```

```python
import functools
import math

import numpy as np
import jax
import jax.numpy as jnp
from jax import lax
from jax.experimental import pallas as pl
from jax.experimental.pallas import tpu as pltpu

F32 = jnp.float32
BF16 = jnp.bfloat16
I32 = jnp.int32

D_MODEL = 1024
HEAD_DIM = 64
N_HEADS = 8
D_ATT = N_HEADS * HEAD_DIM
IDX_HEADS = 8
IDX_DIM = 64
IDX_TOPK = 256
MOBA_BLOCK = 256
MOBA_TOPK = 3
N_BUCKETS = 32
MAX_DISTANCE = 128
N_EXPERTS = 32
TOP_K = 4
D_EXPERT = D_MODEL
SWIGLU_LIMIT = 7.0
SWIGLU_ALPHA = 1.702
PLE_DIM = 256
LN_EPS = 1e-5
PAGE_SIZE = 128
DEPTH = 2
ALPHA_DN = (2 * DEPTH) ** 0.25
ATTN_SCALE = HEAD_DIM ** -0.5
IDX_SCALE = IDX_DIM ** -0.5
IN_SIZES = (D_ATT, D_ATT, D_ATT, IDX_HEADS * IDX_DIM, IDX_DIM, IDX_HEADS, D_ATT, D_ATT, D_ATT, 2 * D_MODEL)
IN_OFFS = tuple(int(v) for v in np.cumsum((0,) + IN_SIZES))

INT_MIN = -(2 ** 31)
NEG_BIG = -1e30
VMEM_LIMIT_BYTES = 56 * 1024 * 1024


def _bucket_of(d):
    d = max(d, 0)
    max_exact = N_BUCKETS // 2
    if d < max_exact:
        return d
    ratio = math.log(d / max_exact) / math.log(MAX_DISTANCE / max_exact)
    return min(max_exact + int(ratio * (N_BUCKETS - max_exact)), N_BUCKETS - 1)


_BUCKET_HI = tuple(max(d for d in range(4 * MAX_DISTANCE) if _bucket_of(d) == b) for b in range(N_BUCKETS - 1))
FAR_DIST = _BUCKET_HI[-1] + 1


def _cparams(sem):
    return pltpu.CompilerParams(dimension_semantics=sem, vmem_limit_bytes=VMEM_LIMIT_BYTES)


def _split(x):
    hi = x.astype(BF16)
    lo = (x - hi.astype(F32)).astype(BF16)
    return hi, lo


_NN = (((1,), (0,)), ((), ()))
_NT = (((1,), (1,)), ((), ()))


def _dot(a, b, dims=_NN):
    return lax.dot_general(a, b, dims, preferred_element_type=F32)


def _dot3(a, b, dims=_NN):
    ah, al = _split(a)
    bh, bl = _split(b)
    return _dot(al, bh, dims) + _dot(ah, bl, dims) + _dot(ah, bh, dims)


def _layer_norm(x, g, b):
    mu = jnp.mean(x, axis=-1, keepdims=True)
    xc = x - mu
    var = jnp.mean(xc * xc, axis=-1, keepdims=True)
    return xc * lax.rsqrt(var + LN_EPS) * g + b


def _order_key(s):
    s = jnp.where(s == 0.0, 0.0, s)
    u = pltpu.bitcast(s, I32)
    return u ^ (jnp.right_shift(u, 31) & 0x7FFFFFFF)


def _bias_chain(d, rb_of):
    val = rb_of(N_BUCKETS - 1)
    for b in range(N_BUCKETS - 2, -1, -1):
        val = jnp.where(d <= _BUCKET_HI[b], rb_of(b), val)
    return val


def _mm_body(x_ref, w_ref, o_ref, *maybe_o16_ref, passes, act):
    x = x_ref[...]
    w = w_ref[...]
    y = _dot(x.astype(BF16), w.astype(BF16)) if passes == 1 else _dot3(x, w)
    if act == "sigmoid":
        y = jax.nn.sigmoid(y)
    o_ref[...] = y
    for o16_ref in maybe_o16_ref:
        o16_ref[...] = y.astype(BF16)


def _mm(x, w, *, passes=1, act=None, also_bf16=False):
    M, K = x.shape
    N = w.shape[1]
    tm = min(M, 512)
    tn = min(N, 512)
    spec = pl.BlockSpec((tm, tn), lambda i, j: (i, j))
    shape = jax.ShapeDtypeStruct((M, N), F32)
    return pl.pallas_call(
        functools.partial(_mm_body, passes=passes, act=act),
        grid=(M // tm, N // tn),
        in_specs=[pl.BlockSpec((tm, K), lambda i, j: (i, 0)),
                  pl.BlockSpec((K, tn), lambda i, j: (0, j))],
        out_specs=[spec, spec] if also_bf16 else spec,
        out_shape=[shape, jax.ShapeDtypeStruct((M, N), BF16)] if also_bf16 else shape,
        compiler_params=_cparams(("parallel", "parallel")),
        name="proj_mm",
    )(x, w)


def _mmT_body(wT_ref, x_ref, o_ref, *, passes, scales):
    wT = wT_ref[...]
    x = x_ref[...]
    y = _dot(wT.astype(BF16), x.astype(BF16), _NT) if passes == 1 else _dot3(wT, x, _NT)
    for s in scales:
        y = y * s
    o_ref[...] = y.astype(o_ref.dtype)


def _mmT(wT, x, *, passes=1, scales=(), out_dtype=F32):
    M, K = x.shape
    N = wT.shape[0]
    tm = min(M, 512)
    tn = min(N, 512)
    return pl.pallas_call(
        functools.partial(_mmT_body, passes=passes, scales=scales),
        grid=(M // tm, N // tn),
        in_specs=[pl.BlockSpec((tn, K), lambda i, j: (j, 0)),
                  pl.BlockSpec((tm, K), lambda i, j: (i, 0))],
        out_specs=pl.BlockSpec((tn, tm), lambda i, j: (j, i)),
        out_shape=jax.ShapeDtypeStruct((N, M), out_dtype),
        compiler_params=_cparams(("parallel", "parallel")),
        name="proj_mmT",
    )(wT, x)


def _proj_kidx_body(x_ref, wk_ref, g_ref, b_ref, ki_ref, kcat_ref):
    k = _layer_norm(_dot3(x_ref[...], wk_ref[...]), g_ref[...], b_ref[...])
    ki_ref[...] = k
    kh, kl = _split(k)
    kcat_ref[...] = jnp.concatenate([kh, kl, kh, jnp.zeros_like(kh)], axis=-1)


def _proj_kidx(x, wk, g, b):
    M, K = x.shape
    tm = min(M, 512)
    full = lambda a: pl.BlockSpec(a.shape, lambda i: (0,) * a.ndim)
    row = lambda n: pl.BlockSpec((tm, n), lambda i: (i, 0))
    return pl.pallas_call(
        _proj_kidx_body,
        grid=(M // tm,),
        in_specs=[row(K), full(wk), full(g), full(b)],
        out_specs=[row(IDX_DIM), row(4 * IDX_DIM)],
        out_shape=[jax.ShapeDtypeStruct((M, IDX_DIM), F32),
                   jax.ShapeDtypeStruct((M, 4 * IDX_DIM), BF16)],
        compiler_params=_cparams(("parallel",)),
        name="proj_kidx",
    )(x, wk, g, b)


def _proj_idx_body(x_ref, wq_ref, wk_ref, ww_ref, g_ref, b_ref, qi_ref, ki_ref, kcat_ref, wi_ref):
    x = x_ref[...]
    qi_ref[...] = _dot3(x, wq_ref[...])
    k = _layer_norm(_dot3(x, wk_ref[...]), g_ref[...], b_ref[...])
    ki_ref[...] = k
    kh, kl = _split(k)
    kcat_ref[...] = jnp.concatenate([kh, kl, kh, jnp.zeros_like(kh)], axis=-1)
    wi_ref[...] = _dot3(x, ww_ref[...]) * (IDX_HEADS ** -0.5)


def _proj_idx(x, wq, wk, ww, g, b):
    M, K = x.shape
    tm = min(M, 512)
    full = lambda a: pl.BlockSpec(a.shape, lambda i: (0,) * a.ndim)
    row = lambda n: pl.BlockSpec((tm, n), lambda i: (i, 0))
    return pl.pallas_call(
        _proj_idx_body,
        grid=(M // tm,),
        in_specs=[row(K), full(wq), full(wk), full(ww), full(g), full(b)],
        out_specs=[row(IDX_HEADS * IDX_DIM), row(IDX_DIM), row(4 * IDX_DIM), row(IDX_HEADS)],
        out_shape=[jax.ShapeDtypeStruct((M, IDX_HEADS * IDX_DIM), F32),
                   jax.ShapeDtypeStruct((M, IDX_DIM), F32),
                   jax.ShapeDtypeStruct((M, 4 * IDX_DIM), BF16),
                   jax.ShapeDtypeStruct((M, IDX_HEADS), F32)],
        compiler_params=_cparams(("parallel",)),
        name="proj_idx",
    )(x, wq, wk, ww, g, b)


def _bias_tiles_body(rb_ref, o_ref, *, T):
    rel = pl.program_id(0)
    h = pl.program_id(1)
    key = lax.broadcasted_iota(I32, (T, T), 0)
    qry = lax.broadcasted_iota(I32, (T, T), 1)
    d = qry - key + rel * T
    o_ref[...] = _bias_chain(d, lambda b: rb_ref[b, h])


def _bias_tiles(rb, T):
    H = rb.shape[1]
    return pl.pallas_call(
        functools.partial(_bias_tiles_body, T=T),
        grid=(2, H),
        in_specs=[pl.BlockSpec(memory_space=pltpu.SMEM)],
        out_specs=pl.BlockSpec((None, None, T, T), lambda r, h: (r, h, 0, 0)),
        out_shape=jax.ShapeDtypeStruct((2, H, T, T), F32),
        compiler_params=_cparams(("parallel", "parallel")),
        name="bias_tiles",
    )(rb)


def _attn_update(logits_of, pen_of, vT_of, m_ref, l_ref, acc_ref):
    heads = range(N_HEADS)
    s = [logits_of(h) + pen_of(h) for h in heads]
    m_old = [m_ref[h] for h in heads]
    m_new = [jnp.maximum(m_old[h], jnp.max(s[h], axis=0, keepdims=True)) for h in heads]
    p = [jnp.exp(s[h] - m_new[h]) for h in heads]
    alpha = [jnp.exp(m_old[h] - m_new[h]) for h in heads]
    for h in heads:
        l_ref[h] = alpha[h] * l_ref[h] + jnp.sum(p[h], axis=0, keepdims=True)
        m_ref[h] = m_new[h]
    pv = [_dot(vT_of(h), p[h].astype(BF16)) for h in heads]
    for h in heads:
        acc_ref[h] = alpha[h] * acc_ref[h] + pv[h]


def _attn_init(m_ref, l_ref, acc_ref):
    m_ref[...] = jnp.full(m_ref.shape, NEG_BIG, F32)
    l_ref[...] = jnp.zeros(l_ref.shape, F32)
    acc_ref[...] = jnp.zeros(acc_ref.shape, F32)


def _attn_finish(o_ref, l_ref, acc_ref):
    outT = jnp.concatenate([acc_ref[h] / l_ref[h] for h in range(N_HEADS)], axis=0)
    o_ref[...] = outT.T


def _padded_heads(qT_of):
    out = []
    for h in range(N_HEADS):
        q = qT_of(h)
        z = jnp.zeros_like(q)
        out.append(jnp.concatenate([q, z] if h % 2 == 0 else [z, q], axis=0))
    return out


def _head_pair(k, h):
    lo = (h // 2) * 2 * HEAD_DIM
    return k[:, lo:lo + 2 * HEAD_DIM]


def _dsa_prompt_body(rbf_ref, bt_ref, qa_ref, k_ref, vT_ref, qi_ref, wi_ref, kcat_ref, o_ref,
                     key_ref, m_ref, l_ref, acc_ref, *, T, n_top, idx_bits):
    i = pl.program_id(1)
    H = N_HEADS
    kpos = lax.broadcasted_iota(I32, (T, T), 0)
    qpos = lax.broadcasted_iota(I32, (T, T), 1)

    wi = wi_ref[...]
    qcat = []
    for h in range(IDX_HEADS):
        qh, ql = _split(qi_ref[h * IDX_DIM:(h + 1) * IDX_DIM, :])
        qcat.append(jnp.concatenate([qh, qh, ql, jnp.zeros_like(qh)], axis=0))

    def score_chunk(kc, carry):
        kcat = kcat_ref[pl.ds(pl.multiple_of(kc * T, T), T), :]
        s = jnp.zeros((T, T), F32)
        for h in range(IDX_HEADS):
            s = s + jnp.maximum(_dot(kcat, qcat[h]), 0.0) * wi[h:h + 1, :]
        key = _order_key(s)
        key_ref[kc] = jnp.where((kc == i) & (kpos > qpos), INT_MIN, key)
        return carry

    lax.fori_loop(0, i + 1, score_chunk, 0)

    def count(pred):
        def body(kc, acc):
            hit = jnp.where(pred(key_ref[kc], kc * T + kpos), 1.0, 0.0)
            return acc + jnp.sum(hit.reshape(T // 8, 8, T), axis=0)
        acc = lax.fori_loop(0, i + 1, body, jnp.zeros((8, T), F32))
        return jnp.sum(acc, axis=0, keepdims=True)

    def thr_bit(bi, prefix):
        cand = prefix | jnp.left_shift(jnp.int32(1), 31 - bi)
        cs = cand ^ INT_MIN
        return jnp.where(count(lambda k, g: k >= cs) >= n_top, cand, prefix)

    thr = lax.fori_loop(0, 32, thr_bit, jnp.zeros((1, T), I32)) ^ INT_MIN
    need = n_top - count(lambda k, g: k > thr)
    n_tie = count(lambda k, g: k == thr)

    def tie_search():
        def bit(bi, j0):
            cand = j0 | jnp.left_shift(jnp.int32(1), idx_bits - 1 - bi)
            c = count(lambda k, g: (k == thr) & (g < cand))
            return jnp.where(c < need, cand, j0)
        return lax.fori_loop(0, idx_bits, bit, jnp.zeros((1, T), I32))

    any_excess = jnp.max(n_tie - need) > 0.0
    jcut = lax.cond(any_excess, tie_search, lambda: jnp.full((1, T), 2 ** 30, I32))

    _attn_init(m_ref, l_ref, acc_ref)
    q = _padded_heads(lambda h: qa_ref[h * HEAD_DIM:(h + 1) * HEAD_DIM, :])

    def attn_chunk(kc, near):
        kk = k_ref[pl.ds(pl.multiple_of(kc * T, T), T), :]
        keys = key_ref[kc]
        g = kc * T + kpos
        sel = (keys > thr) | ((keys == thr) & (g <= jcut))
        if near:
            sel = sel & (g <= i * T + qpos)
        pen = jnp.where(sel, 0.0, NEG_BIG)
        _attn_update(
            lambda h: _dot(_head_pair(kk, h), q[h]) + (bt_ref[i - kc, h] if near else rbf_ref[h]),
            lambda h: pen,
            lambda h: vT_ref[h * HEAD_DIM:(h + 1) * HEAD_DIM, pl.ds(pl.multiple_of(kc * T, T), T)],
            m_ref, l_ref, acc_ref)

    n_far = jnp.maximum(i - 1, 0)
    lax.fori_loop(0, n_far, lambda kc, c: (attn_chunk(kc, False), c)[1], 0)
    lax.fori_loop(n_far, i + 1, lambda kc, c: (attn_chunk(kc, True), c)[1], 0)
    _attn_finish(o_ref, l_ref, acc_ref)


def _attn_scratch(T):
    return [pltpu.VMEM((N_HEADS, 1, T), F32), pltpu.VMEM((N_HEADS, 1, T), F32),
            pltpu.VMEM((N_HEADS, HEAD_DIM, T), F32)]


def _dsa_prompt(rb_far, btiles, qaT, k16, vT, qiT, wiT, kcat, *, B, T):
    M = k16.shape[0]
    S = M // B
    H = N_HEADS
    nc = S // T
    n_top = min(IDX_TOPK, S // 4)
    body = functools.partial(_dsa_prompt_body, T=T, n_top=n_top, idx_bits=max(1, (S - 1).bit_length()))
    tile = lambda rows: pl.BlockSpec((rows, T), lambda b, i: (0, b * nc + i))
    return pl.pallas_call(
        body,
        grid=(B, nc),
        in_specs=[pl.BlockSpec(memory_space=pltpu.SMEM),
                  pl.BlockSpec((2, H, T, T), lambda b, i: (0, 0, 0, 0)),
                  tile(D_ATT),
                  pl.BlockSpec((S, D_ATT), lambda b, i: (b, 0)),
                  pl.BlockSpec((D_ATT, S), lambda b, i: (0, b)),
                  tile(IDX_HEADS * IDX_DIM),
                  tile(IDX_HEADS),
                  pl.BlockSpec((S, 4 * IDX_DIM), lambda b, i: (b, 0))],
        out_specs=pl.BlockSpec((T, D_ATT), lambda b, i: (b * nc + i, 0)),
        out_shape=jax.ShapeDtypeStruct((M, D_ATT), F32),
        scratch_shapes=[pltpu.VMEM((nc, T, T), I32)] + _attn_scratch(T),
        compiler_params=_cparams(("parallel", "arbitrary")),
        name="dsa_prompt",
    )(rb_far, btiles, qaT, k16, vT, qiT, wiT, kcat)


def _block_means_body(k_ref, o_ref):
    o_ref[...] = jnp.sum(k_ref[...], axis=0, keepdims=True) * (1.0 / MOBA_BLOCK)


def _block_means(k):
    B, L, D = k.shape
    nf = L // MOBA_BLOCK
    return pl.pallas_call(
        _block_means_body,
        grid=(B, nf),
        in_specs=[pl.BlockSpec((None, MOBA_BLOCK, D), lambda b, j: (b, j, 0))],
        out_specs=pl.BlockSpec((None, None, 1, D), lambda b, j: (b, j, 0, 0)),
        out_shape=jax.ShapeDtypeStruct((B, nf, 1, D), F32),
        compiler_params=_cparams(("parallel", "parallel")),
        name="block_means",
    )(k)


def _moba_prompt_body(rbf_ref, bt_ref, qbT_ref, k_ref, vT_ref, mbdT_ref, o_ref,
                      ch_ref, m_ref, l_ref, acc_ref, *, T, n_sel, n_slots):
    i = pl.program_id(1)
    H = N_HEADS
    W = n_slots * H
    kpos = lax.broadcasted_iota(I32, (T, T), 0)
    qpos = lax.broadcasted_iota(I32, (T, T), 1)
    qT = qbT_ref[...]

    g = _dot3(mbdT_ref[...], qT)
    blk = jnp.right_shift(lax.broadcasted_iota(I32, (W, T), 0), int(math.log2(H)))
    g = jnp.where(blk < i, g, -jnp.inf)
    rank = jnp.zeros((W, T), F32)
    for r in range(1, n_slots):
        other = pltpu.roll(g, r * H, axis=0)
        rank = rank + jnp.where(blk >= r, jnp.where(other >= g, 1.0, 0.0), jnp.where(other > g, 1.0, 0.0))
    ch_ref[...] = jnp.where((rank < n_sel) & (blk < i), 0.0, NEG_BIG)

    _attn_init(m_ref, l_ref, acc_ref)
    q = _padded_heads(lambda h: (qT[h * HEAD_DIM:(h + 1) * HEAD_DIM, :] * ATTN_SCALE).astype(BF16))
    causal_pen = jnp.where(kpos <= qpos, 0.0, NEG_BIG)

    def attn_chunk(kc, near):
        kk = k_ref[pl.ds(pl.multiple_of(kc * T, T), T), :]

        def pen_of(h):
            pen = ch_ref[pl.ds(kc * H + h, 1), :]
            return jnp.where(kc == i, causal_pen, pen) if near else pen

        _attn_update(
            lambda h: _dot(_head_pair(kk, h), q[h]) + (bt_ref[i - kc, h] if near else rbf_ref[h]),
            pen_of,
            lambda h: vT_ref[h * HEAD_DIM:(h + 1) * HEAD_DIM, pl.ds(pl.multiple_of(kc * T, T), T)],
            m_ref, l_ref, acc_ref)

    n_far = jnp.maximum(i - 1, 0)
    lax.fori_loop(0, n_far, lambda kc, c: (attn_chunk(kc, False), c)[1], 0)
    lax.fori_loop(n_far, i + 1, lambda kc, c: (attn_chunk(kc, True), c)[1], 0)
    _attn_finish(o_ref, l_ref, acc_ref)


def _moba_prompt(rb_far, btiles, qbT, k16, vT, mbdT, *, B):
    M = k16.shape[0]
    S = M // B
    H = N_HEADS
    T = MOBA_BLOCK
    nc = S // T
    W = mbdT.shape[1]
    body = functools.partial(_moba_prompt_body, T=T, n_sel=min(MOBA_TOPK, nc), n_slots=W // H)
    return pl.pallas_call(
        body,
        grid=(B, nc),
        in_specs=[pl.BlockSpec(memory_space=pltpu.SMEM),
                  pl.BlockSpec((2, H, T, T), lambda b, i: (0, 0, 0, 0)),
                  pl.BlockSpec((D_ATT, T), lambda b, i: (0, b * nc + i)),
                  pl.BlockSpec((S, D_ATT), lambda b, i: (b, 0)),
                  pl.BlockSpec((D_ATT, S), lambda b, i: (0, b)),
                  pl.BlockSpec((None, W, D_ATT), lambda b, i: (b, 0, 0))],
        out_specs=pl.BlockSpec((T, D_ATT), lambda b, i: (b * nc + i, 0)),
        out_shape=jax.ShapeDtypeStruct((M, D_ATT), F32),
        scratch_shapes=[pltpu.VMEM((W, T), F32)] + _attn_scratch(T),
        compiler_params=_cparams(("parallel", "arbitrary")),
        name="moba_prompt",
    )(rb_far, btiles, qbT, k16, vT, mbdT)


def _merge_body(x_ref, oa_ref, ob_ref, g_ref, p_ref, wba_ref, wbb_ref, wo_ref, g1_ref, b1_ref,
                wr_ref, br_ref, wpg_ref, wpp_ref, x1b_ref, comb_ref, res_ref):
    bra = _dot(oa_ref[...].astype(BF16), wba_ref[...])
    brb = _dot(ob_ref[...].astype(BF16), wbb_ref[...])
    gates = g_ref[...]
    mix = gates[:, :D_MODEL] * bra + gates[:, D_MODEL:] * brb
    y = _dot(mix.astype(BF16), wo_ref[...])
    x1 = _layer_norm(ALPHA_DN * x_ref[...] + y, g1_ref[...], b1_ref[...])
    x1b = x1.astype(BF16)
    x1b_ref[...] = x1b

    logits = _dot3(x1, wr_ref[...]) + br_ref[...]
    lane = lax.broadcasted_iota(I32, logits.shape, 1)
    work = logits
    kept = jnp.zeros(logits.shape, jnp.bool_)
    for _ in range(TOP_K):
        mx = jnp.max(work, axis=-1, keepdims=True)
        first = jnp.min(jnp.where(work == mx, lane, N_EXPERTS), axis=-1, keepdims=True)
        hit = lane == first
        kept = kept | hit
        work = jnp.where(hit, -jnp.inf, work)
    top = jnp.max(logits, axis=-1, keepdims=True)
    e = jnp.where(kept, jnp.exp(logits - top), 0.0)
    comb_ref[...] = e / jnp.sum(e, axis=-1, keepdims=True)

    ple = jax.nn.sigmoid(_dot(x1b, wpg_ref[...])) * _dot(p_ref[...].astype(BF16), wpp_ref[...])
    res_ref[...] = ALPHA_DN * x1 + ple


def _merge(x, oa, ob, gates, p, wba, wbb, wo, g1, b1, wr, br, wpg, wpp):
    M = x.shape[0]
    tm = min(M, 512)
    full = lambda a: pl.BlockSpec(a.shape, lambda i: (0,) * a.ndim)
    row = lambda n: pl.BlockSpec((tm, n), lambda i: (i, 0))
    return pl.pallas_call(
        _merge_body,
        grid=(M // tm,),
        in_specs=[row(D_MODEL), row(D_ATT), row(D_ATT), row(2 * D_MODEL), row(PLE_DIM),
                  full(wba), full(wbb), full(wo), full(g1), full(b1), full(wr), full(br), full(wpg), full(wpp)],
        out_specs=[row(D_MODEL), row(N_EXPERTS), row(D_MODEL)],
        out_shape=[jax.ShapeDtypeStruct((M, D_MODEL), BF16),
                   jax.ShapeDtypeStruct((M, N_EXPERTS), F32),
                   jax.ShapeDtypeStruct((M, D_MODEL), F32)],
        compiler_params=_cparams(("parallel",)),
        name="merge",
    )(x, oa, ob, gates, p, wba, wbb, wo, g1, b1, wr, br, wpg, wpp)


MOE_TM = 1024
MOE_RB = 192
MOE_CB = 256


def _moe_body(xb_ref, res_ref, comb_ref, wgu_ref, bgu_ref, wdn_ref, bdn_ref, g2_ref, b2_ref,
              o_ref, rank_ref, rankT_ref, acc_ref):
    e = pl.program_id(1)
    TM, E = comb_ref.shape
    RB = MOE_RB

    @pl.when(e == 0)
    def _():
        routed = jnp.where(comb_ref[...] != 0.0, 1.0, 0.0)
        eye = jnp.where(lax.broadcasted_iota(I32, (E, E), 0) == lax.broadcasted_iota(I32, (E, E), 1),
                        1.0, 0.0).astype(BF16)
        routedT = _dot(eye, routed.astype(BF16), _NT)
        CB = min(MOE_CB, TM)
        r_i = lax.broadcasted_iota(I32, (CB, CB), 0)
        c_i = lax.broadcasted_iota(I32, (CB, CB), 1)
        before = jnp.where(c_i < r_i, 1.0, 0.0).astype(BF16)
        beforeT = jnp.where(r_i < c_i, 1.0, 0.0).astype(BF16)
        off = jnp.zeros((1, E), F32)
        offT = jnp.zeros((E, 1), F32)
        for blk in range(TM // CB):
            rb = routed[blk * CB:(blk + 1) * CB]
            rbT = routedT[:, blk * CB:(blk + 1) * CB]
            rank = _dot(before, rb.astype(BF16)) + off
            rankT = _dot(rbT.astype(BF16), beforeT) + offT
            rank_ref[blk * CB:(blk + 1) * CB, :] = jnp.where(rb > 0.5, rank, -1.0).astype(I32)
            rankT_ref[:, blk * CB:(blk + 1) * CB] = jnp.where(rbT > 0.5, rankT, -1.0).astype(I32)
            off = off + jnp.sum(rb, axis=0, keepdims=True)
            offT = offT + jnp.sum(rbT, axis=1, keepdims=True)
        acc_ref[...] = jnp.zeros(acc_ref.shape, F32)

    lane = lax.broadcasted_iota(I32, (TM, E), 1)
    gate_col = jnp.sum(jnp.where(lane == e, comb_ref[...], 0.0), axis=-1, keepdims=True)
    rank_col = jnp.sum(jnp.where(lane == e, rank_ref[...], 0), axis=-1, keepdims=True)
    rank_row = rankT_ref[pl.ds(e, 1), :]
    n_routed = jnp.sum(jnp.where(rank_row >= 0, 1, 0))
    n_pass = (n_routed + (RB - 1)) // RB
    slot_rows = lax.broadcasted_iota(I32, (RB, TM), 0)
    slot_lanes = lax.broadcasted_iota(I32, (TM, RB), 1)

    def one_pass(pi, carry):
        base = pi * RB
        pick = jnp.where(rank_row == slot_rows + base, 1.0, 0.0).astype(BF16)
        xg = _dot(pick, xb_ref[...]).astype(BF16)
        gu = _dot(xg, wgu_ref[...]) + bgu_ref[...]
        gt = jnp.minimum(gu[:, :D_EXPERT], SWIGLU_LIMIT)
        up = jnp.clip(gu[:, D_EXPERT:], -SWIGLU_LIMIT, SWIGLU_LIMIT)
        hid = (up + 1.0) * gt * jax.nn.sigmoid(SWIGLU_ALPHA * gt)
        down = _dot(hid.astype(BF16), wdn_ref[...])
        place = jnp.where(rank_col == slot_lanes + base, 1.0, 0.0).astype(BF16)
        acc_ref[...] += _dot(place, down.astype(BF16)) * gate_col
        return carry

    lax.fori_loop(0, n_pass, one_pass, 0)

    @pl.when(e == pl.num_programs(1) - 1)
    def _():
        y = acc_ref[...] + _dot3(comb_ref[...], bdn_ref[...])
        o_ref[...] = _layer_norm(res_ref[...] + y, g2_ref[...], b2_ref[...])


def _moe(xb, res, comb, w_gu, b_gu, w_dn, b_dn, g2, b2):
    M = xb.shape[0]
    tm = min(M, MOE_TM)
    E = w_gu.shape[0]
    b_gu3 = b_gu.reshape(E, 1, 2 * D_EXPERT)
    row = lambda n: pl.BlockSpec((tm, n), lambda i, e: (i, 0))
    return pl.pallas_call(
        _moe_body,
        grid=(M // tm, E),
        in_specs=[row(D_MODEL), row(D_MODEL), row(E),
                  pl.BlockSpec((None, D_MODEL, 2 * D_EXPERT), lambda i, e: (e, 0, 0)),
                  pl.BlockSpec((None, 1, 2 * D_EXPERT), lambda i, e: (e, 0, 0)),
                  pl.BlockSpec((None, D_EXPERT, D_MODEL), lambda i, e: (e, 0, 0)),
                  pl.BlockSpec((E, D_MODEL), lambda i, e: (0, 0)),
                  pl.BlockSpec((1, D_MODEL), lambda i, e: (0, 0)),
                  pl.BlockSpec((1, D_MODEL), lambda i, e: (0, 0))],
        out_specs=row(D_MODEL),
        out_shape=jax.ShapeDtypeStruct((M, D_MODEL), F32),
        scratch_shapes=[pltpu.VMEM((tm, E), I32), pltpu.VMEM((E, tm), I32), pltpu.VMEM((tm, D_MODEL), F32)],
        compiler_params=_cparams(("parallel", "arbitrary")),
        name="moe",
    )(xb, res, comb, w_gu, b_gu3, w_dn, b_dn, g2, b2)


SEL_PAGES = 16
ATT_PAGES = 8


def _page_specs(n, rows, width, li):
    def spec(u):
        return pl.BlockSpec((None, None, rows, width),
                            lambda b, c, pt: (li, pt[b, c * n + u], 0, 0))
    return [spec(u) for u in range(n)]


PAGE_ROWS = PAGE_SIZE * N_HEADS


def _smp_dsa_select_body(pt_ref, qi_ref, wi_ref, knew_ref, *rest, n_pages, n_top, idx_bits):
    pages = rest[:SEL_PAGES]
    sel_ref = rest[SEL_PAGES]
    key_ref = rest[SEL_PAGES + 1]
    c = pl.program_id(1)
    T = qi_ref.shape[0] // IDX_HEADS
    qi = qi_ref[...]
    wi = wi_ref[...] * IDX_SCALE

    def page_score(kp):
        s = jnp.maximum(_dot3(qi, kp, _NT), 0.0) * wi
        return jnp.sum(s.reshape(IDX_HEADS, T, PAGE_SIZE), axis=0)

    for u in range(SEL_PAGES):
        key_ref[c * SEL_PAGES + u] = _order_key(page_score(pages[u][...]))

    @pl.when(c == pl.num_programs(1) - 1)
    def _():
        qrow = lax.broadcasted_iota(I32, (T, PAGE_SIZE), 0)
        lane = lax.broadcasted_iota(I32, (T, PAGE_SIZE), 1)
        key_ref[n_pages] = jnp.where(lane <= qrow, _order_key(page_score(knew_ref[...])), INT_MIN)
        keys = key_ref[...]
        shape = keys.shape
        gidx = lax.broadcasted_iota(I32, shape, 0) * PAGE_SIZE + lax.broadcasted_iota(I32, shape, 2)
        valid = (lax.broadcasted_iota(I32, shape, 0) < n_pages) | \
                (lax.broadcasted_iota(I32, shape, 2) <= lax.broadcasted_iota(I32, shape, 1))

        def count(hit):
            per_lane = jnp.sum(hit.astype(I32), axis=0)
            return jnp.sum(per_lane, axis=-1, keepdims=True)[None]

        def thr_bit(bi, prefix):
            cand = prefix | jnp.left_shift(jnp.int32(1), 31 - bi)
            cs = cand ^ INT_MIN
            return jnp.where(count(keys >= cs) >= n_top, cand, prefix)

        thr = lax.fori_loop(0, 32, thr_bit, jnp.zeros((1, T, 1), I32)) ^ INT_MIN
        need = n_top - count(keys > thr)

        def tie_bit(bi, j0):
            cand = j0 | jnp.left_shift(jnp.int32(1), idx_bits - 1 - bi)
            return jnp.where(count((keys == thr) & (gidx < cand)) < need, cand, j0)

        jcut = lax.fori_loop(0, idx_bits, tie_bit, jnp.zeros((1, T, 1), I32))
        sel = ((keys > thr) | ((keys == thr) & (gidx <= jcut))) & valid
        sel_ref[...] = sel.astype(F32)


def _smp_dsa_select(page_table, qi_rows, wi_rows, knew, cache_kidx, li):
    DB, P = page_table.shape
    T = qi_rows.shape[1] // IDX_HEADS
    L = P * PAGE_SIZE + T
    body = functools.partial(_smp_dsa_select_body, n_pages=P, n_top=min(IDX_TOPK, L // 4),
                             idx_bits=max(1, ((P + 1) * PAGE_SIZE - 1).bit_length()))
    grid_spec = pltpu.PrefetchScalarGridSpec(
        num_scalar_prefetch=1,
        grid=(DB, P // SEL_PAGES),
        in_specs=[pl.BlockSpec((None, IDX_HEADS * T, IDX_DIM), lambda b, c, pt: (b, 0, 0)),
                  pl.BlockSpec((None, IDX_HEADS * T, 1), lambda b, c, pt: (b, 0, 0)),
                  pl.BlockSpec((None, PAGE_SIZE, IDX_DIM), lambda b, c, pt: (b, 0, 0))]
                 + _page_specs(SEL_PAGES, PAGE_SIZE, IDX_DIM, li),
        out_specs=pl.BlockSpec((None, P + 1, T, PAGE_SIZE), lambda b, c, pt: (b, 0, 0, 0)),
        scratch_shapes=[pltpu.VMEM((P + 1, T, PAGE_SIZE), I32)],
    )
    return pl.pallas_call(
        body,
        grid_spec=grid_spec,
        out_shape=jax.ShapeDtypeStruct((DB, P + 1, T, PAGE_SIZE), F32),
        compiler_params=_cparams(("parallel", "arbitrary")),
        name="smp_dsa_select",
    )(page_table, qi_rows, wi_rows, knew, *([cache_kidx] * SEL_PAGES))


def _smp_moba_select_body(pt_ref, q_ref, *rest, n_blocks, n_sel, T):
    pages = rest[:SEL_PAGES]
    sel_ref = rest[SEL_PAGES]
    mean_ref = rest[SEL_PAGES + 1]
    c = pl.program_id(1)
    per_step = SEL_PAGES * PAGE_SIZE // MOBA_BLOCK
    per_block = MOBA_BLOCK // PAGE_SIZE

    @pl.when(c == 0)
    def _():
        mean_ref[...] = jnp.zeros(mean_ref.shape, F32)

    for j in range(per_step):
        tot = pages[j * per_block][...]
        for u in range(1, per_block):
            tot = tot + pages[j * per_block + u][...]
        mean_ref[c * per_step + j] = jnp.sum(tot.reshape(PAGE_SIZE, N_HEADS, HEAD_DIM), axis=0) * (1.0 / MOBA_BLOCK)

    @pl.when(c == pl.num_programs(1) - 1)
    def _():
        H = N_HEADS
        means = mean_ref[...].reshape(-1, HEAD_DIM)
        g = _dot3(q_ref[...], means, _NT)
        W = g.shape[1]
        lane = lax.broadcasted_iota(I32, g.shape, 1)
        blk = jnp.right_shift(lane, int(math.log2(H)))
        g = jnp.where(blk < n_blocks, g, -jnp.inf)
        rank = jnp.zeros(g.shape, F32)
        for r in range(1, W // H):
            other = pltpu.roll(g, r * H, axis=1)
            rank = rank + jnp.where(blk >= r, jnp.where(other >= g, 1.0, 0.0), jnp.where(other > g, 1.0, 0.0))
        own_head = (lane & (H - 1)) == jnp.right_shift(lax.broadcasted_iota(I32, g.shape, 0), int(math.log2(T)))
        chosen = jnp.where((rank < n_sel) & (blk < n_blocks) & own_head, 1.0, 0.0).astype(BF16)
        fold = jnp.where(jnp.right_shift(lax.broadcasted_iota(I32, (W, sel_ref.shape[-1]), 0), int(math.log2(H)))
                         == lax.broadcasted_iota(I32, (W, sel_ref.shape[-1]), 1), 1.0, 0.0).astype(BF16)
        sel_ref[...] = _dot(chosen, fold)


MAX_SMP_BLOCKS = 32


def _smp_moba_select(page_table, q_rows, cache_k, li):
    DB, P = page_table.shape
    n_blocks = P * PAGE_SIZE // MOBA_BLOCK
    per_step = SEL_PAGES * PAGE_SIZE // MOBA_BLOCK
    n_steps = P // SEL_PAGES
    W = 128
    assert n_blocks <= MAX_SMP_BLOCKS
    R = q_rows.shape[1]
    T = R // N_HEADS
    assert T & (T - 1) == 0
    body = functools.partial(_smp_moba_select_body, n_blocks=n_blocks, n_sel=min(MOBA_TOPK, n_blocks), T=T)
    grid_spec = pltpu.PrefetchScalarGridSpec(
        num_scalar_prefetch=1,
        grid=(DB, n_steps),
        in_specs=[pl.BlockSpec((None, R, HEAD_DIM), lambda b, c, pt: (b, 0, 0))]
                 + _page_specs(SEL_PAGES, PAGE_ROWS, HEAD_DIM, li),
        out_specs=pl.BlockSpec((None, R, W), lambda b, c, pt: (b, 0, 0)),
        scratch_shapes=[pltpu.VMEM((MAX_SMP_BLOCKS, N_HEADS, HEAD_DIM), F32)],
    )
    return pl.pallas_call(
        body,
        grid_spec=grid_spec,
        out_shape=jax.ShapeDtypeStruct((DB, R, W), F32),
        compiler_params=_cparams(("parallel", "arbitrary")),
        name="smp_moba_select",
    )(page_table, q_rows, *([cache_k] * SEL_PAGES))


def _smp_attn_body(pt_ref, q_ref, rbr_ref, spread_ref, knew_ref, vnew_ref, sel_ref, selnew_ref, *rest, mode, n_pages):
    kp = rest[:ATT_PAGES]
    vp = rest[ATT_PAGES:2 * ATT_PAGES]
    o_ref, m_ref, l_ref, acc_ref = rest[2 * ATT_PAGES:]
    c = pl.program_id(1)
    H = N_HEADS
    R = q_ref.shape[0]
    T = R // H
    NL = knew_ref.shape[0]
    q = (q_ref[...] * ATTN_SCALE).astype(BF16)
    rbr = rbr_ref[...]
    far_bias = rbr[:, N_BUCKETS - 1:]

    def geometry(width):
        row = lax.broadcasted_iota(I32, (R, width), 0)
        lane = lax.broadcasted_iota(I32, (R, width), 1)
        own = (lane & (H - 1)) == jnp.right_shift(row, int(math.log2(T)))
        return row & (T - 1), jnp.right_shift(lane, int(math.log2(H))), jnp.where(own, 0.0, NEG_BIG)

    rowq, key, head_pen = geometry(PAGE_ROWS)

    @pl.when(c == 0)
    def _():
        m_ref[...] = jnp.full(m_ref.shape, NEG_BIG, F32)
        l_ref[...] = jnp.zeros(l_ref.shape, F32)
        acc_ref[...] = jnp.zeros(acc_ref.shape, F32)

    def bias_for(dist):
        return _bias_chain(dist, lambda b: rbr[:, b:b + 1])

    def spread_sel(sel):
        wide = _dot(sel.astype(BF16), spread_ref[...])
        return jnp.where(jnp.tile(wide, (H, 1)) > 0.5, 0.0, NEG_BIG)

    def update(s, v):
        m_old = m_ref[...]
        m_new = jnp.maximum(m_old, jnp.max(s, axis=-1, keepdims=True))
        p = jnp.exp(s - m_new)
        alpha = jnp.exp(m_old - m_new)
        l_ref[...] = alpha * l_ref[...] + jnp.sum(p, axis=-1, keepdims=True)
        acc_ref[...] = alpha * acc_ref[...] + _dot(p.astype(BF16), v)
        m_ref[...] = m_new

    logits, vals = [], []
    for u in range(ATT_PAGES):
        page = c * ATT_PAGES + u
        s = _dot(q, kp[u][...].astype(BF16), _NT)
        if u == ATT_PAGES - 1:
            dist = (n_pages - page) * PAGE_SIZE + rowq - key
            bias = lax.cond(c == pl.num_programs(1) - 1, lambda: bias_for(dist),
                            lambda: jnp.broadcast_to(far_bias, (R, PAGE_ROWS)))
        else:
            bias = far_bias
        if mode == "dsa":
            pen = spread_sel(sel_ref[u])
        else:
            blk = page // (MOBA_BLOCK // PAGE_SIZE)
            sel = sel_ref[...]
            pick = jnp.sum(jnp.where(lax.broadcasted_iota(I32, sel.shape, 1) == blk, sel, 0.0),
                           axis=-1, keepdims=True)
            pen = jnp.where(pick > 0.5, 0.0, NEG_BIG)
        logits.append(s + bias + (pen + head_pen))
        vals.append(vp[u][...].astype(BF16))
    update(jnp.concatenate(logits, axis=1), jnp.concatenate(vals, axis=0))

    @pl.when(c == pl.num_programs(1) - 1)
    def _():
        rowq_n, key_n, head_pen_n = geometry(NL)
        s = _dot(q, knew_ref[...].astype(BF16), _NT) + bias_for(rowq_n - key_n)
        pen = jnp.where(key_n <= rowq_n, head_pen_n, NEG_BIG)
        if mode == "dsa":
            pen = pen + spread_sel(selnew_ref[...])[:, :NL]
        update(s + pen, vnew_ref[...].astype(BF16))
        out = acc_ref[...] / l_ref[...]
        for h in range(H):
            o_ref[:, h * HEAD_DIM:(h + 1) * HEAD_DIM] = out[h * T:(h + 1) * T, :]


def _smp_attn(page_table, q_rows, rbr, knew, vnew, sel, cache_k, cache_v, li, *, mode):
    DB, P = page_table.shape
    R = q_rows.shape[1]
    T = R // N_HEADS
    assert T & (T - 1) == 0 and N_HEADS & (N_HEADS - 1) == 0
    spread = jnp.repeat(jnp.eye(PAGE_SIZE, dtype=BF16), N_HEADS, axis=1)
    if mode == "dsa":
        sel_specs = [pl.BlockSpec((None, ATT_PAGES, T, PAGE_SIZE), lambda b, c, pt: (b, c, 0, 0)),
                     pl.BlockSpec((None, None, T, PAGE_SIZE), lambda b, c, pt: (b, P, 0, 0))]
    else:
        sel_specs = [pl.BlockSpec((None, R, sel.shape[-1]), lambda b, c, pt: (b, 0, 0)),
                     pl.BlockSpec((None, R, sel.shape[-1]), lambda b, c, pt: (b, 0, 0))]
    new_spec = pl.BlockSpec((None, T * N_HEADS, HEAD_DIM), lambda b, c, pt: (b, 0, 0))
    grid_spec = pltpu.PrefetchScalarGridSpec(
        num_scalar_prefetch=1,
        grid=(DB, P // ATT_PAGES),
        in_specs=[pl.BlockSpec((None, R, HEAD_DIM), lambda b, c, pt: (b, 0, 0)),
                  pl.BlockSpec(rbr.shape, lambda b, c, pt: (0, 0)),
                  pl.BlockSpec(spread.shape, lambda b, c, pt: (0, 0)),
                  new_spec, new_spec]
                 + sel_specs + _page_specs(ATT_PAGES, PAGE_ROWS, HEAD_DIM, li) + _page_specs(ATT_PAGES, PAGE_ROWS, HEAD_DIM, li),
        out_specs=pl.BlockSpec((None, T, D_ATT), lambda b, c, pt: (b, 0, 0)),
        scratch_shapes=[pltpu.VMEM((R, 1), F32), pltpu.VMEM((R, 1), F32), pltpu.VMEM((R, HEAD_DIM), F32)],
    )
    return pl.pallas_call(
        functools.partial(_smp_attn_body, mode=mode, n_pages=P),
        grid_spec=grid_spec,
        out_shape=jax.ShapeDtypeStruct((DB, T, D_ATT), F32),
        compiler_params=_cparams(("parallel", "arbitrary")),
        name="smp_attn_" + mode,
    )(page_table, q_rows, rbr, spread, knew, vnew, sel, sel, *([cache_k] * ATT_PAGES), *([cache_v] * ATT_PAGES))


def _project(x2, w, g, b):
    o = IN_OFFS
    qkv_a = _mm(x2, w[:, o[0]:o[3]])
    qi, ki, kcat, wi = _proj_idx(x2, w[:, o[3]:o[4]], w[:, o[4]:o[5]], w[:, o[5]:o[6]], g[None], b[None])
    qk_b = _mm(x2, w[:, o[6]:o[8]], passes=3)
    v_b = _mm(x2, w[:, o[8]:o[9]])
    gates = _mm(x2, w[:, o[9]:o[10]], act="sigmoid")
    return dict(q_a=qkv_a[:, :D_ATT], k_a=qkv_a[:, D_ATT:2 * D_ATT], v_a=qkv_a[:, 2 * D_ATT:],
                q_i=qi, k_i=ki, kcat=kcat, w_i=wi, q_b=qk_b[:, :D_ATT], k_b=qk_b[:, D_ATT:], v_b=v_b, gates=gates)


def _project_prompt(x2, w, g, b):
    o = IN_OFFS
    wT = lambda lo, hi: w[:, lo:hi].T
    k_a, k_a16 = _mm(x2, w[:, o[1]:o[2]], also_bf16=True)
    k_i, kcat = _proj_kidx(x2, w[:, o[4]:o[5]], g[None], b[None])
    k_b, k_b16 = _mm(x2, w[:, o[7]:o[8]], passes=3, also_bf16=True)
    return dict(
        q_aT=_mmT(wT(o[0], o[1]), x2, scales=(ATTN_SCALE,), out_dtype=BF16),
        k_a=k_a, k_a16=k_a16,
        v_a=_mm(x2, w[:, o[2]:o[3]]), v_aT=_mmT(wT(o[2], o[3]), x2, out_dtype=BF16),
        q_iT=_mmT(wT(o[3], o[4]), x2, passes=3),
        k_i=k_i, kcat=kcat,
        w_iT=_mmT(wT(o[5], o[6]), x2, passes=3, scales=(IDX_HEADS ** -0.5, IDX_SCALE)),
        q_bT=_mmT(wT(o[6], o[7]), x2, passes=3),
        k_b=k_b, k_b16=k_b16,
        v_b=_mm(x2, w[:, o[8]:o[9]]), v_bT=_mmT(wT(o[8], o[9]), x2, out_dtype=BF16),
        gates=_mm(x2, w[:, o[9]:o[10]], act="sigmoid"))


def _heads_major(a, B, S):
    return a.reshape(B, S, N_HEADS, HEAD_DIM).transpose(0, 2, 1, 3)


def _block_diag_rows(q):
    DB, T, H, Dh = q.shape
    eye = jnp.eye(H, dtype=q.dtype)
    return jnp.einsum("bthd,hg->bhtgd", q, eye).reshape(DB, H * T, H * Dh)


def _prompt_mixers(pr, B, S, rb_a, rb_b, T_dsa):
    bt_a = _bias_tiles(rb_a, T_dsa)
    o_a = _dsa_prompt(rb_a[N_BUCKETS - 1], bt_a, pr["q_aT"], pr["k_a16"], pr["v_aT"],
                      pr["q_iT"], pr["w_iT"], pr["kcat"], B=B, T=T_dsa)
    T = MOBA_BLOCK
    nf = S // T
    n_slots = 128 // N_HEADS
    assert nf <= n_slots
    means = _block_means(pr["k_b"].reshape(B, S, D_ATT)).reshape(B, nf, N_HEADS, HEAD_DIM)
    eye = jnp.eye(N_HEADS, dtype=F32)
    mbdT = jnp.einsum("bjhd,hg->bjghd", means, eye)
    mbdT = jnp.pad(mbdT, ((0, 0), (0, n_slots - nf), (0, 0), (0, 0), (0, 0))).reshape(B, n_slots * N_HEADS, D_ATT)
    bt_b = _bias_tiles(rb_b, T)
    o_b = _moba_prompt(rb_b[N_BUCKETS - 1], bt_b, pr["q_bT"], pr["k_b16"], pr["v_bT"], mbdT, B=B)
    return o_a, o_b


def _sample_mixers(sm, DB, T, caches, page_table, rb_a, rb_b, li):
    ck_a, cv_a, ck_i, ck_b, cv_b = caches
    pad_rows = lambda a: jnp.pad(a.reshape(DB, T, -1), ((0, 0), (0, PAGE_SIZE - T), (0, 0)))
    qi_rows = _heads_major(sm["q_i"], DB, T).reshape(DB, IDX_HEADS * T, IDX_DIM)
    wi_rows = sm["w_i"].reshape(DB, T, IDX_HEADS).transpose(0, 2, 1).reshape(DB, IDX_HEADS * T, 1)
    sel_a = _smp_dsa_select(page_table, qi_rows, wi_rows, pad_rows(sm["k_i"]), ck_i, li)
    rbr_a = jnp.repeat(rb_a.T, T, axis=0)
    rbr_b = jnp.repeat(rb_b.T, T, axis=0)
    head_rows = lambda a: _heads_major(a, DB, T).reshape(DB, N_HEADS * T, HEAD_DIM)
    key_rows = lambda a: a.reshape(DB, T * N_HEADS, HEAD_DIM)
    o_a = _smp_attn(page_table, head_rows(sm["q_a"]), rbr_a, key_rows(sm["k_a"]), key_rows(sm["v_a"]), sel_a,
                    ck_a, cv_a, li, mode="dsa")
    q_b = head_rows(sm["q_b"])
    sel_b = _smp_moba_select(page_table, q_b, ck_b, li)
    o_b = _smp_attn(page_table, q_b, rbr_b, key_rows(sm["k_b"]), key_rows(sm["v_b"]), sel_b,
                    ck_b, cv_b, li, mode="moba")
    return o_a.reshape(DB * T, D_ATT), o_b.reshape(DB * T, D_ATT)


def kernel(x_prompt, x_sample, cache_k_a, cache_v_a, cache_kidx, cache_k_b, cache_v_b, page_table, p_prompt, p_sample, rel_bias, w_in, kidx_ln_g, kidx_ln_b, w_branch_a, w_branch_b, w_out, ln1_g, ln1_b, w_router, b_router, w_gate_up, b_gate_up, w_down, b_down, w_ple_gate, w_ple_proj, ln2_g, ln2_b):
    B, S, D = x_prompt.shape
    DB, T, _ = x_sample.shape
    depth = w_in.shape[0]
    n_pool = cache_k_a.shape[1]
    rb_a = rel_bias[:, :N_HEADS]
    rb_b = rel_bias[:, N_HEADS:]
    rows = lambda c: c.reshape(depth, n_pool, PAGE_ROWS, HEAD_DIM)
    caches = (rows(cache_k_a), rows(cache_v_a), cache_kidx, rows(cache_k_b), rows(cache_v_b))
    T_dsa = min(256, S)

    xp = x_prompt.reshape(B * S, D)
    xs = x_sample.reshape(DB * T, D)
    rows_p, rows_s = [], []
    for li in range(depth):
        bf = lambda a: a[li].astype(BF16)
        merge_w = (bf(w_branch_a), bf(w_branch_b), bf(w_out), ln1_g[li][None], ln1_b[li][None],
                   w_router[li], b_router[li][None], bf(w_ple_gate), bf(w_ple_proj))
        moe_w = (bf(w_gate_up), b_gate_up[li], bf(w_down), b_down[li], ln2_g[li][None], ln2_b[li][None])

        pr = _project_prompt(xp, w_in[li], kidx_ln_g[li], kidx_ln_b[li])
        o_a, o_b = _prompt_mixers(pr, B, S, rb_a, rb_b, T_dsa)
        x1b, comb, res = _merge(xp, o_a, o_b, pr["gates"], p_prompt[li].reshape(B * S, -1), *merge_w)
        xp = _moe(x1b, res, comb, *moe_w)
        rows_p.append(pr)

        sm = _project(xs, w_in[li], kidx_ln_g[li], kidx_ln_b[li])
        o_a, o_b = _sample_mixers(sm, DB, T, caches, page_table, rb_a, rb_b, li)
        x1b, comb, res = _merge(xs, o_a, o_b, sm["gates"], p_sample[li].reshape(DB * T, -1), *merge_w)
        xs = _moe(x1b, res, comb, *moe_w)
        rows_s.append(sm)

    def stack(rows, name, lead, tail):
        return jnp.stack([r[name].reshape(lead + tail) for r in rows])

    hd = (N_HEADS, HEAD_DIM)
    outs = [xp.reshape(B, S, D), xs.reshape(DB, T, D)]
    for rows, lead in ((rows_p, (B, S)), (rows_s, (DB, T))):
        outs += [stack(rows, "k_a", lead, hd), stack(rows, "v_a", lead, hd), stack(rows, "k_i", lead, (IDX_DIM,)),
                 stack(rows, "k_b", lead, hd), stack(rows, "v_b", lead, hd)]
    return tuple(outs)
```

```python
import functools
import math

import numpy as np
import jax
import jax.numpy as jnp
from jax import lax
from jax.experimental import pallas as pl
from jax.experimental.pallas import tpu as pltpu

F32 = jnp.float32
BF16 = jnp.bfloat16
I32 = jnp.int32

D_MODEL = 1024
HEAD_DIM = 64
N_HEADS = 8
D_ATT = N_HEADS * HEAD_DIM
IDX_HEADS = 8
IDX_DIM = 64
IDX_TOPK = 256
MOBA_BLOCK = 256
MOBA_TOPK = 3
N_BUCKETS = 32
MAX_DISTANCE = 128
N_EXPERTS = 32
TOP_K = 4
D_EXPERT = D_MODEL
SWIGLU_LIMIT = 7.0
SWIGLU_ALPHA = 1.702
PLE_DIM = 256
LN_EPS = 1e-5
PAGE_SIZE = 128
DEPTH = 2
ALPHA_DN = (2 * DEPTH) ** 0.25
ATTN_SCALE = HEAD_DIM ** -0.5
IDX_SCALE = IDX_DIM ** -0.5
IN_SIZES = (D_ATT, D_ATT, D_ATT, IDX_HEADS * IDX_DIM, IDX_DIM, IDX_HEADS, D_ATT, D_ATT, D_ATT, 2 * D_MODEL)
IN_OFFS = tuple(int(v) for v in np.cumsum((0,) + IN_SIZES))

INT_MIN = -(2 ** 31)
NEG_BIG = -1e30
VMEM_LIMIT_BYTES = 56 * 1024 * 1024


def _bucket_of(d):
    d = max(d, 0)
    max_exact = N_BUCKETS // 2
    if d < max_exact:
        return d
    ratio = math.log(d / max_exact) / math.log(MAX_DISTANCE / max_exact)
    return min(max_exact + int(ratio * (N_BUCKETS - max_exact)), N_BUCKETS - 1)


_BUCKET_HI = tuple(max(d for d in range(4 * MAX_DISTANCE) if _bucket_of(d) == b) for b in range(N_BUCKETS - 1))
FAR_DIST = _BUCKET_HI[-1] + 1


def _cparams(sem):
    return pltpu.CompilerParams(dimension_semantics=sem, vmem_limit_bytes=VMEM_LIMIT_BYTES)


def _split(x):
    hi = x.astype(BF16)
    lo = (x - hi.astype(F32)).astype(BF16)
    return hi, lo


_NN = (((1,), (0,)), ((), ()))
_NT = (((1,), (1,)), ((), ()))


def _dot(a, b, dims=_NN):
    return lax.dot_general(a, b, dims, preferred_element_type=F32)


def _dot3(a, b, dims=_NN):
    ah, al = _split(a)
    bh, bl = _split(b)
    return _dot(al, bh, dims) + _dot(ah, bl, dims) + _dot(ah, bh, dims)


def _layer_norm(x, g, b):
    mu = jnp.mean(x, axis=-1, keepdims=True)
    xc = x - mu
    var = jnp.mean(xc * xc, axis=-1, keepdims=True)
    return xc * lax.rsqrt(var + LN_EPS) * g + b


def _order_key(s):
    s = jnp.where(s == 0.0, 0.0, s)
    u = pltpu.bitcast(s, I32)
    return u ^ (jnp.right_shift(u, 31) & 0x7FFFFFFF)


def _bias_chain(d, rb_of):
    val = rb_of(N_BUCKETS - 1)
    for b in range(N_BUCKETS - 2, -1, -1):
        val = jnp.where(d <= _BUCKET_HI[b], rb_of(b), val)
    return val


def _mm_body(x_ref, wT_ref, o_ref, *, passes, act):
    x = x_ref[...]
    wT = wT_ref[...]
    y = _dot(x.astype(BF16), wT.astype(BF16), _NT) if passes == 1 else _dot3(x, wT, _NT)
    if act == "sigmoid":
        y = jax.nn.sigmoid(y)
    o_ref[...] = y.astype(o_ref.dtype)


def _mm(x, wT, *, passes=1, act=None, out_dtype=F32):
    M, K = x.shape
    N = wT.shape[0]
    tm = min(M, 512)
    tn = min(N, 512)
    return pl.pallas_call(
        functools.partial(_mm_body, passes=passes, act=act),
        grid=(M // tm, N // tn),
        in_specs=[pl.BlockSpec((tm, K), lambda i, j: (i, 0)),
                  pl.BlockSpec((tn, K), lambda i, j: (j, 0))],
        out_specs=pl.BlockSpec((tm, tn), lambda i, j: (i, j)),
        out_shape=jax.ShapeDtypeStruct((M, N), out_dtype),
        compiler_params=_cparams(("parallel", "parallel")),
        name="proj_mm",
    )(x, wT)


def _mmT_body(wT_ref, x_ref, *o_refs, passes, scales, n_transposed):
    wT = wT_ref[...]
    x = x_ref[...]
    y = _dot(wT.astype(BF16), x.astype(BF16), _NT) if passes == 1 else _dot3(wT, x, _NT)
    for s in scales:
        y = y * s
    for o_ref in o_refs[:n_transposed]:
        o_ref[...] = y.astype(o_ref.dtype)
    for o_ref in o_refs[n_transposed:]:
        o_ref[...] = y.T.astype(o_ref.dtype)


def _mmT(wT, x, B, *, passes=1, scales=(), out_dtypes=(F32,), natural_dtypes=()):
    M, K = x.shape
    S = M // B
    N = wT.shape[0]
    tm = min(S, 512)
    tn = min(N, 512)
    nt = S // tm
    specT = pl.BlockSpec((None, tn, tm), lambda i, j: (i // nt, j, i % nt))
    spec = pl.BlockSpec((tm, tn), lambda i, j: (i, j))
    return pl.pallas_call(
        functools.partial(_mmT_body, passes=passes, scales=scales, n_transposed=len(out_dtypes)),
        grid=(M // tm, N // tn),
        in_specs=[pl.BlockSpec((tn, K), lambda i, j: (j, 0)),
                  pl.BlockSpec((tm, K), lambda i, j: (i, 0))],
        out_specs=[specT] * len(out_dtypes) + [spec] * len(natural_dtypes),
        out_shape=[jax.ShapeDtypeStruct((B, N, S), d) for d in out_dtypes]
                  + [jax.ShapeDtypeStruct((M, N), d) for d in natural_dtypes],
        compiler_params=_cparams(("parallel", "parallel")),
        name="proj_mmT",
    )(wT, x)


def _kcat(k):
    kh, kl = _split(k)
    return jnp.concatenate([kh, kl, kh, jnp.zeros_like(kh)], axis=-1)


def _proj_kidx_body(x_ref, wkT_ref, g_ref, b_ref, gc_ref, bc_ref, kiT_ref, kcat_ref):
    x = x_ref[...]
    wkT = wkT_ref[...]
    kcat_ref[...] = _kcat(_layer_norm(_dot3(x, wkT, _NT), g_ref[...], b_ref[...]))
    kT = _dot3(wkT, x, _NT)
    mu = jnp.mean(kT, axis=0, keepdims=True)
    kc = kT - mu
    var = jnp.mean(kc * kc, axis=0, keepdims=True)
    kiT_ref[...] = kc * lax.rsqrt(var + LN_EPS) * gc_ref[...] + bc_ref[...]


def _proj_kidx(x, wkT, g, b, B):
    M, K = x.shape
    S = M // B
    tm = min(S, 512)
    nt = S // tm
    full = lambda a: pl.BlockSpec(a.shape, lambda i: (0,) * a.ndim)
    return pl.pallas_call(
        _proj_kidx_body,
        grid=(M // tm,),
        in_specs=[pl.BlockSpec((tm, K), lambda i: (i, 0)), full(wkT),
                  pl.BlockSpec((1, IDX_DIM), lambda i: (0, 0)), pl.BlockSpec((1, IDX_DIM), lambda i: (0, 0)),
                  pl.BlockSpec((IDX_DIM, 1), lambda i: (0, 0)), pl.BlockSpec((IDX_DIM, 1), lambda i: (0, 0))],
        out_specs=[pl.BlockSpec((None, IDX_DIM, tm), lambda i: (i // nt, 0, i % nt)),
                   pl.BlockSpec((tm, 4 * IDX_DIM), lambda i: (i, 0))],
        out_shape=[jax.ShapeDtypeStruct((B, IDX_DIM, S), F32),
                   jax.ShapeDtypeStruct((M, 4 * IDX_DIM), BF16)],
        compiler_params=_cparams(("parallel",)),
        name="proj_kidx",
    )(x, wkT, g[None], b[None], g[:, None], b[:, None])


def _proj_idx_body(x_ref, wqT_ref, wkT_ref, wwT_ref, g_ref, b_ref, qi_ref, ki_ref, wi_ref):
    x = x_ref[...]
    qi_ref[...] = _dot3(x, wqT_ref[...], _NT)
    ki_ref[...] = _layer_norm(_dot3(x, wkT_ref[...], _NT), g_ref[...], b_ref[...])
    wi_ref[...] = _dot3(x, wwT_ref[...], _NT) * (IDX_HEADS ** -0.5)


def _proj_idx(x, wqT, wkT, wwT, g, b):
    M, K = x.shape
    tm = min(M, 512)
    full = lambda a: pl.BlockSpec(a.shape, lambda i: (0,) * a.ndim)
    row = lambda n: pl.BlockSpec((tm, n), lambda i: (i, 0))
    return pl.pallas_call(
        _proj_idx_body,
        grid=(M // tm,),
        in_specs=[row(K), full(wqT), full(wkT), full(wwT), full(g), full(b)],
        out_specs=[row(IDX_HEADS * IDX_DIM), row(IDX_DIM), row(IDX_HEADS)],
        out_shape=[jax.ShapeDtypeStruct((M, IDX_HEADS * IDX_DIM), F32),
                   jax.ShapeDtypeStruct((M, IDX_DIM), F32),
                   jax.ShapeDtypeStruct((M, IDX_HEADS), F32)],
        compiler_params=_cparams(("parallel",)),
        name="proj_idx",
    )(x, wqT, wkT, wwT, g, b)


def _bias_tiles_body(rb_ref, o_ref, *, T):
    rel = pl.program_id(0)
    h = pl.program_id(1)
    key = lax.broadcasted_iota(I32, (T, T), 0)
    qry = lax.broadcasted_iota(I32, (T, T), 1)
    d = qry - key + rel * T
    o_ref[...] = _bias_chain(d, lambda b: rb_ref[b, h])


def _bias_tiles(rb, T):
    H = rb.shape[1]
    return pl.pallas_call(
        functools.partial(_bias_tiles_body, T=T),
        grid=(2, H),
        in_specs=[pl.BlockSpec(memory_space=pltpu.SMEM)],
        out_specs=pl.BlockSpec((None, None, T, T), lambda r, h: (r, h, 0, 0)),
        out_shape=jax.ShapeDtypeStruct((2, H, T, T), F32),
        compiler_params=_cparams(("parallel", "parallel")),
        name="bias_tiles",
    )(rb)


def _attn_update(logits_of, pen_of, vT_of, m_ref, l_ref, acc_ref):
    heads = range(N_HEADS)
    s = [logits_of(h) + pen_of(h) for h in heads]
    m_old = [m_ref[h] for h in heads]
    m_new = [jnp.maximum(m_old[h], jnp.max(s[h], axis=0, keepdims=True)) for h in heads]
    p = [jnp.exp(s[h] - m_new[h]) for h in heads]
    alpha = [jnp.exp(m_old[h] - m_new[h]) for h in heads]
    for h in heads:
        l_ref[h] = alpha[h] * l_ref[h] + jnp.sum(p[h], axis=0, keepdims=True)
        m_ref[h] = m_new[h]
    pv = [_dot(vT_of(h), p[h].astype(BF16)) for h in heads]
    for h in heads:
        acc_ref[h] = alpha[h] * acc_ref[h] + pv[h]


def _attn_init(m_ref, l_ref, acc_ref):
    m_ref[...] = jnp.full(m_ref.shape, NEG_BIG, F32)
    l_ref[...] = jnp.zeros(l_ref.shape, F32)
    acc_ref[...] = jnp.zeros(acc_ref.shape, F32)


def _attn_finish(o_ref, l_ref, acc_ref):
    outT = jnp.concatenate([acc_ref[h] / l_ref[h] for h in range(N_HEADS)], axis=0)
    o_ref[...] = outT.T


def _padded_heads(qT_of):
    out = []
    for h in range(N_HEADS):
        q = qT_of(h)
        z = jnp.zeros_like(q)
        out.append(jnp.concatenate([q, z] if h % 2 == 0 else [z, q], axis=0))
    return out


def _head_pair(k, h):
    lo = (h // 2) * 2 * HEAD_DIM
    return k[:, lo:lo + 2 * HEAD_DIM]


def _dsa_prompt_body(rbf_ref, bt_ref, qa_ref, k_ref, vT_ref, qi_ref, wi_ref, kcat_ref, o_ref,
                     key_ref, m_ref, l_ref, acc_ref, *, T, n_top, idx_bits):
    i = pl.program_id(1)
    H = N_HEADS
    kpos = lax.broadcasted_iota(I32, (T, T), 0)
    qpos = lax.broadcasted_iota(I32, (T, T), 1)

    wi = wi_ref[...]
    qcat = []
    for h in range(IDX_HEADS):
        qh, ql = _split(qi_ref[h * IDX_DIM:(h + 1) * IDX_DIM, :])
        qcat.append(jnp.concatenate([qh, qh, ql, jnp.zeros_like(qh)], axis=0))

    def score_chunk(kc, carry):
        kcat = kcat_ref[pl.ds(pl.multiple_of(kc * T, T), T), :]
        s = jnp.zeros((T, T), F32)
        for h in range(IDX_HEADS):
            s = s + jnp.maximum(_dot(kcat, qcat[h]), 0.0) * wi[h:h + 1, :]
        key = _order_key(s)
        key_ref[kc] = jnp.where((kc == i) & (kpos > qpos), INT_MIN, key)
        return carry

    lax.fori_loop(0, i + 1, score_chunk, 0)

    def count(pred):
        def body(kc, acc):
            hit = jnp.where(pred(key_ref[kc], kc * T + kpos), 1.0, 0.0)
            return acc + jnp.sum(hit.reshape(T // 8, 8, T), axis=0)
        acc = lax.fori_loop(0, i + 1, body, jnp.zeros((8, T), F32))
        return jnp.sum(acc, axis=0, keepdims=True)

    def thr_bit(bi, prefix):
        cand = prefix | jnp.left_shift(jnp.int32(1), 31 - bi)
        cs = cand ^ INT_MIN
        return jnp.where(count(lambda k, g: k >= cs) >= n_top, cand, prefix)

    thr = lax.fori_loop(0, 32, thr_bit, jnp.zeros((1, T), I32)) ^ INT_MIN
    need = n_top - count(lambda k, g: k > thr)
    n_tie = count(lambda k, g: k == thr)

    def tie_search():
        def bit(bi, j0):
            cand = j0 | jnp.left_shift(jnp.int32(1), idx_bits - 1 - bi)
            c = count(lambda k, g: (k == thr) & (g < cand))
            return jnp.where(c < need, cand, j0)
        return lax.fori_loop(0, idx_bits, bit, jnp.zeros((1, T), I32))

    any_excess = jnp.max(n_tie - need) > 0.0
    jcut = lax.cond(any_excess, tie_search, lambda: jnp.full((1, T), 2 ** 30, I32))

    _attn_init(m_ref, l_ref, acc_ref)
    q = _padded_heads(lambda h: qa_ref[h * HEAD_DIM:(h + 1) * HEAD_DIM, :])

    def attn_chunk(kc, near):
        kk = k_ref[pl.ds(pl.multiple_of(kc * T, T), T), :]
        keys = key_ref[kc]
        g = kc * T + kpos
        sel = (keys > thr) | ((keys == thr) & (g <= jcut))
        if near:
            sel = sel & (g <= i * T + qpos)
        pen = jnp.where(sel, 0.0, NEG_BIG)
        _attn_update(
            lambda h: _dot(_head_pair(kk, h), q[h]) + (bt_ref[i - kc, h] if near else rbf_ref[h]),
            lambda h: pen,
            lambda h: vT_ref[h * HEAD_DIM:(h + 1) * HEAD_DIM, pl.ds(pl.multiple_of(kc * T, T), T)],
            m_ref, l_ref, acc_ref)

    n_far = jnp.maximum(i - 1, 0)
    lax.fori_loop(0, n_far, lambda kc, c: (attn_chunk(kc, False), c)[1], 0)
    lax.fori_loop(n_far, i + 1, lambda kc, c: (attn_chunk(kc, True), c)[1], 0)
    _attn_finish(o_ref, l_ref, acc_ref)


def _attn_scratch(T):
    return [pltpu.VMEM((N_HEADS, 1, T), F32), pltpu.VMEM((N_HEADS, 1, T), F32),
            pltpu.VMEM((N_HEADS, HEAD_DIM, T), F32)]


def _dsa_prompt(rb_far, btiles, qaT, k16, vT, qiT, wiT, kcat, *, T):
    B, _, S = qaT.shape
    M = B * S
    H = N_HEADS
    nc = S // T
    n_top = min(IDX_TOPK, S // 4)
    body = functools.partial(_dsa_prompt_body, T=T, n_top=n_top, idx_bits=max(1, (S - 1).bit_length()))
    tile = lambda rows: pl.BlockSpec((None, rows, T), lambda b, i: (b, 0, i))
    return pl.pallas_call(
        body,
        grid=(B, nc),
        in_specs=[pl.BlockSpec(memory_space=pltpu.SMEM),
                  pl.BlockSpec((2, H, T, T), lambda b, i: (0, 0, 0, 0)),
                  tile(D_ATT),
                  pl.BlockSpec((S, D_ATT), lambda b, i: (b, 0)),
                  pl.BlockSpec((None, D_ATT, S), lambda b, i: (b, 0, 0)),
                  tile(IDX_HEADS * IDX_DIM),
                  tile(IDX_HEADS),
                  pl.BlockSpec((S, 4 * IDX_DIM), lambda b, i: (b, 0))],
        out_specs=pl.BlockSpec((T, D_ATT), lambda b, i: (b * nc + i, 0)),
        out_shape=jax.ShapeDtypeStruct((M, D_ATT), F32),
        scratch_shapes=[pltpu.VMEM((nc, T, T), I32)] + _attn_scratch(T),
        compiler_params=_cparams(("parallel", "arbitrary")),
        name="dsa_prompt",
    )(rb_far, btiles, qaT, k16, vT, qiT, wiT, kcat)


def _block_means_body(kT_ref, o_ref):
    nf = o_ref.shape[-1]
    lane = lax.broadcasted_iota(I32, o_ref.shape, 1)
    means = jnp.zeros(o_ref.shape, F32)
    for j in range(nf):
        col = jnp.sum(kT_ref[:, j * MOBA_BLOCK:(j + 1) * MOBA_BLOCK], axis=1, keepdims=True) * (1.0 / MOBA_BLOCK)
        means = jnp.where(lane == j, col, means)
    o_ref[...] = means


def _block_means(kT):
    B, D, L = kT.shape
    nf = L // MOBA_BLOCK
    return pl.pallas_call(
        _block_means_body,
        grid=(B,),
        in_specs=[pl.BlockSpec((None, D, L), lambda b: (b, 0, 0))],
        out_specs=pl.BlockSpec((None, D, nf), lambda b: (b, 0, 0)),
        out_shape=jax.ShapeDtypeStruct((B, D, nf), F32),
        compiler_params=_cparams(("parallel",)),
        name="block_means",
    )(kT)


def _moba_prompt_body(rbf_ref, bt_ref, qbT_ref, k_ref, vT_ref, mbdT_ref, o_ref,
                      ch_ref, m_ref, l_ref, acc_ref, *, T, n_sel, n_slots):
    i = pl.program_id(1)
    H = N_HEADS
    W = n_slots * H
    kpos = lax.broadcasted_iota(I32, (T, T), 0)
    qpos = lax.broadcasted_iota(I32, (T, T), 1)
    qT = qbT_ref[...]

    g = _dot3(mbdT_ref[...], qT)
    blk = jnp.right_shift(lax.broadcasted_iota(I32, (W, T), 0), int(math.log2(H)))
    g = jnp.where(blk < i, g, -jnp.inf)
    rank = jnp.zeros((W, T), F32)
    for r in range(1, n_slots):
        other = pltpu.roll(g, r * H, axis=0)
        rank = rank + jnp.where(blk >= r, jnp.where(other >= g, 1.0, 0.0), jnp.where(other > g, 1.0, 0.0))
    ch_ref[...] = jnp.where((rank < n_sel) & (blk < i), 0.0, NEG_BIG)

    _attn_init(m_ref, l_ref, acc_ref)
    q = _padded_heads(lambda h: (qT[h * HEAD_DIM:(h + 1) * HEAD_DIM, :] * ATTN_SCALE).astype(BF16))
    causal_pen = jnp.where(kpos <= qpos, 0.0, NEG_BIG)

    def attn_chunk(kc, near):
        kk = k_ref[pl.ds(pl.multiple_of(kc * T, T), T), :]

        def pen_of(h):
            pen = ch_ref[pl.ds(kc * H + h, 1), :]
            return jnp.where(kc == i, causal_pen, pen) if near else pen

        _attn_update(
            lambda h: _dot(_head_pair(kk, h), q[h]) + (bt_ref[i - kc, h] if near else rbf_ref[h]),
            pen_of,
            lambda h: vT_ref[h * HEAD_DIM:(h + 1) * HEAD_DIM, pl.ds(pl.multiple_of(kc * T, T), T)],
            m_ref, l_ref, acc_ref)

    n_far = jnp.maximum(i - 1, 0)
    lax.fori_loop(0, n_far, lambda kc, c: (attn_chunk(kc, False), c)[1], 0)
    lax.fori_loop(n_far, i + 1, lambda kc, c: (attn_chunk(kc, True), c)[1], 0)
    _attn_finish(o_ref, l_ref, acc_ref)


def _moba_prompt(rb_far, btiles, qbT, k16, vT, mbdT):
    B, _, S = qbT.shape
    M = B * S
    H = N_HEADS
    T = MOBA_BLOCK
    nc = S // T
    W = mbdT.shape[1]
    body = functools.partial(_moba_prompt_body, T=T, n_sel=min(MOBA_TOPK, nc), n_slots=W // H)
    return pl.pallas_call(
        body,
        grid=(B, nc),
        in_specs=[pl.BlockSpec(memory_space=pltpu.SMEM),
                  pl.BlockSpec((2, H, T, T), lambda b, i: (0, 0, 0, 0)),
                  pl.BlockSpec((None, D_ATT, T), lambda b, i: (b, 0, i)),
                  pl.BlockSpec((S, D_ATT), lambda b, i: (b, 0)),
                  pl.BlockSpec((None, D_ATT, S), lambda b, i: (b, 0, 0)),
                  pl.BlockSpec((None, W, D_ATT), lambda b, i: (b, 0, 0))],
        out_specs=pl.BlockSpec((T, D_ATT), lambda b, i: (b * nc + i, 0)),
        out_shape=jax.ShapeDtypeStruct((M, D_ATT), F32),
        scratch_shapes=[pltpu.VMEM((W, T), F32)] + _attn_scratch(T),
        compiler_params=_cparams(("parallel", "arbitrary")),
        name="moba_prompt",
    )(rb_far, btiles, qbT, k16, vT, mbdT)


def _merge_body(x_ref, oa_ref, ob_ref, g_ref, p_ref, wba_ref, wbb_ref, wo_ref, g1_ref, b1_ref,
                wr_ref, br_ref, wpg_ref, wpp_ref, x1b_ref, comb_ref, res_ref):
    bra = _dot(oa_ref[...].astype(BF16), wba_ref[...])
    brb = _dot(ob_ref[...].astype(BF16), wbb_ref[...])
    gates = g_ref[...]
    mix = gates[:, :D_MODEL] * bra + gates[:, D_MODEL:] * brb
    y = _dot(mix.astype(BF16), wo_ref[...])
    x1 = _layer_norm(ALPHA_DN * x_ref[...] + y, g1_ref[...], b1_ref[...])
    x1b = x1.astype(BF16)
    x1b_ref[...] = x1b

    logits = _dot3(x1, wr_ref[...]) + br_ref[...]
    lane = lax.broadcasted_iota(I32, logits.shape, 1)
    work = logits
    kept = jnp.zeros(logits.shape, jnp.bool_)
    for _ in range(TOP_K):
        mx = jnp.max(work, axis=-1, keepdims=True)
        first = jnp.min(jnp.where(work == mx, lane, N_EXPERTS), axis=-1, keepdims=True)
        hit = lane == first
        kept = kept | hit
        work = jnp.where(hit, -jnp.inf, work)
    top = jnp.max(logits, axis=-1, keepdims=True)
    e = jnp.where(kept, jnp.exp(logits - top), 0.0)
    comb_ref[...] = e / jnp.sum(e, axis=-1, keepdims=True)

    ple = jax.nn.sigmoid(_dot(x1b, wpg_ref[...])) * _dot(p_ref[...].astype(BF16), wpp_ref[...])
    res_ref[...] = ALPHA_DN * x1 + ple


def _merge(x, oa, ob, gates, p, wba, wbb, wo, g1, b1, wr, br, wpg, wpp):
    M = x.shape[0]
    tm = min(M, 512)
    full = lambda a: pl.BlockSpec(a.shape, lambda i: (0,) * a.ndim)
    row = lambda n: pl.BlockSpec((tm, n), lambda i: (i, 0))
    return pl.pallas_call(
        _merge_body,
        grid=(M // tm,),
        in_specs=[row(D_MODEL), row(D_ATT), row(D_ATT), row(2 * D_MODEL), row(PLE_DIM),
                  full(wba), full(wbb), full(wo), full(g1), full(b1), full(wr), full(br), full(wpg), full(wpp)],
        out_specs=[row(D_MODEL), row(N_EXPERTS), row(D_MODEL)],
        out_shape=[jax.ShapeDtypeStruct((M, D_MODEL), BF16),
                   jax.ShapeDtypeStruct((M, N_EXPERTS), F32),
                   jax.ShapeDtypeStruct((M, D_MODEL), F32)],
        compiler_params=_cparams(("parallel",)),
        name="merge",
    )(x, oa, ob, gates, p, wba, wbb, wo, g1, b1, wr, br, wpg, wpp)


MOE_TM = 1024
MOE_RB = 192
MOE_CB = 256


def _moe_body(xb_ref, res_ref, comb_ref, wgu_ref, bgu_ref, wdn_ref, bdn_ref, g2_ref, b2_ref,
              o_ref, rank_ref, rankT_ref, acc_ref):
    e = pl.program_id(1)
    TM, E = comb_ref.shape
    RB = MOE_RB

    @pl.when(e == 0)
    def _():
        routed = jnp.where(comb_ref[...] != 0.0, 1.0, 0.0)
        eye = jnp.where(lax.broadcasted_iota(I32, (E, E), 0) == lax.broadcasted_iota(I32, (E, E), 1),
                        1.0, 0.0).astype(BF16)
        routedT = _dot(eye, routed.astype(BF16), _NT)
        CB = min(MOE_CB, TM)
        r_i = lax.broadcasted_iota(I32, (CB, CB), 0)
        c_i = lax.broadcasted_iota(I32, (CB, CB), 1)
        before = jnp.where(c_i < r_i, 1.0, 0.0).astype(BF16)
        beforeT = jnp.where(r_i < c_i, 1.0, 0.0).astype(BF16)
        off = jnp.zeros((1, E), F32)
        offT = jnp.zeros((E, 1), F32)
        for blk in range(TM // CB):
            rb = routed[blk * CB:(blk + 1) * CB]
            rbT = routedT[:, blk * CB:(blk + 1) * CB]
            rank = _dot(before, rb.astype(BF16)) + off
            rankT = _dot(rbT.astype(BF16), beforeT) + offT
            rank_ref[blk * CB:(blk + 1) * CB, :] = jnp.where(rb > 0.5, rank, -1.0).astype(I32)
            rankT_ref[:, blk * CB:(blk + 1) * CB] = jnp.where(rbT > 0.5, rankT, -1.0).astype(I32)
            off = off + jnp.sum(rb, axis=0, keepdims=True)
            offT = offT + jnp.sum(rbT, axis=1, keepdims=True)
        acc_ref[...] = jnp.zeros(acc_ref.shape, F32)

    lane = lax.broadcasted_iota(I32, (TM, E), 1)
    gate_col = jnp.sum(jnp.where(lane == e, comb_ref[...], 0.0), axis=-1, keepdims=True)
    rank_col = jnp.sum(jnp.where(lane == e, rank_ref[...], 0), axis=-1, keepdims=True)
    rank_row = rankT_ref[pl.ds(e, 1), :]
    n_routed = jnp.sum(jnp.where(rank_row >= 0, 1, 0))
    n_pass = (n_routed + (RB - 1)) // RB
    slot_rows = lax.broadcasted_iota(I32, (RB, TM), 0)
    slot_lanes = lax.broadcasted_iota(I32, (TM, RB), 1)

    def one_pass(pi, carry):
        base = pi * RB
        pick = jnp.where(rank_row == slot_rows + base, 1.0, 0.0).astype(BF16)
        xg = _dot(pick, xb_ref[...]).astype(BF16)
        gu = _dot(xg, wgu_ref[...]) + bgu_ref[...]
        gt = jnp.minimum(gu[:, :D_EXPERT], SWIGLU_LIMIT)
        up = jnp.clip(gu[:, D_EXPERT:], -SWIGLU_LIMIT, SWIGLU_LIMIT)
        hid = (up + 1.0) * gt * jax.nn.sigmoid(SWIGLU_ALPHA * gt)
        down = _dot(hid.astype(BF16), wdn_ref[...])
        place = jnp.where(rank_col == slot_lanes + base, 1.0, 0.0).astype(BF16)
        acc_ref[...] += _dot(place, down.astype(BF16)) * gate_col
        return carry

    lax.fori_loop(0, n_pass, one_pass, 0)

    @pl.when(e == pl.num_programs(1) - 1)
    def _():
        y = acc_ref[...] + _dot3(comb_ref[...], bdn_ref[...])
        o_ref[...] = _layer_norm(res_ref[...] + y, g2_ref[...], b2_ref[...])


def _moe(xb, res, comb, w_gu, b_gu, w_dn, b_dn, g2, b2):
    M = xb.shape[0]
    tm = min(M, MOE_TM)
    E = w_gu.shape[0]
    b_gu3 = b_gu.reshape(E, 1, 2 * D_EXPERT)
    row = lambda n: pl.BlockSpec((tm, n), lambda i, e: (i, 0))
    return pl.pallas_call(
        _moe_body,
        grid=(M // tm, E),
        in_specs=[row(D_MODEL), row(D_MODEL), row(E),
                  pl.BlockSpec((None, D_MODEL, 2 * D_EXPERT), lambda i, e: (e, 0, 0)),
                  pl.BlockSpec((None, 1, 2 * D_EXPERT), lambda i, e: (e, 0, 0)),
                  pl.BlockSpec((None, D_EXPERT, D_MODEL), lambda i, e: (e, 0, 0)),
                  pl.BlockSpec((E, D_MODEL), lambda i, e: (0, 0)),
                  pl.BlockSpec((1, D_MODEL), lambda i, e: (0, 0)),
                  pl.BlockSpec((1, D_MODEL), lambda i, e: (0, 0))],
        out_specs=row(D_MODEL),
        out_shape=jax.ShapeDtypeStruct((M, D_MODEL), F32),
        scratch_shapes=[pltpu.VMEM((tm, E), I32), pltpu.VMEM((E, tm), I32), pltpu.VMEM((tm, D_MODEL), F32)],
        compiler_params=_cparams(("parallel", "arbitrary")),
        name="moe",
    )(xb, res, comb, w_gu, b_gu3, w_dn, b_dn, g2, b2)


SEL_PAGES = 16
ATT_PAGES = 8


def _page_specs(n, rows, width, li):
    def spec(u):
        return pl.BlockSpec((None, None, rows, width),
                            lambda b, c, pt: (li, pt[b, c * n + u], 0, 0))
    return [spec(u) for u in range(n)]


def _smp_dsa_select_body(pt_ref, qi_ref, wi_ref, knew_ref, *rest, n_pages, n_top, idx_bits):
    pages = rest[:SEL_PAGES]
    sel_ref = rest[SEL_PAGES]
    key_ref = rest[SEL_PAGES + 1]
    c = pl.program_id(1)
    T = qi_ref.shape[0] // IDX_HEADS
    qi = qi_ref[...]
    wi = wi_ref[...] * IDX_SCALE

    def page_score(kpT):
        s = jnp.maximum(_dot3(qi, kpT), 0.0) * wi
        return jnp.sum(s.reshape(IDX_HEADS, T, PAGE_SIZE), axis=0)

    for u in range(SEL_PAGES):
        key_ref[c * SEL_PAGES + u] = _order_key(page_score(pages[u][...]))

    @pl.when(c == pl.num_programs(1) - 1)
    def _():
        qrow = lax.broadcasted_iota(I32, (T, PAGE_SIZE), 0)
        lane = lax.broadcasted_iota(I32, (T, PAGE_SIZE), 1)
        key_ref[n_pages] = jnp.where(lane <= qrow, _order_key(page_score(knew_ref[...])), INT_MIN)
        keys = key_ref[...]
        shape = keys.shape
        gidx = lax.broadcasted_iota(I32, shape, 0) * PAGE_SIZE + lax.broadcasted_iota(I32, shape, 2)
        valid = (lax.broadcasted_iota(I32, shape, 0) < n_pages) | \
                (lax.broadcasted_iota(I32, shape, 2) <= lax.broadcasted_iota(I32, shape, 1))

        def count(hit):
            per_lane = jnp.sum(hit.astype(I32), axis=0)
            return jnp.sum(per_lane, axis=-1, keepdims=True)[None]

        def thr_bit(bi, prefix):
            cand = prefix | jnp.left_shift(jnp.int32(1), 31 - bi)
            cs = cand ^ INT_MIN
            return jnp.where(count(keys >= cs) >= n_top, cand, prefix)

        thr = lax.fori_loop(0, 32, thr_bit, jnp.zeros((1, T, 1), I32)) ^ INT_MIN
        need = n_top - count(keys > thr)

        def tie_bit(bi, j0):
            cand = j0 | jnp.left_shift(jnp.int32(1), idx_bits - 1 - bi)
            return jnp.where(count((keys == thr) & (gidx < cand)) < need, cand, j0)

        jcut = lax.fori_loop(0, idx_bits, tie_bit, jnp.zeros((1, T, 1), I32))
        sel = ((keys > thr) | ((keys == thr) & (gidx <= jcut))) & valid
        sel_ref[...] = sel.astype(F32)


def _smp_dsa_select(page_table, qi_rows, wi_rows, knew, cache_kidx, li):
    DB, P = page_table.shape
    T = qi_rows.shape[1] // IDX_HEADS
    L = P * PAGE_SIZE + T
    body = functools.partial(_smp_dsa_select_body, n_pages=P, n_top=min(IDX_TOPK, L // 4),
                             idx_bits=max(1, ((P + 1) * PAGE_SIZE - 1).bit_length()))
    grid_spec = pltpu.PrefetchScalarGridSpec(
        num_scalar_prefetch=1,
        grid=(DB, P // SEL_PAGES),
        in_specs=[pl.BlockSpec((None, IDX_HEADS * T, IDX_DIM), lambda b, c, pt: (b, 0, 0)),
                  pl.BlockSpec((None, IDX_HEADS * T, 1), lambda b, c, pt: (b, 0, 0)),
                  pl.BlockSpec((None, IDX_DIM, PAGE_SIZE), lambda b, c, pt: (b, 0, 0))]
                 + _page_specs(SEL_PAGES, IDX_DIM, PAGE_SIZE, li),
        out_specs=pl.BlockSpec((None, P + 1, T, PAGE_SIZE), lambda b, c, pt: (b, 0, 0, 0)),
        scratch_shapes=[pltpu.VMEM((P + 1, T, PAGE_SIZE), I32)],
    )
    return pl.pallas_call(
        body,
        grid_spec=grid_spec,
        out_shape=jax.ShapeDtypeStruct((DB, P + 1, T, PAGE_SIZE), F32),
        compiler_params=_cparams(("parallel", "arbitrary")),
        name="smp_dsa_select",
    )(page_table, qi_rows, wi_rows, knew, *([cache_kidx] * SEL_PAGES))


def _smp_moba_select_body(pt_ref, qbd_ref, *rest, n_blocks, n_sel):
    pages = rest[:SEL_PAGES]
    sel_ref = rest[SEL_PAGES]
    mean_ref = rest[SEL_PAGES + 1]
    c = pl.program_id(1)
    per_step = SEL_PAGES * PAGE_SIZE // MOBA_BLOCK
    per_block = MOBA_BLOCK // PAGE_SIZE
    W = mean_ref.shape[1]
    lane = lax.broadcasted_iota(I32, mean_ref.shape, 1)

    @pl.when(c == 0)
    def _():
        mean_ref[...] = jnp.zeros(mean_ref.shape, F32)

    means = mean_ref[...]
    for j in range(per_step):
        tot = pages[j * per_block][...]
        for u in range(1, per_block):
            tot = tot + pages[j * per_block + u][...]
        col = jnp.sum(tot, axis=1, keepdims=True) * (1.0 / MOBA_BLOCK)
        means = jnp.where(lane == c * per_step + j, col, means)
    mean_ref[...] = means

    @pl.when(c == pl.num_programs(1) - 1)
    def _():
        g = _dot3(qbd_ref[...], means)
        blk = lax.broadcasted_iota(I32, g.shape, 1)
        g = jnp.where(blk < n_blocks, g, -jnp.inf)
        rank = jnp.zeros(g.shape, F32)
        for r in range(1, n_blocks):
            lower = pltpu.roll(g, r, axis=1)
            upper = pltpu.roll(g, W - r, axis=1)
            rank = rank + jnp.where(lower >= g, 1.0, 0.0) + jnp.where(upper > g, 1.0, 0.0)
        sel_ref[...] = jnp.where((rank < n_sel) & (blk < n_blocks), 1.0, 0.0)


def _smp_moba_select(page_table, qbd, cache_kT, li):
    DB, P = page_table.shape
    n_blocks = P * PAGE_SIZE // MOBA_BLOCK
    n_steps = P // SEL_PAGES
    W = 128
    assert n_blocks < W
    R = qbd.shape[1]
    body = functools.partial(_smp_moba_select_body, n_blocks=n_blocks, n_sel=min(MOBA_TOPK, n_blocks))
    grid_spec = pltpu.PrefetchScalarGridSpec(
        num_scalar_prefetch=1,
        grid=(DB, n_steps),
        in_specs=[pl.BlockSpec((None, R, D_ATT), lambda b, c, pt: (b, 0, 0))]
                 + _page_specs(SEL_PAGES, D_ATT, PAGE_SIZE, li),
        out_specs=pl.BlockSpec((None, R, W), lambda b, c, pt: (b, 0, 0)),
        scratch_shapes=[pltpu.VMEM((D_ATT, W), F32)],
    )
    return pl.pallas_call(
        body,
        grid_spec=grid_spec,
        out_shape=jax.ShapeDtypeStruct((DB, R, W), F32),
        compiler_params=_cparams(("parallel", "arbitrary")),
        name="smp_moba_select",
    )(page_table, qbd, *([cache_kT] * SEL_PAGES))


def _smp_attn_body(pt_ref, qbd_ref, rbr_ref, knew_ref, vnew_ref, sel_ref, selnew_ref, *rest, mode, n_pages):
    kp = rest[:ATT_PAGES]
    vp = rest[ATT_PAGES:2 * ATT_PAGES]
    o_ref, m_ref, l_ref, acc_ref = rest[2 * ATT_PAGES:]
    c = pl.program_id(1)
    R = qbd_ref.shape[0]
    T = R // N_HEADS
    q = (qbd_ref[...] * ATTN_SCALE).astype(BF16)
    rbr = rbr_ref[...]
    far_bias = rbr[:, N_BUCKETS - 1:]
    rowq = lax.broadcasted_iota(I32, (R, PAGE_SIZE), 0) & (T - 1)
    lane = lax.broadcasted_iota(I32, (R, PAGE_SIZE), 1)

    @pl.when(c == 0)
    def _():
        m_ref[...] = jnp.full(m_ref.shape, NEG_BIG, F32)
        l_ref[...] = jnp.zeros(l_ref.shape, F32)
        acc_ref[...] = jnp.zeros(acc_ref.shape, F32)

    def bias_for(dist):
        return _bias_chain(dist, lambda b: rbr[:, b:b + 1])

    def update(s, vT):
        m_old = m_ref[...]
        m_new = jnp.maximum(m_old, jnp.max(s, axis=-1, keepdims=True))
        p = jnp.exp(s - m_new)
        alpha = jnp.exp(m_old - m_new)
        l_ref[...] = alpha * l_ref[...] + jnp.sum(p, axis=-1, keepdims=True)
        acc_ref[...] = alpha * acc_ref[...] + _dot(p.astype(BF16), vT, _NT)
        m_ref[...] = m_new

    logits, vals = [], []
    for u in range(ATT_PAGES):
        page = c * ATT_PAGES + u
        s = _dot(q, kp[u][...].astype(BF16))
        if u == ATT_PAGES - 1:
            dist = (n_pages - page) * PAGE_SIZE + rowq - lane
            bias = lax.cond(c == pl.num_programs(1) - 1, lambda: bias_for(dist),
                            lambda: jnp.broadcast_to(far_bias, (R, PAGE_SIZE)))
        else:
            bias = far_bias
        if mode == "dsa":
            pen = jnp.where(jnp.tile(sel_ref[u], (N_HEADS, 1)) > 0.5, 0.0, NEG_BIG)
        else:
            blk = page // (MOBA_BLOCK // PAGE_SIZE)
            sel = sel_ref[...]
            pick = jnp.sum(jnp.where(lax.broadcasted_iota(I32, sel.shape, 1) == blk, sel, 0.0),
                           axis=-1, keepdims=True)
            pen = jnp.where(pick > 0.5, 0.0, NEG_BIG)
        logits.append(s + (bias + pen))
        vals.append(vp[u][...].astype(BF16))
    update(jnp.concatenate(logits, axis=1), jnp.concatenate(vals, axis=1))

    @pl.when(c == pl.num_programs(1) - 1)
    def _():
        s = _dot(q, knew_ref[...].astype(BF16)) + bias_for(rowq - lane)
        keep = lane <= rowq
        if mode == "dsa":
            keep = keep & (jnp.tile(selnew_ref[...], (N_HEADS, 1)) > 0.5)
        update(s + jnp.where(keep, 0.0, NEG_BIG), vnew_ref[...].astype(BF16))
        out = acc_ref[...] / l_ref[...]
        for h in range(N_HEADS):
            o_ref[:, h * HEAD_DIM:(h + 1) * HEAD_DIM] = out[h * T:(h + 1) * T, h * HEAD_DIM:(h + 1) * HEAD_DIM]


def _smp_attn(page_table, qbd, rbr, knewT, vnewT, sel, cache_kT, cache_vT, li, *, mode):
    DB, P = page_table.shape
    R = qbd.shape[1]
    T = R // N_HEADS
    assert T & (T - 1) == 0
    if mode == "dsa":
        sel_specs = [pl.BlockSpec((None, ATT_PAGES, T, PAGE_SIZE), lambda b, c, pt: (b, c, 0, 0)),
                     pl.BlockSpec((None, None, T, PAGE_SIZE), lambda b, c, pt: (b, P, 0, 0))]
    else:
        sel_specs = [pl.BlockSpec((None, R, sel.shape[-1]), lambda b, c, pt: (b, 0, 0)),
                     pl.BlockSpec((None, R, sel.shape[-1]), lambda b, c, pt: (b, 0, 0))]
    new_spec = pl.BlockSpec((None, D_ATT, PAGE_SIZE), lambda b, c, pt: (b, 0, 0))
    grid_spec = pltpu.PrefetchScalarGridSpec(
        num_scalar_prefetch=1,
        grid=(DB, P // ATT_PAGES),
        in_specs=[pl.BlockSpec((None, R, D_ATT), lambda b, c, pt: (b, 0, 0)),
                  pl.BlockSpec(rbr.shape, lambda b, c, pt: (0, 0)),
                  new_spec, new_spec]
                 + sel_specs + _page_specs(ATT_PAGES, D_ATT, PAGE_SIZE, li) + _page_specs(ATT_PAGES, D_ATT, PAGE_SIZE, li),
        out_specs=pl.BlockSpec((None, T, D_ATT), lambda b, c, pt: (b, 0, 0)),
        scratch_shapes=[pltpu.VMEM((R, 1), F32), pltpu.VMEM((R, 1), F32), pltpu.VMEM((R, D_ATT), F32)],
    )
    return pl.pallas_call(
        functools.partial(_smp_attn_body, mode=mode, n_pages=P),
        grid_spec=grid_spec,
        out_shape=jax.ShapeDtypeStruct((DB, T, D_ATT), F32),
        compiler_params=_cparams(("parallel", "arbitrary")),
        name="smp_attn_" + mode,
    )(page_table, qbd, rbr, knewT, vnewT, sel, sel, *([cache_kT] * ATT_PAGES), *([cache_vT] * ATT_PAGES))


def _project(x2, wT, g, b):
    o = IN_OFFS
    qkv_a = _mm(x2, wT[o[0]:o[3]])
    qi, ki, wi = _proj_idx(x2, wT[o[3]:o[4]], wT[o[4]:o[5]], wT[o[5]:o[6]], g[None], b[None])
    qk_b = _mm(x2, wT[o[6]:o[8]], passes=3)
    v_b = _mm(x2, wT[o[8]:o[9]])
    gates = _mm(x2, wT[o[9]:o[10]], act="sigmoid")
    return dict(q_a=qkv_a[:, :D_ATT], k_a=qkv_a[:, D_ATT:2 * D_ATT], v_a=qkv_a[:, 2 * D_ATT:],
                q_i=qi, k_i=ki, w_i=wi, q_b=qk_b[:, :D_ATT], k_b=qk_b[:, D_ATT:], v_b=v_b, gates=gates)


def _project_prompt(x2, wT, g, b, B):
    o = IN_OFFS
    k_aT, k_a16 = _mmT(wT[o[1]:o[2]], x2, B, natural_dtypes=(BF16,))
    v_aT, v_aT16 = _mmT(wT[o[2]:o[3]], x2, B, out_dtypes=(F32, BF16))
    k_iT, kcat = _proj_kidx(x2, wT[o[4]:o[5]], g, b, B)
    k_bT, k_b16 = _mmT(wT[o[7]:o[8]], x2, B, passes=3, natural_dtypes=(BF16,))
    v_bT, v_bT16 = _mmT(wT[o[8]:o[9]], x2, B, out_dtypes=(F32, BF16))
    return dict(
        q_aT=_mmT(wT[o[0]:o[1]], x2, B, scales=(ATTN_SCALE,), out_dtypes=(BF16,))[0],
        k_aT=k_aT, k_a16=k_a16, v_aT=v_aT, v_aT16=v_aT16,
        q_iT=_mmT(wT[o[3]:o[4]], x2, B, passes=3)[0],
        k_iT=k_iT, kcat=kcat,
        w_iT=_mmT(wT[o[5]:o[6]], x2, B, passes=3, scales=(IDX_HEADS ** -0.5, IDX_SCALE))[0],
        q_bT=_mmT(wT[o[6]:o[7]], x2, B, passes=3)[0],
        k_bT=k_bT, k_b16=k_b16, v_bT=v_bT, v_bT16=v_bT16,
        gates=_mm(x2, wT[o[9]:o[10]], act="sigmoid"))


def _heads_major(a, B, S):
    return a.reshape(B, S, N_HEADS, HEAD_DIM).transpose(0, 2, 1, 3)


def _block_diag_rows(q):
    DB, T, H, Dh = q.shape
    eye = jnp.eye(H, dtype=q.dtype)
    return jnp.einsum("bthd,hg->bhtgd", q, eye).reshape(DB, H * T, H * Dh)


def _prompt_mixers(pr, B, S, rb_a, rb_b, T_dsa):
    bt_a = _bias_tiles(rb_a, T_dsa)
    o_a = _dsa_prompt(rb_a[N_BUCKETS - 1], bt_a, pr["q_aT"], pr["k_a16"], pr["v_aT16"],
                      pr["q_iT"], pr["w_iT"], pr["kcat"], T=T_dsa)
    T = MOBA_BLOCK
    nf = S // T
    n_slots = 128 // N_HEADS
    assert nf <= n_slots
    means = _block_means(pr["k_bT"]).reshape(B, N_HEADS, HEAD_DIM, nf)
    eye = jnp.eye(N_HEADS, dtype=F32)
    mbdT = jnp.einsum("bhdj,hg->bjghd", means, eye)
    mbdT = jnp.pad(mbdT, ((0, 0), (0, n_slots - nf), (0, 0), (0, 0), (0, 0))).reshape(B, n_slots * N_HEADS, D_ATT)
    bt_b = _bias_tiles(rb_b, T)
    o_b = _moba_prompt(rb_b[N_BUCKETS - 1], bt_b, pr["q_bT"], pr["k_b16"], pr["v_bT16"], mbdT)
    return o_a, o_b


def _sample_mixers(sm, DB, T, caches, page_table, rb_a, rb_b, li):
    ck_a, cv_a, ck_i, ck_b, cv_b = caches
    new_page = lambda a: jnp.pad(a.reshape(DB, T, -1), ((0, 0), (0, PAGE_SIZE - T), (0, 0))).transpose(0, 2, 1)
    qi_rows = _heads_major(sm["q_i"], DB, T).reshape(DB, IDX_HEADS * T, IDX_DIM)
    wi_rows = sm["w_i"].reshape(DB, T, IDX_HEADS).transpose(0, 2, 1).reshape(DB, IDX_HEADS * T, 1)
    sel_a = _smp_dsa_select(page_table, qi_rows, wi_rows, new_page(sm["k_i"]), ck_i, li)
    rbr_a = jnp.repeat(rb_a.T, T, axis=0)
    rbr_b = jnp.repeat(rb_b.T, T, axis=0)
    qbd_a = _block_diag_rows(sm["q_a"].reshape(DB, T, N_HEADS, HEAD_DIM))
    o_a = _smp_attn(page_table, qbd_a, rbr_a, new_page(sm["k_a"]), new_page(sm["v_a"]), sel_a,
                    ck_a, cv_a, li, mode="dsa")
    qbd_b = _block_diag_rows(sm["q_b"].reshape(DB, T, N_HEADS, HEAD_DIM))
    sel_b = _smp_moba_select(page_table, qbd_b, ck_b, li)
    o_b = _smp_attn(page_table, qbd_b, rbr_b, new_page(sm["k_b"]), new_page(sm["v_b"]), sel_b,
                    ck_b, cv_b, li, mode="moba")
    return o_a.reshape(DB * T, D_ATT), o_b.reshape(DB * T, D_ATT)


def kernel(x_prompt, x_sample, cache_k_a, cache_v_a, cache_kidx, cache_k_b, cache_v_b, page_table, p_prompt, p_sample, rel_bias, w_in, kidx_ln_g, kidx_ln_b, w_branch_a, w_branch_b, w_out, ln1_g, ln1_b, w_router, b_router, w_gate_up, b_gate_up, w_down, b_down, w_ple_gate, w_ple_proj, ln2_g, ln2_b):
    B, S, D = x_prompt.shape
    DB, T, _ = x_sample.shape
    depth = w_in.shape[0]
    n_pool = cache_k_a.shape[1]
    rb_a = rel_bias[:, :N_HEADS]
    rb_b = rel_bias[:, N_HEADS:]
    kv_pages = lambda c: jnp.transpose(c, (0, 1, 3, 4, 2)).reshape(depth, n_pool, D_ATT, PAGE_SIZE)
    caches = (kv_pages(cache_k_a), kv_pages(cache_v_a), jnp.transpose(cache_kidx, (0, 1, 3, 2)),
              kv_pages(cache_k_b), kv_pages(cache_v_b))
    T_dsa = min(256, S)

    xp = x_prompt.reshape(B * S, D)
    xs = x_sample.reshape(DB * T, D)
    rows_p, rows_s = [], []
    for li in range(depth):
        bf = lambda a: a[li].astype(BF16)
        merge_w = (bf(w_branch_a), bf(w_branch_b), bf(w_out), ln1_g[li][None], ln1_b[li][None],
                   w_router[li], b_router[li][None], bf(w_ple_gate), bf(w_ple_proj))
        moe_w = (bf(w_gate_up), b_gate_up[li], bf(w_down), b_down[li], ln2_g[li][None], ln2_b[li][None])

        w_inT = jnp.swapaxes(w_in[li], 0, 1)
        pr = _project_prompt(xp, w_inT, kidx_ln_g[li], kidx_ln_b[li], B)
        o_a, o_b = _prompt_mixers(pr, B, S, rb_a, rb_b, T_dsa)
        x1b, comb, res = _merge(xp, o_a, o_b, pr["gates"], p_prompt[li].reshape(B * S, -1), *merge_w)
        xp = _moe(x1b, res, comb, *moe_w)
        rows_p.append(pr)

        sm = _project(xs, w_inT, kidx_ln_g[li], kidx_ln_b[li])
        o_a, o_b = _sample_mixers(sm, DB, T, caches, page_table, rb_a, rb_b, li)
        x1b, comb, res = _merge(xs, o_a, o_b, sm["gates"], p_sample[li].reshape(DB * T, -1), *merge_w)
        xs = _moe(x1b, res, comb, *moe_w)
        rows_s.append(sm)

    def stack(rows, name, lead, tail):
        return jnp.stack([r[name].reshape(lead + tail) for r in rows])

    def stack_T(name, feat):
        a = jnp.stack([r[name] for r in rows_p])
        return jnp.moveaxis(a.reshape((depth, B) + feat + (S,)), -1, 2)

    hd = (N_HEADS, HEAD_DIM)
    outs = [xp.reshape(B, S, D), xs.reshape(DB, T, D)]
    outs += [stack_T("k_aT", hd), stack_T("v_aT", hd), stack_T("k_iT", (IDX_DIM,)),
             stack_T("k_bT", hd), stack_T("v_bT", hd)]
    lead = (DB, T)
    outs += [stack(rows_s, "k_a", lead, hd), stack(rows_s, "v_a", lead, hd), stack(rows_s, "k_i", lead, (IDX_DIM,)),
             stack(rows_s, "k_b", lead, hd), stack(rows_s, "v_b", lead, hd)]
    return tuple(outs)
```

```python
import functools
import math

import numpy as np
import jax
import jax.numpy as jnp
from jax import lax
from jax.experimental import pallas as pl
from jax.experimental.pallas import tpu as pltpu

F32 = jnp.float32
BF16 = jnp.bfloat16
I32 = jnp.int32

D_MODEL = 1024
HEAD_DIM = 64
N_HEADS = 8
D_ATT = N_HEADS * HEAD_DIM
IDX_HEADS = 8
IDX_DIM = 64
IDX_TOPK = 256
MOBA_BLOCK = 256
MOBA_TOPK = 3
N_BUCKETS = 32
MAX_DISTANCE = 128
N_EXPERTS = 32
TOP_K = 4
D_EXPERT = D_MODEL
SWIGLU_LIMIT = 7.0
SWIGLU_ALPHA = 1.702
PLE_DIM = 256
LN_EPS = 1e-5
PAGE_SIZE = 128
DEPTH = 2
ALPHA_DN = (2 * DEPTH) ** 0.25
ATTN_SCALE = HEAD_DIM ** -0.5
IDX_SCALE = IDX_DIM ** -0.5
IN_SIZES = (D_ATT, D_ATT, D_ATT, IDX_HEADS * IDX_DIM, IDX_DIM, IDX_HEADS, D_ATT, D_ATT, D_ATT, 2 * D_MODEL)
IN_OFFS = tuple(int(v) for v in np.cumsum((0,) + IN_SIZES))

INT_MIN = -(2 ** 31)
NEG_BIG = -1e30
VMEM_LIMIT_BYTES = 56 * 1024 * 1024


def _bucket_of(d):
    d = max(d, 0)
    max_exact = N_BUCKETS // 2
    if d < max_exact:
        return d
    ratio = math.log(d / max_exact) / math.log(MAX_DISTANCE / max_exact)
    return min(max_exact + int(ratio * (N_BUCKETS - max_exact)), N_BUCKETS - 1)


_BUCKET_HI = tuple(max(d for d in range(4 * MAX_DISTANCE) if _bucket_of(d) == b) for b in range(N_BUCKETS - 1))
FAR_DIST = _BUCKET_HI[-1] + 1


def _cparams(sem):
    return pltpu.CompilerParams(dimension_semantics=sem, vmem_limit_bytes=VMEM_LIMIT_BYTES)


def _split(x):
    hi = x.astype(BF16)
    lo = (x - hi.astype(F32)).astype(BF16)
    return hi, lo


_NN = (((1,), (0,)), ((), ()))
_NT = (((1,), (1,)), ((), ()))


def _dot(a, b, dims=_NN):
    return lax.dot_general(a, b, dims, preferred_element_type=F32)


def _dot3(a, b, dims=_NN):
    ah, al = _split(a)
    bh, bl = _split(b)
    return _dot(al, bh, dims) + _dot(ah, bl, dims) + _dot(ah, bh, dims)


def _layer_norm(x, g, b):
    mu = jnp.mean(x, axis=-1, keepdims=True)
    xc = x - mu
    var = jnp.mean(xc * xc, axis=-1, keepdims=True)
    return xc * lax.rsqrt(var + LN_EPS) * g + b


def _order_key(s):
    s = jnp.where(s == 0.0, 0.0, s)
    u = pltpu.bitcast(s, I32)
    return u ^ (jnp.right_shift(u, 31) & 0x7FFFFFFF)


def _bias_chain(d, rb_of):
    val = rb_of(N_BUCKETS - 1)
    for b in range(N_BUCKETS - 2, -1, -1):
        val = jnp.where(d <= _BUCKET_HI[b], rb_of(b), val)
    return val


def _mm_body(x_ref, wT_ref, o_ref, *, passes, act):
    x = x_ref[...]
    wT = wT_ref[...]
    y = _dot(x.astype(BF16), wT.astype(BF16), _NT) if passes == 1 else _dot3(x, wT, _NT)
    if act == "sigmoid":
        y = jax.nn.sigmoid(y)
    o_ref[...] = y.astype(o_ref.dtype)


def _mm(x, wT, *, passes=1, act=None, out_dtype=F32):
    M, K = x.shape
    N = wT.shape[0]
    tm = min(M, 512)
    tn = min(N, 512)
    return pl.pallas_call(
        functools.partial(_mm_body, passes=passes, act=act),
        grid=(M // tm, N // tn),
        in_specs=[pl.BlockSpec((tm, K), lambda i, j: (i, 0)),
                  pl.BlockSpec((tn, K), lambda i, j: (j, 0))],
        out_specs=pl.BlockSpec((tm, tn), lambda i, j: (i, j)),
        out_shape=jax.ShapeDtypeStruct((M, N), out_dtype),
        compiler_params=_cparams(("parallel", "parallel")),
        name="proj_mm",
    )(x, wT)


def _kcat(k):
    kh, kl = _split(k)
    return jnp.concatenate([kh, kl, kh, jnp.zeros_like(kh)], axis=-1)


def _proj_idx_body(x_ref, wqT_ref, wkT_ref, wwT_ref, g_ref, b_ref, qi_ref, ki_ref, wi_ref):
    x = x_ref[...]
    qi_ref[...] = _dot3(x, wqT_ref[...], _NT)
    ki_ref[...] = _layer_norm(_dot3(x, wkT_ref[...], _NT), g_ref[...], b_ref[...])
    wi_ref[...] = _dot3(x, wwT_ref[...], _NT) * (IDX_HEADS ** -0.5)


def _proj_idx(x, wqT, wkT, wwT, g, b):
    M, K = x.shape
    tm = min(M, 512)
    full = lambda a: pl.BlockSpec(a.shape, lambda i: (0,) * a.ndim)
    row = lambda n: pl.BlockSpec((tm, n), lambda i: (i, 0))
    return pl.pallas_call(
        _proj_idx_body,
        grid=(M // tm,),
        in_specs=[row(K), full(wqT), full(wkT), full(wwT), full(g), full(b)],
        out_specs=[row(IDX_HEADS * IDX_DIM), row(IDX_DIM), row(IDX_HEADS)],
        out_shape=[jax.ShapeDtypeStruct((M, IDX_HEADS * IDX_DIM), F32),
                   jax.ShapeDtypeStruct((M, IDX_DIM), F32),
                   jax.ShapeDtypeStruct((M, IDX_HEADS), F32)],
        compiler_params=_cparams(("parallel",)),
        name="proj_idx",
    )(x, wqT, wkT, wwT, g, b)


def _bias_tiles_body(rb_ref, o_ref, *, T):
    rel = pl.program_id(0)
    h = pl.program_id(1)
    key = lax.broadcasted_iota(I32, (T, T), 0)
    qry = lax.broadcasted_iota(I32, (T, T), 1)
    d = qry - key + rel * T
    o_ref[...] = _bias_chain(d, lambda b: rb_ref[b, h])


def _bias_tiles(rb, T):
    H = rb.shape[1]
    return pl.pallas_call(
        functools.partial(_bias_tiles_body, T=T),
        grid=(2, H),
        in_specs=[pl.BlockSpec(memory_space=pltpu.SMEM)],
        out_specs=pl.BlockSpec((None, None, T, T), lambda r, h: (r, h, 0, 0)),
        out_shape=jax.ShapeDtypeStruct((2, H, T, T), F32),
        compiler_params=_cparams(("parallel", "parallel")),
        name="bias_tiles",
    )(rb)


def _attn_update(logits_of, pen_of, vT_of, m_ref, l_ref, acc_ref):
    heads = range(N_HEADS)
    s = [logits_of(h) + pen_of(h) for h in heads]
    m_old = [m_ref[h] for h in heads]
    m_new = [jnp.maximum(m_old[h], jnp.max(s[h], axis=0, keepdims=True)) for h in heads]
    p = [jnp.exp(s[h] - m_new[h]) for h in heads]
    alpha = [jnp.exp(m_old[h] - m_new[h]) for h in heads]
    for h in heads:
        l_ref[h] = alpha[h] * l_ref[h] + jnp.sum(p[h], axis=0, keepdims=True)
        m_ref[h] = m_new[h]
    pv = [_dot(vT_of(h), p[h].astype(BF16)) for h in heads]
    for h in heads:
        acc_ref[h] = alpha[h] * acc_ref[h] + pv[h]


def _attn_init(m_ref, l_ref, acc_ref):
    m_ref[...] = jnp.full(m_ref.shape, NEG_BIG, F32)
    l_ref[...] = jnp.zeros(l_ref.shape, F32)
    acc_ref[...] = jnp.zeros(acc_ref.shape, F32)


def _attn_finish(o_ref, l_ref, acc_ref):
    outT = jnp.concatenate([acc_ref[h] / l_ref[h] for h in range(N_HEADS)], axis=0)
    o_ref[...] = outT.T


def _padded_heads(qT_of):
    out = []
    for h in range(N_HEADS):
        q = qT_of(h)
        z = jnp.zeros_like(q)
        out.append(jnp.concatenate([q, z] if h % 2 == 0 else [z, q], axis=0))
    return out


def _head_pair(k, h):
    lo = (h // 2) * 2 * HEAD_DIM
    return k[:, lo:lo + 2 * HEAD_DIM]


def _dsa_prompt_body(rbf_ref, bt_ref, qa_ref, k_ref, vT_ref, qi_ref, wi_ref, kcat_ref, o_ref,
                     key_ref, m_ref, l_ref, acc_ref, *, T, n_top, idx_bits):
    i = pl.program_id(1)
    H = N_HEADS
    kpos = lax.broadcasted_iota(I32, (T, T), 0)
    qpos = lax.broadcasted_iota(I32, (T, T), 1)

    wi = wi_ref[...]
    qcat = []
    for h in range(IDX_HEADS):
        qh, ql = _split(qi_ref[h * IDX_DIM:(h + 1) * IDX_DIM, :])
        qcat.append(jnp.concatenate([qh, qh, ql, jnp.zeros_like(qh)], axis=0))

    def score_chunk(kc, carry):
        kcat = kcat_ref[pl.ds(pl.multiple_of(kc * T, T), T), :]
        s = jnp.zeros((T, T), F32)
        for h in range(IDX_HEADS):
            s = s + jnp.maximum(_dot(kcat, qcat[h]), 0.0) * wi[h:h + 1, :]
        key = _order_key(s)
        key_ref[kc] = jnp.where((kc == i) & (kpos > qpos), INT_MIN, key)
        return carry

    lax.fori_loop(0, i + 1, score_chunk, 0)

    def count(pred):
        def body(kc, acc):
            hit = jnp.where(pred(key_ref[kc], kc * T + kpos), 1.0, 0.0)
            return acc + jnp.sum(hit.reshape(T // 8, 8, T), axis=0)
        acc = lax.fori_loop(0, i + 1, body, jnp.zeros((8, T), F32))
        return jnp.sum(acc, axis=0, keepdims=True)

    def thr_bit(bi, prefix):
        cand = prefix | jnp.left_shift(jnp.int32(1), 31 - bi)
        cs = cand ^ INT_MIN
        return jnp.where(count(lambda k, g: k >= cs) >= n_top, cand, prefix)

    thr = lax.fori_loop(0, 32, thr_bit, jnp.zeros((1, T), I32)) ^ INT_MIN
    need = n_top - count(lambda k, g: k > thr)
    n_tie = count(lambda k, g: k == thr)

    def tie_search():
        def bit(bi, j0):
            cand = j0 | jnp.left_shift(jnp.int32(1), idx_bits - 1 - bi)
            c = count(lambda k, g: (k == thr) & (g < cand))
            return jnp.where(c < need, cand, j0)
        return lax.fori_loop(0, idx_bits, bit, jnp.zeros((1, T), I32))

    any_excess = jnp.max(n_tie - need) > 0.0
    jcut = lax.cond(any_excess, tie_search, lambda: jnp.full((1, T), 2 ** 30, I32))

    _attn_init(m_ref, l_ref, acc_ref)
    q = _padded_heads(lambda h: qa_ref[h * HEAD_DIM:(h + 1) * HEAD_DIM, :])

    def attn_chunk(kc, near):
        kk = k_ref[pl.ds(pl.multiple_of(kc * T, T), T), :]
        keys = key_ref[kc]
        g = kc * T + kpos
        sel = (keys > thr) | ((keys == thr) & (g <= jcut))
        if near:
            sel = sel & (g <= i * T + qpos)
        pen = jnp.where(sel, 0.0, NEG_BIG)
        _attn_update(
            lambda h: _dot(_head_pair(kk, h), q[h]) + (bt_ref[i - kc, h] if near else rbf_ref[h]),
            lambda h: pen,
            lambda h: vT_ref[h * HEAD_DIM:(h + 1) * HEAD_DIM, pl.ds(pl.multiple_of(kc * T, T), T)],
            m_ref, l_ref, acc_ref)

    n_far = jnp.maximum(i - 1, 0)
    lax.fori_loop(0, n_far, lambda kc, c: (attn_chunk(kc, False), c)[1], 0)
    lax.fori_loop(n_far, i + 1, lambda kc, c: (attn_chunk(kc, True), c)[1], 0)
    _attn_finish(o_ref, l_ref, acc_ref)


def _attn_scratch(T):
    return [pltpu.VMEM((N_HEADS, 1, T), F32), pltpu.VMEM((N_HEADS, 1, T), F32),
            pltpu.VMEM((N_HEADS, HEAD_DIM, T), F32)]


def _dsa_prompt(rb_far, btiles, qaT, k16, vT, qiT, wiT, kcat, *, T):
    B, _, S = qaT.shape
    M = B * S
    H = N_HEADS
    nc = S // T
    n_top = min(IDX_TOPK, S // 4)
    body = functools.partial(_dsa_prompt_body, T=T, n_top=n_top, idx_bits=max(1, (S - 1).bit_length()))
    tile = lambda rows: pl.BlockSpec((None, rows, T), lambda b, i: (b, 0, i))
    return pl.pallas_call(
        body,
        grid=(B, nc),
        in_specs=[pl.BlockSpec(memory_space=pltpu.SMEM),
                  pl.BlockSpec((2, H, T, T), lambda b, i: (0, 0, 0, 0)),
                  tile(D_ATT),
                  pl.BlockSpec((S, D_ATT), lambda b, i: (b, 0)),
                  pl.BlockSpec((None, D_ATT, S), lambda b, i: (b, 0, 0)),
                  tile(IDX_HEADS * IDX_DIM),
                  tile(IDX_HEADS),
                  pl.BlockSpec((S, 4 * IDX_DIM), lambda b, i: (b, 0))],
        out_specs=pl.BlockSpec((T, D_ATT), lambda b, i: (b * nc + i, 0)),
        out_shape=jax.ShapeDtypeStruct((M, D_ATT), F32),
        scratch_shapes=[pltpu.VMEM((nc, T, T), I32)] + _attn_scratch(T),
        compiler_params=_cparams(("parallel", "arbitrary")),
        name="dsa_prompt",
    )(rb_far, btiles, qaT, k16, vT, qiT, wiT, kcat)


def _block_means_body(kT_ref, o_ref):
    nf = o_ref.shape[-1]
    lane = lax.broadcasted_iota(I32, o_ref.shape, 1)
    means = jnp.zeros(o_ref.shape, F32)
    for j in range(nf):
        col = jnp.sum(kT_ref[:, j * MOBA_BLOCK:(j + 1) * MOBA_BLOCK], axis=1, keepdims=True) * (1.0 / MOBA_BLOCK)
        means = jnp.where(lane == j, col, means)
    o_ref[...] = means


def _block_means(kT):
    B, D, L = kT.shape
    nf = L // MOBA_BLOCK
    return pl.pallas_call(
        _block_means_body,
        grid=(B,),
        in_specs=[pl.BlockSpec((None, D, L), lambda b: (b, 0, 0))],
        out_specs=pl.BlockSpec((None, D, nf), lambda b: (b, 0, 0)),
        out_shape=jax.ShapeDtypeStruct((B, D, nf), F32),
        compiler_params=_cparams(("parallel",)),
        name="block_means",
    )(kT)


def _moba_prompt_body(rbf_ref, bt_ref, qbT_ref, k_ref, vT_ref, mbdT_ref, o_ref,
                      ch_ref, m_ref, l_ref, acc_ref, *, T, n_sel, n_slots):
    i = pl.program_id(1)
    H = N_HEADS
    W = n_slots * H
    kpos = lax.broadcasted_iota(I32, (T, T), 0)
    qpos = lax.broadcasted_iota(I32, (T, T), 1)
    qT = qbT_ref[...]

    g = _dot3(mbdT_ref[...], qT)
    blk = jnp.right_shift(lax.broadcasted_iota(I32, (W, T), 0), int(math.log2(H)))
    g = jnp.where(blk < i, g, -jnp.inf)
    rank = jnp.zeros((W, T), F32)
    for r in range(1, n_slots):
        other = pltpu.roll(g, r * H, axis=0)
        rank = rank + jnp.where(blk >= r, jnp.where(other >= g, 1.0, 0.0), jnp.where(other > g, 1.0, 0.0))
    ch_ref[...] = jnp.where((rank < n_sel) & (blk < i), 0.0, NEG_BIG)

    _attn_init(m_ref, l_ref, acc_ref)
    q = _padded_heads(lambda h: (qT[h * HEAD_DIM:(h + 1) * HEAD_DIM, :] * ATTN_SCALE).astype(BF16))
    causal_pen = jnp.where(kpos <= qpos, 0.0, NEG_BIG)

    def attn_chunk(kc, near):
        kk = k_ref[pl.ds(pl.multiple_of(kc * T, T), T), :]

        def pen_of(h):
            pen = ch_ref[pl.ds(kc * H + h, 1), :]
            return jnp.where(kc == i, causal_pen, pen) if near else pen

        _attn_update(
            lambda h: _dot(_head_pair(kk, h), q[h]) + (bt_ref[i - kc, h] if near else rbf_ref[h]),
            pen_of,
            lambda h: vT_ref[h * HEAD_DIM:(h + 1) * HEAD_DIM, pl.ds(pl.multiple_of(kc * T, T), T)],
            m_ref, l_ref, acc_ref)

    n_far = jnp.maximum(i - 1, 0)
    lax.fori_loop(0, n_far, lambda kc, c: (attn_chunk(kc, False), c)[1], 0)
    lax.fori_loop(n_far, i + 1, lambda kc, c: (attn_chunk(kc, True), c)[1], 0)
    _attn_finish(o_ref, l_ref, acc_ref)


def _moba_prompt(rb_far, btiles, qbT, k16, vT, mbdT):
    B, _, S = qbT.shape
    M = B * S
    H = N_HEADS
    T = MOBA_BLOCK
    nc = S // T
    W = mbdT.shape[1]
    body = functools.partial(_moba_prompt_body, T=T, n_sel=min(MOBA_TOPK, nc), n_slots=W // H)
    return pl.pallas_call(
        body,
        grid=(B, nc),
        in_specs=[pl.BlockSpec(memory_space=pltpu.SMEM),
                  pl.BlockSpec((2, H, T, T), lambda b, i: (0, 0, 0, 0)),
                  pl.BlockSpec((None, D_ATT, T), lambda b, i: (b, 0, i)),
                  pl.BlockSpec((S, D_ATT), lambda b, i: (b, 0)),
                  pl.BlockSpec((None, D_ATT, S), lambda b, i: (b, 0, 0)),
                  pl.BlockSpec((None, W, D_ATT), lambda b, i: (b, 0, 0))],
        out_specs=pl.BlockSpec((T, D_ATT), lambda b, i: (b * nc + i, 0)),
        out_shape=jax.ShapeDtypeStruct((M, D_ATT), F32),
        scratch_shapes=[pltpu.VMEM((W, T), F32)] + _attn_scratch(T),
        compiler_params=_cparams(("parallel", "arbitrary")),
        name="moba_prompt",
    )(rb_far, btiles, qbT, k16, vT, mbdT)


def _merge_body(x_ref, oa_ref, ob_ref, g_ref, p_ref, wba_ref, wbb_ref, wo_ref, g1_ref, b1_ref,
                wr_ref, br_ref, wpg_ref, wpp_ref, x1b_ref, comb_ref, res_ref):
    bra = _dot(oa_ref[...].astype(BF16), wba_ref[...])
    brb = _dot(ob_ref[...].astype(BF16), wbb_ref[...])
    gates = g_ref[...]
    mix = gates[:, :D_MODEL] * bra + gates[:, D_MODEL:] * brb
    y = _dot(mix.astype(BF16), wo_ref[...])
    x1 = _layer_norm(ALPHA_DN * x_ref[...] + y, g1_ref[...], b1_ref[...])
    x1b = x1.astype(BF16)
    x1b_ref[...] = x1b

    logits = _dot3(x1, wr_ref[...]) + br_ref[...]
    lane = lax.broadcasted_iota(I32, logits.shape, 1)
    work = logits
    kept = jnp.zeros(logits.shape, jnp.bool_)
    for _ in range(TOP_K):
        mx = jnp.max(work, axis=-1, keepdims=True)
        first = jnp.min(jnp.where(work == mx, lane, N_EXPERTS), axis=-1, keepdims=True)
        hit = lane == first
        kept = kept | hit
        work = jnp.where(hit, -jnp.inf, work)
    top = jnp.max(logits, axis=-1, keepdims=True)
    e = jnp.where(kept, jnp.exp(logits - top), 0.0)
    comb_ref[...] = e / jnp.sum(e, axis=-1, keepdims=True)

    ple = jax.nn.sigmoid(_dot(x1b, wpg_ref[...])) * _dot(p_ref[...].astype(BF16), wpp_ref[...])
    res_ref[...] = ALPHA_DN * x1 + ple


def _merge(x, oa, ob, gates, p, wba, wbb, wo, g1, b1, wr, br, wpg, wpp):
    M = x.shape[0]
    tm = min(M, 512)
    full = lambda a: pl.BlockSpec(a.shape, lambda i: (0,) * a.ndim)
    row = lambda n: pl.BlockSpec((tm, n), lambda i: (i, 0))
    return pl.pallas_call(
        _merge_body,
        grid=(M // tm,),
        in_specs=[row(D_MODEL), row(D_ATT), row(D_ATT), row(2 * D_MODEL), row(PLE_DIM),
                  full(wba), full(wbb), full(wo), full(g1), full(b1), full(wr), full(br), full(wpg), full(wpp)],
        out_specs=[row(D_MODEL), row(N_EXPERTS), row(D_MODEL)],
        out_shape=[jax.ShapeDtypeStruct((M, D_MODEL), BF16),
                   jax.ShapeDtypeStruct((M, N_EXPERTS), F32),
                   jax.ShapeDtypeStruct((M, D_MODEL), F32)],
        compiler_params=_cparams(("parallel",)),
        name="merge",
    )(x, oa, ob, gates, p, wba, wbb, wo, g1, b1, wr, br, wpg, wpp)


MOE_TM = 1024
MOE_RB = 160
MOE_CB = 256


def _moe_body(xb_ref, res_ref, comb_ref, wgu_ref, bgu_ref, wdn_ref, bdn_ref, g2_ref, b2_ref,
              o_ref, rank_ref, rankT_ref, acc_ref):
    e = pl.program_id(1)
    TM, E = comb_ref.shape
    RB = MOE_RB

    @pl.when(e == 0)
    def _():
        routed = jnp.where(comb_ref[...] != 0.0, 1.0, 0.0)
        eye = jnp.where(lax.broadcasted_iota(I32, (E, E), 0) == lax.broadcasted_iota(I32, (E, E), 1),
                        1.0, 0.0).astype(BF16)
        routedT = _dot(eye, routed.astype(BF16), _NT)
        CB = min(MOE_CB, TM)
        r_i = lax.broadcasted_iota(I32, (CB, CB), 0)
        c_i = lax.broadcasted_iota(I32, (CB, CB), 1)
        before = jnp.where(c_i < r_i, 1.0, 0.0).astype(BF16)
        beforeT = jnp.where(r_i < c_i, 1.0, 0.0).astype(BF16)
        off = jnp.zeros((1, E), F32)
        offT = jnp.zeros((E, 1), F32)
        for blk in range(TM // CB):
            rb = routed[blk * CB:(blk + 1) * CB]
            rbT = routedT[:, blk * CB:(blk + 1) * CB]
            rank = _dot(before, rb.astype(BF16)) + off
            rankT = _dot(rbT.astype(BF16), beforeT) + offT
            rank_ref[blk * CB:(blk + 1) * CB, :] = jnp.where(rb > 0.5, rank, -1.0).astype(I32)
            rankT_ref[:, blk * CB:(blk + 1) * CB] = jnp.where(rbT > 0.5, rankT, -1.0).astype(I32)
            off = off + jnp.sum(rb, axis=0, keepdims=True)
            offT = offT + jnp.sum(rbT, axis=1, keepdims=True)
        acc_ref[...] = jnp.zeros(acc_ref.shape, F32)

    lane = lax.broadcasted_iota(I32, (TM, E), 1)
    gate_col = jnp.sum(jnp.where(lane == e, comb_ref[...], 0.0), axis=-1, keepdims=True)
    rank_col = jnp.sum(jnp.where(lane == e, rank_ref[...], 0), axis=-1, keepdims=True)
    rank_row = rankT_ref[pl.ds(e, 1), :]
    n_routed = jnp.sum(jnp.where(rank_row >= 0, 1, 0))
    n_pass = (n_routed + (RB - 1)) // RB
    slot_rows = lax.broadcasted_iota(I32, (RB, TM), 0)
    slot_lanes = lax.broadcasted_iota(I32, (TM, RB), 1)

    def one_pass(pi, carry):
        base = pi * RB
        pick = jnp.where(rank_row == slot_rows + base, 1.0, 0.0).astype(BF16)
        xg = _dot(pick, xb_ref[...]).astype(BF16)
        gu = _dot(xg, wgu_ref[...]) + bgu_ref[...]
        gt = jnp.minimum(gu[:, :D_EXPERT], SWIGLU_LIMIT)
        up = jnp.clip(gu[:, D_EXPERT:], -SWIGLU_LIMIT, SWIGLU_LIMIT)
        hid = (up + 1.0) * gt * jax.nn.sigmoid(SWIGLU_ALPHA * gt)
        down = _dot(hid.astype(BF16), wdn_ref[...])
        place = jnp.where(rank_col == slot_lanes + base, 1.0, 0.0).astype(BF16)
        acc_ref[...] += _dot(place, down.astype(BF16)) * gate_col
        return carry

    lax.fori_loop(0, n_pass, one_pass, 0)

    @pl.when(e == pl.num_programs(1) - 1)
    def _():
        y = acc_ref[...] + _dot3(comb_ref[...], bdn_ref[...])
        o_ref[...] = _layer_norm(res_ref[...] + y, g2_ref[...], b2_ref[...])


def _moe(xb, res, comb, w_gu, b_gu, w_dn, b_dn, g2, b2):
    M = xb.shape[0]
    tm = min(M, MOE_TM)
    E = w_gu.shape[0]
    b_gu3 = b_gu.reshape(E, 1, 2 * D_EXPERT)
    row = lambda n: pl.BlockSpec((tm, n), lambda i, e: (i, 0))
    return pl.pallas_call(
        _moe_body,
        grid=(M // tm, E),
        in_specs=[row(D_MODEL), row(D_MODEL), row(E),
                  pl.BlockSpec((None, D_MODEL, 2 * D_EXPERT), lambda i, e: (e, 0, 0)),
                  pl.BlockSpec((None, 1, 2 * D_EXPERT), lambda i, e: (e, 0, 0)),
                  pl.BlockSpec((None, D_EXPERT, D_MODEL), lambda i, e: (e, 0, 0)),
                  pl.BlockSpec((E, D_MODEL), lambda i, e: (0, 0)),
                  pl.BlockSpec((1, D_MODEL), lambda i, e: (0, 0)),
                  pl.BlockSpec((1, D_MODEL), lambda i, e: (0, 0))],
        out_specs=row(D_MODEL),
        out_shape=jax.ShapeDtypeStruct((M, D_MODEL), F32),
        scratch_shapes=[pltpu.VMEM((tm, E), I32), pltpu.VMEM((E, tm), I32), pltpu.VMEM((tm, D_MODEL), F32)],
        compiler_params=_cparams(("parallel", "arbitrary")),
        name="moe",
    )(xb, res, comb, w_gu, b_gu3, w_dn, b_dn, g2, b2)


SEL_PAGES = 16
ATT_PAGES = 16


def _page_specs(n, rows, width, li):
    def spec(u):
        return pl.BlockSpec((None, None, rows, width),
                            lambda b, c, pt: (li, pt[b, c * n + u], 0, 0))
    return [spec(u) for u in range(n)]


def _smp_dsa_select_body(pt_ref, qi_ref, wi_ref, knew_ref, *rest, n_pages, n_top, idx_bits):
    pages = rest[:SEL_PAGES]
    sel_ref = rest[SEL_PAGES]
    key_ref = rest[SEL_PAGES + 1]
    c = pl.program_id(1)
    T = qi_ref.shape[0] // IDX_HEADS
    qi = qi_ref[...]
    wi = wi_ref[...] * IDX_SCALE

    def page_score(kpT):
        s = jnp.maximum(_dot3(qi, kpT), 0.0) * wi
        return jnp.sum(s.reshape(IDX_HEADS, T, kpT.shape[1]), axis=0)

    keys = _order_key(page_score(jnp.concatenate([pages[u][...] for u in range(SEL_PAGES)], axis=1)))
    for u in range(SEL_PAGES):
        key_ref[c * SEL_PAGES + u] = keys[:, u * PAGE_SIZE:(u + 1) * PAGE_SIZE]

    @pl.when(c == pl.num_programs(1) - 1)
    def _():
        qrow = lax.broadcasted_iota(I32, (T, PAGE_SIZE), 0)
        lane = lax.broadcasted_iota(I32, (T, PAGE_SIZE), 1)
        key_ref[n_pages] = jnp.where(lane <= qrow, _order_key(page_score(knew_ref[...])), INT_MIN)
        keys = key_ref[...]
        shape = keys.shape
        gidx = lax.broadcasted_iota(I32, shape, 0) * PAGE_SIZE + lax.broadcasted_iota(I32, shape, 2)
        valid = (lax.broadcasted_iota(I32, shape, 0) < n_pages) | \
                (lax.broadcasted_iota(I32, shape, 2) <= lax.broadcasted_iota(I32, shape, 1))

        def count(hit):
            per_lane = jnp.sum(hit.astype(I32), axis=0)
            return jnp.sum(per_lane, axis=-1, keepdims=True)[None]

        def thr_bit(bi, prefix):
            cand = prefix | jnp.left_shift(jnp.int32(1), 31 - bi)
            cs = cand ^ INT_MIN
            return jnp.where(count(keys >= cs) >= n_top, cand, prefix)

        thr = lax.fori_loop(0, 32, thr_bit, jnp.zeros((1, T, 1), I32)) ^ INT_MIN
        need = n_top - count(keys > thr)

        def tie_bit(bi, j0):
            cand = j0 | jnp.left_shift(jnp.int32(1), idx_bits - 1 - bi)
            return jnp.where(count((keys == thr) & (gidx < cand)) < need, cand, j0)

        jcut = lax.fori_loop(0, idx_bits, tie_bit, jnp.zeros((1, T, 1), I32))
        sel = ((keys > thr) | ((keys == thr) & (gidx <= jcut))) & valid
        sel_ref[...] = sel.astype(F32)


def _smp_dsa_select(page_table, qi_rows, wi_rows, knew, cache_kidx, li):
    DB, P = page_table.shape
    T = qi_rows.shape[1] // IDX_HEADS
    L = P * PAGE_SIZE + T
    body = functools.partial(_smp_dsa_select_body, n_pages=P, n_top=min(IDX_TOPK, L // 4),
                             idx_bits=max(1, ((P + 1) * PAGE_SIZE - 1).bit_length()))
    grid_spec = pltpu.PrefetchScalarGridSpec(
        num_scalar_prefetch=1,
        grid=(DB, P // SEL_PAGES),
        in_specs=[pl.BlockSpec((None, IDX_HEADS * T, IDX_DIM), lambda b, c, pt: (b, 0, 0)),
                  pl.BlockSpec((None, IDX_HEADS * T, 1), lambda b, c, pt: (b, 0, 0)),
                  pl.BlockSpec((None, IDX_DIM, PAGE_SIZE), lambda b, c, pt: (b, 0, 0))]
                 + _page_specs(SEL_PAGES, IDX_DIM, PAGE_SIZE, li),
        out_specs=pl.BlockSpec((None, P + 1, T, PAGE_SIZE), lambda b, c, pt: (b, 0, 0, 0)),
        scratch_shapes=[pltpu.VMEM((P + 1, T, PAGE_SIZE), I32)],
    )
    return pl.pallas_call(
        body,
        grid_spec=grid_spec,
        out_shape=jax.ShapeDtypeStruct((DB, P + 1, T, PAGE_SIZE), F32),
        compiler_params=_cparams(("parallel", "arbitrary")),
        name="smp_dsa_select",
    )(page_table, qi_rows, wi_rows, knew, *([cache_kidx] * SEL_PAGES))


def _smp_moba_select_body(pt_ref, qbd_ref, *rest, n_blocks, n_sel):
    pages = rest[:SEL_PAGES]
    sel_ref = rest[SEL_PAGES]
    mean_ref = rest[SEL_PAGES + 1]
    c = pl.program_id(1)
    per_step = SEL_PAGES * PAGE_SIZE // MOBA_BLOCK
    per_block = MOBA_BLOCK // PAGE_SIZE
    W = mean_ref.shape[1]
    lane = lax.broadcasted_iota(I32, mean_ref.shape, 1)

    @pl.when(c == 0)
    def _():
        mean_ref[...] = jnp.zeros(mean_ref.shape, F32)

    means = mean_ref[...]
    for j in range(per_step):
        tot = pages[j * per_block][...]
        for u in range(1, per_block):
            tot = tot + pages[j * per_block + u][...]
        col = jnp.sum(tot, axis=1, keepdims=True) * (1.0 / MOBA_BLOCK)
        means = jnp.where(lane == c * per_step + j, col, means)
    mean_ref[...] = means

    @pl.when(c == pl.num_programs(1) - 1)
    def _():
        g = _dot3(qbd_ref[...], means)
        blk = lax.broadcasted_iota(I32, g.shape, 1)
        g = jnp.where(blk < n_blocks, g, -jnp.inf)
        rank = jnp.zeros(g.shape, F32)
        for r in range(1, n_blocks):
            lower = pltpu.roll(g, r, axis=1)
            upper = pltpu.roll(g, W - r, axis=1)
            rank = rank + jnp.where(lower >= g, 1.0, 0.0) + jnp.where(upper > g, 1.0, 0.0)
        sel_ref[...] = jnp.where((rank < n_sel) & (blk < n_blocks), 1.0, 0.0)


def _smp_moba_select(page_table, qbd, cache_kT, li):
    DB, P = page_table.shape
    n_blocks = P * PAGE_SIZE // MOBA_BLOCK
    n_steps = P // SEL_PAGES
    W = 128
    assert n_blocks < W
    R = qbd.shape[1]
    body = functools.partial(_smp_moba_select_body, n_blocks=n_blocks, n_sel=min(MOBA_TOPK, n_blocks))
    grid_spec = pltpu.PrefetchScalarGridSpec(
        num_scalar_prefetch=1,
        grid=(DB, n_steps),
        in_specs=[pl.BlockSpec((None, R, D_ATT), lambda b, c, pt: (b, 0, 0))]
                 + _page_specs(SEL_PAGES, D_ATT, PAGE_SIZE, li),
        out_specs=pl.BlockSpec((None, R, W), lambda b, c, pt: (b, 0, 0)),
        scratch_shapes=[pltpu.VMEM((D_ATT, W), F32)],
    )
    return pl.pallas_call(
        body,
        grid_spec=grid_spec,
        out_shape=jax.ShapeDtypeStruct((DB, R, W), F32),
        compiler_params=_cparams(("parallel", "arbitrary")),
        name="smp_moba_select",
    )(page_table, qbd, *([cache_kT] * SEL_PAGES))


def _smp_attn_body(pt_ref, qbd_ref, rbr_ref, knew_ref, vnew_ref, sel_ref, selnew_ref, *rest, mode, n_pages):
    kp = rest[:ATT_PAGES]
    vp = rest[ATT_PAGES:2 * ATT_PAGES]
    o_ref, m_ref, l_ref, acc_ref = rest[2 * ATT_PAGES:]
    c = pl.program_id(1)
    R = qbd_ref.shape[0]
    T = R // N_HEADS
    q = (qbd_ref[...] * ATTN_SCALE).astype(BF16)
    rbr = rbr_ref[...]
    far_bias = rbr[:, N_BUCKETS - 1:]
    rowq = lax.broadcasted_iota(I32, (R, PAGE_SIZE), 0) & (T - 1)
    lane = lax.broadcasted_iota(I32, (R, PAGE_SIZE), 1)

    @pl.when(c == 0)
    def _():
        m_ref[...] = jnp.full(m_ref.shape, NEG_BIG, F32)
        l_ref[...] = jnp.zeros(l_ref.shape, F32)
        acc_ref[...] = jnp.zeros(acc_ref.shape, F32)

    def bias_for(dist):
        return _bias_chain(dist, lambda b: rbr[:, b:b + 1])

    def update(s, vT):
        m_old = m_ref[...]
        m_new = jnp.maximum(m_old, jnp.max(s, axis=-1, keepdims=True))
        p = jnp.exp(s - m_new)
        alpha = jnp.exp(m_old - m_new)
        l_ref[...] = alpha * l_ref[...] + jnp.sum(p, axis=-1, keepdims=True)
        acc_ref[...] = alpha * acc_ref[...] + _dot(p.astype(BF16), vT, _NT)
        m_ref[...] = m_new

    logits, vals = [], []
    for u in range(ATT_PAGES):
        page = c * ATT_PAGES + u
        s = _dot(q, kp[u][...].astype(BF16))
        if u == ATT_PAGES - 1:
            dist = (n_pages - page) * PAGE_SIZE + rowq - lane
            bias = lax.cond(c == pl.num_programs(1) - 1, lambda: bias_for(dist),
                            lambda: jnp.broadcast_to(far_bias, (R, PAGE_SIZE)))
        else:
            bias = far_bias
        if mode == "dsa":
            pen = jnp.where(jnp.tile(sel_ref[u], (N_HEADS, 1)) > 0.5, 0.0, NEG_BIG)
        else:
            blk = page // (MOBA_BLOCK // PAGE_SIZE)
            sel = sel_ref[...]
            pick = jnp.sum(jnp.where(lax.broadcasted_iota(I32, sel.shape, 1) == blk, sel, 0.0),
                           axis=-1, keepdims=True)
            pen = jnp.where(pick > 0.5, 0.0, NEG_BIG)
        logits.append(s + (bias + pen))
        vals.append(vp[u][...].astype(BF16))
    update(jnp.concatenate(logits, axis=1), jnp.concatenate(vals, axis=1))

    @pl.when(c == pl.num_programs(1) - 1)
    def _():
        s = _dot(q, knew_ref[...].astype(BF16)) + bias_for(rowq - lane)
        keep = lane <= rowq
        if mode == "dsa":
            keep = keep & (jnp.tile(selnew_ref[...], (N_HEADS, 1)) > 0.5)
        update(s + jnp.where(keep, 0.0, NEG_BIG), vnew_ref[...].astype(BF16))
        out = acc_ref[...] / l_ref[...]
        for h in range(N_HEADS):
            o_ref[:, h * HEAD_DIM:(h + 1) * HEAD_DIM] = out[h * T:(h + 1) * T, h * HEAD_DIM:(h + 1) * HEAD_DIM]


def _smp_attn(page_table, qbd, rbr, knewT, vnewT, sel, cache_kT, cache_vT, li, *, mode):
    DB, P = page_table.shape
    R = qbd.shape[1]
    T = R // N_HEADS
    assert T & (T - 1) == 0
    if mode == "dsa":
        sel_specs = [pl.BlockSpec((None, ATT_PAGES, T, PAGE_SIZE), lambda b, c, pt: (b, c, 0, 0)),
                     pl.BlockSpec((None, None, T, PAGE_SIZE), lambda b, c, pt: (b, P, 0, 0))]
    else:
        sel_specs = [pl.BlockSpec((None, R, sel.shape[-1]), lambda b, c, pt: (b, 0, 0)),
                     pl.BlockSpec((None, R, sel.shape[-1]), lambda b, c, pt: (b, 0, 0))]
    new_spec = pl.BlockSpec((None, D_ATT, PAGE_SIZE), lambda b, c, pt: (b, 0, 0))
    grid_spec = pltpu.PrefetchScalarGridSpec(
        num_scalar_prefetch=1,
        grid=(DB, P // ATT_PAGES),
        in_specs=[pl.BlockSpec((None, R, D_ATT), lambda b, c, pt: (b, 0, 0)),
                  pl.BlockSpec(rbr.shape, lambda b, c, pt: (0, 0)),
                  new_spec, new_spec]
                 + sel_specs + _page_specs(ATT_PAGES, D_ATT, PAGE_SIZE, li) + _page_specs(ATT_PAGES, D_ATT, PAGE_SIZE, li),
        out_specs=pl.BlockSpec((None, T, D_ATT), lambda b, c, pt: (b, 0, 0)),
        scratch_shapes=[pltpu.VMEM((R, 1), F32), pltpu.VMEM((R, 1), F32), pltpu.VMEM((R, D_ATT), F32)],
    )
    return pl.pallas_call(
        functools.partial(_smp_attn_body, mode=mode, n_pages=P),
        grid_spec=grid_spec,
        out_shape=jax.ShapeDtypeStruct((DB, T, D_ATT), F32),
        compiler_params=_cparams(("parallel", "arbitrary")),
        name="smp_attn_" + mode,
    )(page_table, qbd, rbr, knewT, vnewT, sel, sel, *([cache_kT] * ATT_PAGES), *([cache_vT] * ATT_PAGES))


def _project(x2, wT, g, b):
    o = IN_OFFS
    qkv_a = _mm(x2, wT[o[0]:o[3]])
    qi, ki, wi = _proj_idx(x2, wT[o[3]:o[4]], wT[o[4]:o[5]], wT[o[5]:o[6]], g[None], b[None])
    qk_b = _mm(x2, wT[o[6]:o[8]], passes=3)
    v_b = _mm(x2, wT[o[8]:o[9]])
    gates = _mm(x2, wT[o[9]:o[10]], act="sigmoid")
    return dict(q_a=qkv_a[:, :D_ATT], k_a=qkv_a[:, D_ATT:2 * D_ATT], v_a=qkv_a[:, 2 * D_ATT:],
                q_i=qi, k_i=ki, w_i=wi, q_b=qk_b[:, :D_ATT], k_b=qk_b[:, D_ATT:], v_b=v_b, gates=gates)


PROJ_TM = 512


def _proj_bulk_body(x_ref, w_ref, qaT_ref, kaT_ref, ka16_ref, vaT_ref, vaT16_ref, vbT_ref, vbT16_ref, g_ref):
    xb = x_ref[...].astype(BF16)
    rows = lambda n: w_ref[n * D_ATT:(n + 1) * D_ATT, :]
    qaT_ref[...] = (_dot(rows(0), xb, _NT) * ATTN_SCALE).astype(BF16)
    kaT = _dot(rows(1), xb, _NT)
    kaT_ref[...] = kaT
    ka16_ref[...] = kaT.T.astype(BF16)
    vaT = _dot(rows(2), xb, _NT)
    vaT_ref[...] = vaT
    vaT16_ref[...] = vaT.astype(BF16)
    vbT = _dot(rows(3), xb, _NT)
    vbT_ref[...] = vbT
    vbT16_ref[...] = vbT.astype(BF16)
    g_ref[...] = jax.nn.sigmoid(_dot(xb, w_ref[4 * D_ATT:, :], _NT))


def _proj_select_body(x_ref, w_ref, g_ref, b_ref, gc_ref, bc_ref,
                      qiT_ref, kiT_ref, kcat_ref, wiT_ref, qbT_ref, kbT_ref, kb16_ref):
    xh, xl = _split(x_ref[...])

    def proj_T(lo, hi):
        wh, wl = _split(w_ref[lo:hi, :])
        return _dot(wl, xh, _NT) + _dot(wh, xl, _NT) + _dot(wh, xh, _NT)

    o = [v - IN_OFFS[3] for v in IN_OFFS[3:9]]
    qiT_ref[...] = proj_T(o[0], o[1])
    kT = proj_T(o[1], o[2])
    mu = jnp.mean(kT, axis=0, keepdims=True)
    kc = kT - mu
    var = jnp.mean(kc * kc, axis=0, keepdims=True)
    kiT_ref[...] = kc * lax.rsqrt(var + LN_EPS) * gc_ref[...] + bc_ref[...]
    wh, wl = _split(w_ref[o[1]:o[2], :])
    k_nat = _dot(xl, wh, _NT) + _dot(xh, wl, _NT) + _dot(xh, wh, _NT)
    kcat_ref[...] = _kcat(_layer_norm(k_nat, g_ref[...], b_ref[...]))
    wiT_ref[...] = proj_T(o[2], o[3]) * (IDX_HEADS ** -0.5) * IDX_SCALE
    qbT_ref[...] = proj_T(o[3], o[4])
    kbT = proj_T(o[4], o[5])
    kbT_ref[...] = kbT
    kb16_ref[...] = kbT.T.astype(BF16)


def _project_prompt(x2, wT, g, b, B):
    M, K = x2.shape
    S = M // B
    tm = min(S, PROJ_TM)
    nt = S // tm
    o = IN_OFFS
    x_spec = pl.BlockSpec((tm, K), lambda i: (i, 0))
    full = lambda a: pl.BlockSpec(a.shape, lambda i: (0,) * a.ndim)
    specT = lambda n: pl.BlockSpec((None, n, tm), lambda i: (i // nt, 0, i % nt))
    spec = lambda n: pl.BlockSpec((tm, n), lambda i: (i, 0))
    shapeT = lambda n, d: jax.ShapeDtypeStruct((B, n, S), d)
    shape = lambda n, d: jax.ShapeDtypeStruct((M, n), d)

    w_bulk = jnp.concatenate([wT[o[0]:o[3]], wT[o[8]:o[10]]], axis=0).astype(BF16)
    q_aT, k_aT, k_a16, v_aT, v_aT16, v_bT, v_bT16, gates = pl.pallas_call(
        _proj_bulk_body,
        grid=(M // tm,),
        in_specs=[x_spec, full(w_bulk)],
        out_specs=[specT(D_ATT), specT(D_ATT), spec(D_ATT), specT(D_ATT), specT(D_ATT), specT(D_ATT), specT(D_ATT),
                   spec(2 * D_MODEL)],
        out_shape=[shapeT(D_ATT, BF16), shapeT(D_ATT, F32), shape(D_ATT, BF16), shapeT(D_ATT, F32), shapeT(D_ATT, BF16),
                   shapeT(D_ATT, F32), shapeT(D_ATT, BF16), shape(2 * D_MODEL, F32)],
        compiler_params=_cparams(("parallel",)),
        name="proj_bulk",
    )(x2, w_bulk)

    w_sel = wT[o[3]:o[8]]
    q_iT, k_iT, kcat, w_iT, q_bT, k_bT, k_b16 = pl.pallas_call(
        _proj_select_body,
        grid=(M // tm,),
        in_specs=[x_spec, full(w_sel)] + [pl.BlockSpec((1, IDX_DIM), lambda i: (0, 0))] * 2
                 + [pl.BlockSpec((IDX_DIM, 1), lambda i: (0, 0))] * 2,
        out_specs=[specT(IDX_HEADS * IDX_DIM), specT(IDX_DIM), spec(4 * IDX_DIM), specT(IDX_HEADS),
                   specT(D_ATT), specT(D_ATT), spec(D_ATT)],
        out_shape=[shapeT(IDX_HEADS * IDX_DIM, F32), shapeT(IDX_DIM, F32), shape(4 * IDX_DIM, BF16),
                   shapeT(IDX_HEADS, F32), shapeT(D_ATT, F32), shapeT(D_ATT, F32), shape(D_ATT, BF16)],
        compiler_params=_cparams(("parallel",)),
        name="proj_select",
    )(x2, w_sel, g[None], b[None], g[:, None], b[:, None])

    return dict(q_aT=q_aT, k_aT=k_aT, k_a16=k_a16, v_aT=v_aT, v_aT16=v_aT16, q_iT=q_iT, k_iT=k_iT, kcat=kcat,
                w_iT=w_iT, q_bT=q_bT, k_bT=k_bT, k_b16=k_b16, v_bT=v_bT, v_bT16=v_bT16, gates=gates)


def _heads_major(a, B, S):
    return a.reshape(B, S, N_HEADS, HEAD_DIM).transpose(0, 2, 1, 3)


def _block_diag_rows(q):
    DB, T, H, Dh = q.shape
    eye = jnp.eye(H, dtype=q.dtype)
    return jnp.einsum("bthd,hg->bhtgd", q, eye).reshape(DB, H * T, H * Dh)


def _prompt_mixers(pr, B, S, rb_a, rb_b, T_dsa):
    bt_a = _bias_tiles(rb_a, T_dsa)
    o_a = _dsa_prompt(rb_a[N_BUCKETS - 1], bt_a, pr["q_aT"], pr["k_a16"], pr["v_aT16"],
                      pr["q_iT"], pr["w_iT"], pr["kcat"], T=T_dsa)
    T = MOBA_BLOCK
    nf = S // T
    n_slots = 128 // N_HEADS
    assert nf <= n_slots
    means = _block_means(pr["k_bT"]).reshape(B, N_HEADS, HEAD_DIM, nf)
    eye = jnp.eye(N_HEADS, dtype=F32)
    mbdT = jnp.einsum("bhdj,hg->bjghd", means, eye)
    mbdT = jnp.pad(mbdT, ((0, 0), (0, n_slots - nf), (0, 0), (0, 0), (0, 0))).reshape(B, n_slots * N_HEADS, D_ATT)
    bt_b = _bias_tiles(rb_b, T)
    o_b = _moba_prompt(rb_b[N_BUCKETS - 1], bt_b, pr["q_bT"], pr["k_b16"], pr["v_bT16"], mbdT)
    return o_a, o_b


def _sample_mixers(sm, DB, T, caches, page_table, rb_a, rb_b, li):
    ck_a, cv_a, ck_i, ck_b, cv_b = caches
    new_page = lambda a: jnp.pad(a.reshape(DB, T, -1), ((0, 0), (0, PAGE_SIZE - T), (0, 0))).transpose(0, 2, 1)
    qi_rows = _heads_major(sm["q_i"], DB, T).reshape(DB, IDX_HEADS * T, IDX_DIM)
    wi_rows = sm["w_i"].reshape(DB, T, IDX_HEADS).transpose(0, 2, 1).reshape(DB, IDX_HEADS * T, 1)
    sel_a = _smp_dsa_select(page_table, qi_rows, wi_rows, new_page(sm["k_i"]), ck_i, li)
    rbr_a = jnp.repeat(rb_a.T, T, axis=0)
    rbr_b = jnp.repeat(rb_b.T, T, axis=0)
    qbd_a = _block_diag_rows(sm["q_a"].reshape(DB, T, N_HEADS, HEAD_DIM))
    o_a = _smp_attn(page_table, qbd_a, rbr_a, new_page(sm["k_a"]), new_page(sm["v_a"]), sel_a,
                    ck_a, cv_a, li, mode="dsa")
    qbd_b = _block_diag_rows(sm["q_b"].reshape(DB, T, N_HEADS, HEAD_DIM))
    sel_b = _smp_moba_select(page_table, qbd_b, ck_b, li)
    o_b = _smp_attn(page_table, qbd_b, rbr_b, new_page(sm["k_b"]), new_page(sm["v_b"]), sel_b,
                    ck_b, cv_b, li, mode="moba")
    return o_a.reshape(DB * T, D_ATT), o_b.reshape(DB * T, D_ATT)


def kernel(x_prompt, x_sample, cache_k_a, cache_v_a, cache_kidx, cache_k_b, cache_v_b, page_table, p_prompt, p_sample, rel_bias, w_in, kidx_ln_g, kidx_ln_b, w_branch_a, w_branch_b, w_out, ln1_g, ln1_b, w_router, b_router, w_gate_up, b_gate_up, w_down, b_down, w_ple_gate, w_ple_proj, ln2_g, ln2_b):
    B, S, D = x_prompt.shape
    DB, T, _ = x_sample.shape
    depth = w_in.shape[0]
    n_pool = cache_k_a.shape[1]
    rb_a = rel_bias[:, :N_HEADS]
    rb_b = rel_bias[:, N_HEADS:]
    kv_pages = lambda c: jnp.transpose(c, (0, 1, 3, 4, 2)).reshape(depth, n_pool, D_ATT, PAGE_SIZE)
    caches = (kv_pages(cache_k_a), kv_pages(cache_v_a), jnp.transpose(cache_kidx, (0, 1, 3, 2)),
              kv_pages(cache_k_b), kv_pages(cache_v_b))
    T_dsa = min(256, S)

    xp = x_prompt.reshape(B * S, D)
    xs = x_sample.reshape(DB * T, D)
    rows_p, rows_s = [], []
    for li in range(depth):
        bf = lambda a: a[li].astype(BF16)
        merge_w = (bf(w_branch_a), bf(w_branch_b), bf(w_out), ln1_g[li][None], ln1_b[li][None],
                   w_router[li], b_router[li][None], bf(w_ple_gate), bf(w_ple_proj))
        moe_w = (bf(w_gate_up), b_gate_up[li], bf(w_down), b_down[li], ln2_g[li][None], ln2_b[li][None])

        w_inT = jnp.swapaxes(w_in[li], 0, 1)
        pr = _project_prompt(xp, w_inT, kidx_ln_g[li], kidx_ln_b[li], B)
        o_a, o_b = _prompt_mixers(pr, B, S, rb_a, rb_b, T_dsa)
        x1b, comb, res = _merge(xp, o_a, o_b, pr["gates"], p_prompt[li].reshape(B * S, -1), *merge_w)
        xp = _moe(x1b, res, comb, *moe_w)
        rows_p.append(pr)

        sm = _project(xs, w_inT, kidx_ln_g[li], kidx_ln_b[li])
        o_a, o_b = _sample_mixers(sm, DB, T, caches, page_table, rb_a, rb_b, li)
        x1b, comb, res = _merge(xs, o_a, o_b, sm["gates"], p_sample[li].reshape(DB * T, -1), *merge_w)
        xs = _moe(x1b, res, comb, *moe_w)
        rows_s.append(sm)

    def stack(rows, name, lead, tail):
        return jnp.stack([r[name].reshape(lead + tail) for r in rows])

    def stack_T(name, feat):
        a = jnp.stack([r[name] for r in rows_p])
        return jnp.moveaxis(a.reshape((depth, B) + feat + (S,)), -1, 2)

    hd = (N_HEADS, HEAD_DIM)
    outs = [xp.reshape(B, S, D), xs.reshape(DB, T, D)]
    outs += [stack_T("k_aT", hd), stack_T("v_aT", hd), stack_T("k_iT", (IDX_DIM,)),
             stack_T("k_bT", hd), stack_T("v_bT", hd)]
    lead = (DB, T)
    outs += [stack(rows_s, "k_a", lead, hd), stack(rows_s, "v_a", lead, hd), stack(rows_s, "k_i", lead, (IDX_DIM,)),
             stack(rows_s, "k_b", lead, hd), stack(rows_s, "v_b", lead, hd)]
    return tuple(outs)
```

```python
import functools
import math

import numpy as np
import jax
import jax.numpy as jnp
from jax import lax
from jax.experimental import pallas as pl
from jax.experimental.pallas import tpu as pltpu

F32 = jnp.float32
BF16 = jnp.bfloat16
I32 = jnp.int32

D_MODEL = 1024
HEAD_DIM = 64
N_HEADS = 8
D_ATT = N_HEADS * HEAD_DIM
IDX_HEADS = 8
IDX_DIM = 64
IDX_TOPK = 256
MOBA_BLOCK = 256
MOBA_TOPK = 3
N_BUCKETS = 32
MAX_DISTANCE = 128
N_EXPERTS = 32
TOP_K = 4
D_EXPERT = D_MODEL
SWIGLU_LIMIT = 7.0
SWIGLU_ALPHA = 1.702
PLE_DIM = 256
LN_EPS = 1e-5
PAGE_SIZE = 128
DEPTH = 2
ALPHA_DN = (2 * DEPTH) ** 0.25
ATTN_SCALE = HEAD_DIM ** -0.5
LOG2E = math.log2(math.e)
PROMPT_Q_SCALE = ATTN_SCALE * LOG2E
IDX_SCALE = IDX_DIM ** -0.5
IN_SIZES = (D_ATT, D_ATT, D_ATT, IDX_HEADS * IDX_DIM, IDX_DIM, IDX_HEADS, D_ATT, D_ATT, D_ATT, 2 * D_MODEL)
IN_OFFS = tuple(int(v) for v in np.cumsum((0,) + IN_SIZES))

INT_MIN = -(2 ** 31)
NEG_BIG = -1e30
VMEM_LIMIT_BYTES = 56 * 1024 * 1024


def _bucket_of(d):
    d = max(d, 0)
    max_exact = N_BUCKETS // 2
    if d < max_exact:
        return d
    ratio = math.log(d / max_exact) / math.log(MAX_DISTANCE / max_exact)
    return min(max_exact + int(ratio * (N_BUCKETS - max_exact)), N_BUCKETS - 1)


_BUCKET_HI = tuple(max(d for d in range(4 * MAX_DISTANCE) if _bucket_of(d) == b) for b in range(N_BUCKETS - 1))
FAR_DIST = _BUCKET_HI[-1] + 1


def _cparams(sem):
    return pltpu.CompilerParams(dimension_semantics=sem, vmem_limit_bytes=VMEM_LIMIT_BYTES)


def _split(x):
    hi = x.astype(BF16)
    lo = (x - hi.astype(F32)).astype(BF16)
    return hi, lo


_NN = (((1,), (0,)), ((), ()))
_NT = (((1,), (1,)), ((), ()))


def _dot(a, b, dims=_NN):
    return lax.dot_general(a, b, dims, preferred_element_type=F32)


def _dot3(a, b, dims=_NN):
    ah, al = _split(a)
    bh, bl = _split(b)
    return _dot(al, bh, dims) + _dot(ah, bl, dims) + _dot(ah, bh, dims)


def _layer_norm(x, g, b):
    mu = jnp.mean(x, axis=-1, keepdims=True)
    xc = x - mu
    var = jnp.mean(xc * xc, axis=-1, keepdims=True)
    return xc * lax.rsqrt(var + LN_EPS) * g + b


def _order_key(s):
    s = jnp.where(s == 0.0, 0.0, s)
    u = pltpu.bitcast(s, I32)
    return u ^ (jnp.right_shift(u, 31) & 0x7FFFFFFF)


def _bias_chain(d, rb_of):
    val = rb_of(N_BUCKETS - 1)
    for b in range(N_BUCKETS - 2, -1, -1):
        val = jnp.where(d <= _BUCKET_HI[b], rb_of(b), val)
    return val


def _mm_body(x_ref, wT_ref, o_ref, *, passes, act):
    x = x_ref[...]
    wT = wT_ref[...]
    y = _dot(x.astype(BF16), wT.astype(BF16), _NT) if passes == 1 else _dot3(x, wT, _NT)
    if act == "sigmoid":
        y = jax.nn.sigmoid(y)
    o_ref[...] = y.astype(o_ref.dtype)


def _mm(x, wT, *, passes=1, act=None, out_dtype=F32):
    M, K = x.shape
    N = wT.shape[0]
    tm = min(M, 512)
    tn = min(N, 512)
    return pl.pallas_call(
        functools.partial(_mm_body, passes=passes, act=act),
        grid=(M // tm, N // tn),
        in_specs=[pl.BlockSpec((tm, K), lambda i, j: (i, 0)),
                  pl.BlockSpec((tn, K), lambda i, j: (j, 0))],
        out_specs=pl.BlockSpec((tm, tn), lambda i, j: (i, j)),
        out_shape=jax.ShapeDtypeStruct((M, N), out_dtype),
        compiler_params=_cparams(("parallel", "parallel")),
        name="proj_mm",
    )(x, wT)


def _kcat(k):
    kh, kl = _split(k)
    return jnp.concatenate([kh, kl, kh, jnp.zeros_like(kh)], axis=-1)


def _proj_idx_body(x_ref, wqT_ref, wkT_ref, wwT_ref, g_ref, b_ref, qi_ref, ki_ref, wi_ref):
    x = x_ref[...]
    qi_ref[...] = _dot3(x, wqT_ref[...], _NT)
    ki_ref[...] = _layer_norm(_dot3(x, wkT_ref[...], _NT), g_ref[...], b_ref[...])
    wi_ref[...] = _dot3(x, wwT_ref[...], _NT) * (IDX_HEADS ** -0.5)


def _proj_idx(x, wqT, wkT, wwT, g, b):
    M, K = x.shape
    tm = min(M, 512)
    full = lambda a: pl.BlockSpec(a.shape, lambda i: (0,) * a.ndim)
    row = lambda n: pl.BlockSpec((tm, n), lambda i: (i, 0))
    return pl.pallas_call(
        _proj_idx_body,
        grid=(M // tm,),
        in_specs=[row(K), full(wqT), full(wkT), full(wwT), full(g), full(b)],
        out_specs=[row(IDX_HEADS * IDX_DIM), row(IDX_DIM), row(IDX_HEADS)],
        out_shape=[jax.ShapeDtypeStruct((M, IDX_HEADS * IDX_DIM), F32),
                   jax.ShapeDtypeStruct((M, IDX_DIM), F32),
                   jax.ShapeDtypeStruct((M, IDX_HEADS), F32)],
        compiler_params=_cparams(("parallel",)),
        name="proj_idx",
    )(x, wqT, wkT, wwT, g, b)


def _bias_tiles_body(rb_ref, o_ref, *, T):
    rel = pl.program_id(0)
    h = pl.program_id(1)
    key = lax.broadcasted_iota(I32, (T, T), 0)
    qry = lax.broadcasted_iota(I32, (T, T), 1)
    d = qry - key + rel * T
    o_ref[...] = _bias_chain(d, lambda b: rb_ref[b, h]) * LOG2E


def _bias_tiles(rb, T):
    H = rb.shape[1]
    return pl.pallas_call(
        functools.partial(_bias_tiles_body, T=T),
        grid=(2, H),
        in_specs=[pl.BlockSpec(memory_space=pltpu.SMEM)],
        out_specs=pl.BlockSpec((None, None, T, T), lambda r, h: (r, h, 0, 0)),
        out_shape=jax.ShapeDtypeStruct((2, H, T, T), F32),
        compiler_params=_cparams(("parallel", "parallel")),
        name="bias_tiles",
    )(rb)


ATTN_HEAD_GROUP = 8


def _attn_update(tile_of, shift_of, vT_of, m_ref, l_ref, acc_ref):
    for h0 in range(0, N_HEADS, ATTN_HEAD_GROUP):
        heads = range(h0, h0 + ATTN_HEAD_GROUP)
        s = {h: tile_of(h) for h in heads}
        shift = {h: shift_of(h) for h in heads}
        m_old = {h: m_ref[h] for h in heads}
        m_tile = {h: jnp.max(s[h], axis=0, keepdims=True) for h in heads}
        m_new = {h: jnp.maximum(m_old[h], m_tile[h] + shift[h]) for h in heads}
        ref = {h: jnp.maximum(m_new[h] - shift[h], m_tile[h]) for h in heads}
        p = {h: jnp.exp2(s[h] - ref[h]) for h in heads}
        alpha = {h: jnp.exp2(m_old[h] - m_new[h]) for h in heads}
        for h in heads:
            l_ref[h] = alpha[h] * l_ref[h] + jnp.sum(p[h], axis=0, keepdims=True)
            m_ref[h] = m_new[h]
        pv = {h: _dot(vT_of(h), p[h].astype(BF16)) for h in heads}
        for h in heads:
            acc_ref[h] = alpha[h] * acc_ref[h] + pv[h]


def _attn_init(m_ref, l_ref, acc_ref):
    m_ref[...] = jnp.full(m_ref.shape, NEG_BIG, F32)
    l_ref[...] = jnp.zeros(l_ref.shape, F32)
    acc_ref[...] = jnp.zeros(acc_ref.shape, F32)


def _attn_finish(o_ref, l_ref, acc_ref):
    outT = jnp.concatenate([acc_ref[h] / l_ref[h] for h in range(N_HEADS)], axis=0)
    o_ref[...] = outT.T


def _padded_heads(qT_of):
    out = []
    for h in range(N_HEADS):
        q = qT_of(h)
        z = jnp.zeros_like(q)
        out.append(jnp.concatenate([q, z] if h % 2 == 0 else [z, q], axis=0))
    return out


def _head_pair(k, h):
    lo = (h // 2) * 2 * HEAD_DIM
    return k[:, lo:lo + 2 * HEAD_DIM]


def _dsa_prompt_body(rbf_ref, bt_ref, qa_ref, k_ref, vT_ref, qi_ref, wi_ref, kcat_ref, o_ref,
                     key_ref, m_ref, l_ref, acc_ref, *, T, n_top, idx_bits):
    i = pl.program_id(1)
    H = N_HEADS
    kpos = lax.broadcasted_iota(I32, (T, T), 0)
    qpos = lax.broadcasted_iota(I32, (T, T), 1)

    wi = wi_ref[...]
    qcat = []
    for h in range(IDX_HEADS):
        qh, ql = _split(qi_ref[h * IDX_DIM:(h + 1) * IDX_DIM, :])
        qcat.append(jnp.concatenate([qh, qh, ql, jnp.zeros_like(qh)], axis=0))

    def score_chunk(kc, carry):
        kcat = kcat_ref[pl.ds(pl.multiple_of(kc * T, T), T), :]
        s = jnp.zeros((T, T), F32)
        for h in range(IDX_HEADS):
            s = s + jnp.maximum(_dot(kcat, qcat[h]), 0.0) * wi[h:h + 1, :]
        key = _order_key(s)
        key_ref[kc] = jnp.where((kc == i) & (kpos > qpos), INT_MIN, key)
        return carry

    lax.fori_loop(0, i + 1, score_chunk, 0)

    def count(pred):
        def body(kc, acc):
            hit = jnp.where(pred(key_ref[kc], kc * T + kpos), 1.0, 0.0)
            return acc + jnp.sum(hit.reshape(T // 8, 8, T), axis=0)
        acc = lax.fori_loop(0, i + 1, body, jnp.zeros((8, T), F32))
        return jnp.sum(acc, axis=0, keepdims=True)

    def thr_bit(bi, prefix):
        cand = prefix | jnp.left_shift(jnp.int32(1), 31 - bi)
        cs = cand ^ INT_MIN
        return jnp.where(count(lambda k, g: k >= cs) >= n_top, cand, prefix)

    thr = lax.fori_loop(0, 32, thr_bit, jnp.zeros((1, T), I32)) ^ INT_MIN
    need = n_top - count(lambda k, g: k > thr)
    n_tie = count(lambda k, g: k == thr)

    def tie_search():
        def bit(bi, j0):
            cand = j0 | jnp.left_shift(jnp.int32(1), idx_bits - 1 - bi)
            c = count(lambda k, g: (k == thr) & (g < cand))
            return jnp.where(c < need, cand, j0)
        return lax.fori_loop(0, idx_bits, bit, jnp.zeros((1, T), I32))

    any_excess = jnp.max(n_tie - need) > 0.0
    jcut = lax.cond(any_excess, tie_search, lambda: jnp.full((1, T), 2 ** 30, I32))

    _attn_init(m_ref, l_ref, acc_ref)
    q = _padded_heads(lambda h: qa_ref[h * HEAD_DIM:(h + 1) * HEAD_DIM, :])

    def attn_chunk(kc, near):
        kk = k_ref[pl.ds(pl.multiple_of(kc * T, T), T), :]
        keys = key_ref[kc]
        g = kc * T + kpos
        sel = (keys > thr) | ((keys == thr) & (g <= jcut))
        if near:
            sel = sel & (g <= i * T + qpos)
        pen = jnp.where(sel, 0.0, NEG_BIG)
        _attn_update(
            lambda h: _dot(_head_pair(kk, h), q[h]) + ((bt_ref[i - kc, h] + pen) if near else pen),
            lambda h: 0.0 if near else rbf_ref[h] * LOG2E,
            lambda h: vT_ref[h * HEAD_DIM:(h + 1) * HEAD_DIM, pl.ds(pl.multiple_of(kc * T, T), T)],
            m_ref, l_ref, acc_ref)

    n_far = jnp.maximum(i - 1, 0)
    lax.fori_loop(0, n_far, lambda kc, c: (attn_chunk(kc, False), c)[1], 0)
    lax.fori_loop(n_far, i + 1, lambda kc, c: (attn_chunk(kc, True), c)[1], 0)
    _attn_finish(o_ref, l_ref, acc_ref)


def _attn_scratch(T):
    return [pltpu.VMEM((N_HEADS, 1, T), F32), pltpu.VMEM((N_HEADS, 1, T), F32),
            pltpu.VMEM((N_HEADS, HEAD_DIM, T), F32)]


def _dsa_prompt(rb_far, btiles, qaT, k16, vT, qiT, wiT, kcat, *, T):
    B, _, S = qaT.shape
    M = B * S
    H = N_HEADS
    nc = S // T
    n_top = min(IDX_TOPK, S // 4)
    body = functools.partial(_dsa_prompt_body, T=T, n_top=n_top, idx_bits=max(1, (S - 1).bit_length()))
    tile = lambda rows: pl.BlockSpec((None, rows, T), lambda b, i: (b, 0, i))
    return pl.pallas_call(
        body,
        grid=(B, nc),
        in_specs=[pl.BlockSpec(memory_space=pltpu.SMEM),
                  pl.BlockSpec((2, H, T, T), lambda b, i: (0, 0, 0, 0)),
                  tile(D_ATT),
                  pl.BlockSpec((S, D_ATT), lambda b, i: (b, 0)),
                  pl.BlockSpec((None, D_ATT, S), lambda b, i: (b, 0, 0)),
                  tile(IDX_HEADS * IDX_DIM),
                  tile(IDX_HEADS),
                  pl.BlockSpec((S, 4 * IDX_DIM), lambda b, i: (b, 0))],
        out_specs=pl.BlockSpec((T, D_ATT), lambda b, i: (b * nc + i, 0)),
        out_shape=jax.ShapeDtypeStruct((M, D_ATT), F32),
        scratch_shapes=[pltpu.VMEM((nc, T, T), I32)] + _attn_scratch(T),
        compiler_params=_cparams(("parallel", "arbitrary")),
        name="dsa_prompt",
    )(rb_far, btiles, qaT, k16, vT, qiT, wiT, kcat)


def _block_means_body(kT_ref, o_ref):
    nf = o_ref.shape[-1]
    lane = lax.broadcasted_iota(I32, o_ref.shape, 1)
    means = jnp.zeros(o_ref.shape, F32)
    for j in range(nf):
        col = jnp.sum(kT_ref[:, j * MOBA_BLOCK:(j + 1) * MOBA_BLOCK], axis=1, keepdims=True) * (1.0 / MOBA_BLOCK)
        means = jnp.where(lane == j, col, means)
    o_ref[...] = means


def _block_means(kT):
    B, D, L = kT.shape
    nf = L // MOBA_BLOCK
    return pl.pallas_call(
        _block_means_body,
        grid=(B,),
        in_specs=[pl.BlockSpec((None, D, L), lambda b: (b, 0, 0))],
        out_specs=pl.BlockSpec((None, D, nf), lambda b: (b, 0, 0)),
        out_shape=jax.ShapeDtypeStruct((B, D, nf), F32),
        compiler_params=_cparams(("parallel",)),
        name="block_means",
    )(kT)


def _moba_prompt_body(rbf_ref, bt_ref, qbT_ref, k_ref, vT_ref, mbdT_ref, o_ref,
                      ch_ref, m_ref, l_ref, acc_ref, *, T, n_sel, n_slots):
    i = pl.program_id(1)
    H = N_HEADS
    W = n_slots * H
    kpos = lax.broadcasted_iota(I32, (T, T), 0)
    qpos = lax.broadcasted_iota(I32, (T, T), 1)
    qT = qbT_ref[...]

    g = _dot3(mbdT_ref[...], qT)
    blk = jnp.right_shift(lax.broadcasted_iota(I32, (W, T), 0), int(math.log2(H)))
    g = jnp.where(blk < i, g, -jnp.inf)
    rank = jnp.zeros((W, T), F32)
    for r in range(1, n_slots):
        other = pltpu.roll(g, r * H, axis=0)
        rank = rank + jnp.where(blk >= r, jnp.where(other >= g, 1.0, 0.0), jnp.where(other > g, 1.0, 0.0))
    ch_ref[...] = jnp.where((rank < n_sel) & (blk < i), 0.0, NEG_BIG)

    _attn_init(m_ref, l_ref, acc_ref)
    q = _padded_heads(lambda h: (qT[h * HEAD_DIM:(h + 1) * HEAD_DIM, :] * PROMPT_Q_SCALE).astype(BF16))
    causal_pen = jnp.where(kpos <= qpos, 0.0, NEG_BIG)

    def attn_chunk(kc, near):
        kk = k_ref[pl.ds(pl.multiple_of(kc * T, T), T), :]

        picked = lambda h: ch_ref[pl.ds(kc * H + h, 1), :]

        def tile_of(h):
            s = _dot(_head_pair(kk, h), q[h])
            return s + (bt_ref[i - kc, h] + jnp.where(kc == i, causal_pen, picked(h))) if near else s

        _attn_update(
            tile_of,
            lambda h: 0.0 if near else picked(h) + rbf_ref[h] * LOG2E,
            lambda h: vT_ref[h * HEAD_DIM:(h + 1) * HEAD_DIM, pl.ds(pl.multiple_of(kc * T, T), T)],
            m_ref, l_ref, acc_ref)

    n_far = jnp.maximum(i - 1, 0)
    lax.fori_loop(0, n_far, lambda kc, c: (attn_chunk(kc, False), c)[1], 0)
    lax.fori_loop(n_far, i + 1, lambda kc, c: (attn_chunk(kc, True), c)[1], 0)
    _attn_finish(o_ref, l_ref, acc_ref)


def _moba_prompt(rb_far, btiles, qbT, k16, vT, mbdT):
    B, _, S = qbT.shape
    M = B * S
    H = N_HEADS
    T = MOBA_BLOCK
    nc = S // T
    W = mbdT.shape[1]
    body = functools.partial(_moba_prompt_body, T=T, n_sel=min(MOBA_TOPK, nc), n_slots=W // H)
    return pl.pallas_call(
        body,
        grid=(B, nc),
        in_specs=[pl.BlockSpec(memory_space=pltpu.SMEM),
                  pl.BlockSpec((2, H, T, T), lambda b, i: (0, 0, 0, 0)),
                  pl.BlockSpec((None, D_ATT, T), lambda b, i: (b, 0, i)),
                  pl.BlockSpec((S, D_ATT), lambda b, i: (b, 0)),
                  pl.BlockSpec((None, D_ATT, S), lambda b, i: (b, 0, 0)),
                  pl.BlockSpec((None, W, D_ATT), lambda b, i: (b, 0, 0))],
        out_specs=pl.BlockSpec((T, D_ATT), lambda b, i: (b * nc + i, 0)),
        out_shape=jax.ShapeDtypeStruct((M, D_ATT), F32),
        scratch_shapes=[pltpu.VMEM((W, T), F32)] + _attn_scratch(T),
        compiler_params=_cparams(("parallel", "arbitrary")),
        name="moba_prompt",
    )(rb_far, btiles, qbT, k16, vT, mbdT)


def _merge_body(x_ref, oa_ref, ob_ref, g_ref, p_ref, wba_ref, wbb_ref, wo_ref, g1_ref, b1_ref,
                wr_ref, br_ref, wpg_ref, wpp_ref, x1b_ref, comb_ref, res_ref):
    bra = _dot(oa_ref[...].astype(BF16), wba_ref[...])
    brb = _dot(ob_ref[...].astype(BF16), wbb_ref[...])
    gates = g_ref[...]
    mix = gates[:, :D_MODEL] * bra + gates[:, D_MODEL:] * brb
    y = _dot(mix.astype(BF16), wo_ref[...])
    x1 = _layer_norm(ALPHA_DN * x_ref[...] + y, g1_ref[...], b1_ref[...])
    x1b = x1.astype(BF16)
    x1b_ref[...] = x1b

    logits = _dot3(x1, wr_ref[...]) + br_ref[...]
    lane = lax.broadcasted_iota(I32, logits.shape, 1)
    work = logits
    kept = jnp.zeros(logits.shape, jnp.bool_)
    for _ in range(TOP_K):
        mx = jnp.max(work, axis=-1, keepdims=True)
        first = jnp.min(jnp.where(work == mx, lane, N_EXPERTS), axis=-1, keepdims=True)
        hit = lane == first
        kept = kept | hit
        work = jnp.where(hit, -jnp.inf, work)
    top = jnp.max(logits, axis=-1, keepdims=True)
    e = jnp.where(kept, jnp.exp(logits - top), 0.0)
    comb_ref[...] = e / jnp.sum(e, axis=-1, keepdims=True)

    ple = jax.nn.sigmoid(_dot(x1b, wpg_ref[...])) * _dot(p_ref[...].astype(BF16), wpp_ref[...])
    res_ref[...] = ALPHA_DN * x1 + ple


def _merge(x, oa, ob, gates, p, wba, wbb, wo, g1, b1, wr, br, wpg, wpp):
    M = x.shape[0]
    tm = min(M, 512)
    full = lambda a: pl.BlockSpec(a.shape, lambda i: (0,) * a.ndim)
    row = lambda n: pl.BlockSpec((tm, n), lambda i: (i, 0))
    return pl.pallas_call(
        _merge_body,
        grid=(M // tm,),
        in_specs=[row(D_MODEL), row(D_ATT), row(D_ATT), row(2 * D_MODEL), row(PLE_DIM),
                  full(wba), full(wbb), full(wo), full(g1), full(b1), full(wr), full(br), full(wpg), full(wpp)],
        out_specs=[row(D_MODEL), row(N_EXPERTS), row(D_MODEL)],
        out_shape=[jax.ShapeDtypeStruct((M, D_MODEL), BF16),
                   jax.ShapeDtypeStruct((M, N_EXPERTS), F32),
                   jax.ShapeDtypeStruct((M, D_MODEL), F32)],
        compiler_params=_cparams(("parallel",)),
        name="merge",
    )(x, oa, ob, gates, p, wba, wbb, wo, g1, b1, wr, br, wpg, wpp)


MOE_TM = 1024
MOE_RB = 160
MOE_CB = 256


def _moe_body(xb_ref, res_ref, comb_ref, wgu_ref, bgu_ref, wdn_ref, bdn_ref, g2_ref, b2_ref,
              o_ref, rank_ref, rankT_ref, acc_ref):
    e = pl.program_id(1)
    TM, E = comb_ref.shape
    RB = MOE_RB

    @pl.when(e == 0)
    def _():
        routed = jnp.where(comb_ref[...] != 0.0, 1.0, 0.0)
        eye = jnp.where(lax.broadcasted_iota(I32, (E, E), 0) == lax.broadcasted_iota(I32, (E, E), 1),
                        1.0, 0.0).astype(BF16)
        routedT = _dot(eye, routed.astype(BF16), _NT)
        CB = min(MOE_CB, TM)
        r_i = lax.broadcasted_iota(I32, (CB, CB), 0)
        c_i = lax.broadcasted_iota(I32, (CB, CB), 1)
        before = jnp.where(c_i < r_i, 1.0, 0.0).astype(BF16)
        beforeT = jnp.where(r_i < c_i, 1.0, 0.0).astype(BF16)
        off = jnp.zeros((1, E), F32)
        offT = jnp.zeros((E, 1), F32)
        for blk in range(TM // CB):
            rb = routed[blk * CB:(blk + 1) * CB]
            rbT = routedT[:, blk * CB:(blk + 1) * CB]
            rank = _dot(before, rb.astype(BF16)) + off
            rankT = _dot(rbT.astype(BF16), beforeT) + offT
            rank_ref[blk * CB:(blk + 1) * CB, :] = jnp.where(rb > 0.5, rank, -1.0).astype(I32)
            rankT_ref[:, blk * CB:(blk + 1) * CB] = jnp.where(rbT > 0.5, rankT, -1.0).astype(I32)
            off = off + jnp.sum(rb, axis=0, keepdims=True)
            offT = offT + jnp.sum(rbT, axis=1, keepdims=True)
        acc_ref[...] = jnp.zeros(acc_ref.shape, F32)

    lane = lax.broadcasted_iota(I32, (TM, E), 1)
    gate_col = jnp.sum(jnp.where(lane == e, comb_ref[...], 0.0), axis=-1, keepdims=True)
    rank_col = jnp.sum(jnp.where(lane == e, rank_ref[...], 0), axis=-1, keepdims=True)
    rank_row = rankT_ref[pl.ds(e, 1), :]
    n_routed = jnp.sum(jnp.where(rank_row >= 0, 1, 0))
    n_pass = (n_routed + (RB - 1)) // RB
    slot_rows = lax.broadcasted_iota(I32, (RB, TM), 0)
    slot_lanes = lax.broadcasted_iota(I32, (TM, RB), 1)

    def one_pass(pi, carry):
        base = pi * RB
        pick = jnp.where(rank_row == slot_rows + base, 1.0, 0.0).astype(BF16)
        xg = _dot(pick, xb_ref[...]).astype(BF16)
        gu = _dot(xg, wgu_ref[...]) + bgu_ref[...]
        gt = jnp.minimum(gu[:, :D_EXPERT], SWIGLU_LIMIT)
        up = jnp.clip(gu[:, D_EXPERT:], -SWIGLU_LIMIT, SWIGLU_LIMIT)
        hid = (up + 1.0) * gt * jax.nn.sigmoid(SWIGLU_ALPHA * gt)
        down = _dot(hid.astype(BF16), wdn_ref[...])
        place = jnp.where(rank_col == slot_lanes + base, 1.0, 0.0).astype(BF16)
        acc_ref[...] += _dot(place, down.astype(BF16)) * gate_col
        return carry

    lax.fori_loop(0, n_pass, one_pass, 0)

    @pl.when(e == pl.num_programs(1) - 1)
    def _():
        y = acc_ref[...] + _dot3(comb_ref[...], bdn_ref[...])
        o_ref[...] = _layer_norm(res_ref[...] + y, g2_ref[...], b2_ref[...])


def _moe(xb, res, comb, w_gu, b_gu, w_dn, b_dn, g2, b2):
    M = xb.shape[0]
    tm = min(M, MOE_TM)
    E = w_gu.shape[0]
    b_gu3 = b_gu.reshape(E, 1, 2 * D_EXPERT)
    row = lambda n: pl.BlockSpec((tm, n), lambda i, e: (i, 0))
    return pl.pallas_call(
        _moe_body,
        grid=(M // tm, E),
        in_specs=[row(D_MODEL), row(D_MODEL), row(E),
                  pl.BlockSpec((None, D_MODEL, 2 * D_EXPERT), lambda i, e: (e, 0, 0)),
                  pl.BlockSpec((None, 1, 2 * D_EXPERT), lambda i, e: (e, 0, 0)),
                  pl.BlockSpec((None, D_EXPERT, D_MODEL), lambda i, e: (e, 0, 0)),
                  pl.BlockSpec((E, D_MODEL), lambda i, e: (0, 0)),
                  pl.BlockSpec((1, D_MODEL), lambda i, e: (0, 0)),
                  pl.BlockSpec((1, D_MODEL), lambda i, e: (0, 0))],
        out_specs=row(D_MODEL),
        out_shape=jax.ShapeDtypeStruct((M, D_MODEL), F32),
        scratch_shapes=[pltpu.VMEM((tm, E), I32), pltpu.VMEM((E, tm), I32), pltpu.VMEM((tm, D_MODEL), F32)],
        compiler_params=_cparams(("parallel", "arbitrary")),
        name="moe",
    )(xb, res, comb, w_gu, b_gu3, w_dn, b_dn, g2, b2)


SEL_PAGES = 16
ATT_PAGES = 16


def _page_specs(n, rows, width, li):
    def spec(u):
        return pl.BlockSpec((None, None, rows, width),
                            lambda b, c, pt: (li, pt[b, c * n + u], 0, 0))
    return [spec(u) for u in range(n)]


def _smp_dsa_select_body(pt_ref, qi_ref, wi_ref, knew_ref, *rest, n_pages, n_top, idx_bits):
    pages = rest[:SEL_PAGES]
    sel_ref = rest[SEL_PAGES]
    key_ref = rest[SEL_PAGES + 1]
    c = pl.program_id(1)
    T = qi_ref.shape[0] // IDX_HEADS
    qi = qi_ref[...]
    wi = wi_ref[...] * IDX_SCALE

    def page_score(kpT):
        s = jnp.maximum(_dot3(qi, kpT), 0.0) * wi
        return jnp.sum(s.reshape(IDX_HEADS, T, kpT.shape[1]), axis=0)

    keys = _order_key(page_score(jnp.concatenate([pages[u][...] for u in range(SEL_PAGES)], axis=1)))
    for u in range(SEL_PAGES):
        key_ref[c * SEL_PAGES + u] = keys[:, u * PAGE_SIZE:(u + 1) * PAGE_SIZE]

    @pl.when(c == pl.num_programs(1) - 1)
    def _():
        qrow = lax.broadcasted_iota(I32, (T, PAGE_SIZE), 0)
        lane = lax.broadcasted_iota(I32, (T, PAGE_SIZE), 1)
        key_ref[n_pages] = jnp.where(lane <= qrow, _order_key(page_score(knew_ref[...])), INT_MIN)
        keys = key_ref[...]
        shape = keys.shape
        gidx = lax.broadcasted_iota(I32, shape, 0) * PAGE_SIZE + lax.broadcasted_iota(I32, shape, 2)
        valid = (lax.broadcasted_iota(I32, shape, 0) < n_pages) | \
                (lax.broadcasted_iota(I32, shape, 2) <= lax.broadcasted_iota(I32, shape, 1))

        def count(hit):
            per_lane = jnp.sum(hit.astype(I32), axis=0)
            return jnp.sum(per_lane, axis=-1, keepdims=True)[None]

        def thr_bit(bi, prefix):
            cand = prefix | jnp.left_shift(jnp.int32(1), 31 - bi)
            cs = cand ^ INT_MIN
            return jnp.where(count(keys >= cs) >= n_top, cand, prefix)

        thr = lax.fori_loop(0, 32, thr_bit, jnp.zeros((1, T, 1), I32)) ^ INT_MIN
        need = n_top - count(keys > thr)

        def tie_bit(bi, j0):
            cand = j0 | jnp.left_shift(jnp.int32(1), idx_bits - 1 - bi)
            return jnp.where(count((keys == thr) & (gidx < cand)) < need, cand, j0)

        jcut = lax.fori_loop(0, idx_bits, tie_bit, jnp.zeros((1, T, 1), I32))
        sel = ((keys > thr) | ((keys == thr) & (gidx <= jcut))) & valid
        sel_ref[...] = sel.astype(F32)


def _smp_dsa_select(page_table, qi_rows, wi_rows, knew, cache_kidx, li):
    DB, P = page_table.shape
    T = qi_rows.shape[1] // IDX_HEADS
    L = P * PAGE_SIZE + T
    body = functools.partial(_smp_dsa_select_body, n_pages=P, n_top=min(IDX_TOPK, L // 4),
                             idx_bits=max(1, ((P + 1) * PAGE_SIZE - 1).bit_length()))
    grid_spec = pltpu.PrefetchScalarGridSpec(
        num_scalar_prefetch=1,
        grid=(DB, P // SEL_PAGES),
        in_specs=[pl.BlockSpec((None, IDX_HEADS * T, IDX_DIM), lambda b, c, pt: (b, 0, 0)),
                  pl.BlockSpec((None, IDX_HEADS * T, 1), lambda b, c, pt: (b, 0, 0)),
                  pl.BlockSpec((None, IDX_DIM, PAGE_SIZE), lambda b, c, pt: (b, 0, 0))]
                 + _page_specs(SEL_PAGES, IDX_DIM, PAGE_SIZE, li),
        out_specs=pl.BlockSpec((None, P + 1, T, PAGE_SIZE), lambda b, c, pt: (b, 0, 0, 0)),
        scratch_shapes=[pltpu.VMEM((P + 1, T, PAGE_SIZE), I32)],
    )
    return pl.pallas_call(
        body,
        grid_spec=grid_spec,
        out_shape=jax.ShapeDtypeStruct((DB, P + 1, T, PAGE_SIZE), F32),
        compiler_params=_cparams(("parallel", "arbitrary")),
        name="smp_dsa_select",
    )(page_table, qi_rows, wi_rows, knew, *([cache_kidx] * SEL_PAGES))


def _smp_moba_select_body(pt_ref, qbd_ref, *rest, n_blocks, n_sel):
    pages = rest[:SEL_PAGES]
    sel_ref = rest[SEL_PAGES]
    mean_ref = rest[SEL_PAGES + 1]
    c = pl.program_id(1)
    per_step = SEL_PAGES * PAGE_SIZE // MOBA_BLOCK
    per_block = MOBA_BLOCK // PAGE_SIZE
    W = mean_ref.shape[1]
    lane = lax.broadcasted_iota(I32, mean_ref.shape, 1)

    @pl.when(c == 0)
    def _():
        mean_ref[...] = jnp.zeros(mean_ref.shape, F32)

    means = mean_ref[...]
    for j in range(per_step):
        tot = pages[j * per_block][...]
        for u in range(1, per_block):
            tot = tot + pages[j * per_block + u][...]
        col = jnp.sum(tot, axis=1, keepdims=True) * (1.0 / MOBA_BLOCK)
        means = jnp.where(lane == c * per_step + j, col, means)
    mean_ref[...] = means

    @pl.when(c == pl.num_programs(1) - 1)
    def _():
        g = _dot3(qbd_ref[...], means)
        blk = lax.broadcasted_iota(I32, g.shape, 1)
        g = jnp.where(blk < n_blocks, g, -jnp.inf)
        rank = jnp.zeros(g.shape, F32)
        for r in range(1, n_blocks):
            lower = pltpu.roll(g, r, axis=1)
            upper = pltpu.roll(g, W - r, axis=1)
            rank = rank + jnp.where(lower >= g, 1.0, 0.0) + jnp.where(upper > g, 1.0, 0.0)
        sel_ref[...] = jnp.where((rank < n_sel) & (blk < n_blocks), 1.0, 0.0)


def _smp_moba_select(page_table, qbd, cache_kT, li):
    DB, P = page_table.shape
    n_blocks = P * PAGE_SIZE // MOBA_BLOCK
    n_steps = P // SEL_PAGES
    W = 128
    assert n_blocks < W
    R = qbd.shape[1]
    body = functools.partial(_smp_moba_select_body, n_blocks=n_blocks, n_sel=min(MOBA_TOPK, n_blocks))
    grid_spec = pltpu.PrefetchScalarGridSpec(
        num_scalar_prefetch=1,
        grid=(DB, n_steps),
        in_specs=[pl.BlockSpec((None, R, D_ATT), lambda b, c, pt: (b, 0, 0))]
                 + _page_specs(SEL_PAGES, D_ATT, PAGE_SIZE, li),
        out_specs=pl.BlockSpec((None, R, W), lambda b, c, pt: (b, 0, 0)),
        scratch_shapes=[pltpu.VMEM((D_ATT, W), F32)],
    )
    return pl.pallas_call(
        body,
        grid_spec=grid_spec,
        out_shape=jax.ShapeDtypeStruct((DB, R, W), F32),
        compiler_params=_cparams(("parallel", "arbitrary")),
        name="smp_moba_select",
    )(page_table, qbd, *([cache_kT] * SEL_PAGES))


def _smp_attn_body(pt_ref, qbd_ref, rbr_ref, knew_ref, vnew_ref, sel_ref, selnew_ref, *rest, mode, n_pages):
    kp = rest[:ATT_PAGES]
    vp = rest[ATT_PAGES:2 * ATT_PAGES]
    o_ref, m_ref, l_ref, acc_ref = rest[2 * ATT_PAGES:]
    c = pl.program_id(1)
    R = qbd_ref.shape[0]
    T = R // N_HEADS
    q = (qbd_ref[...] * ATTN_SCALE).astype(BF16)
    rbr = rbr_ref[...]
    far_bias = rbr[:, N_BUCKETS - 1:]
    rowq = lax.broadcasted_iota(I32, (R, PAGE_SIZE), 0) & (T - 1)
    lane = lax.broadcasted_iota(I32, (R, PAGE_SIZE), 1)

    @pl.when(c == 0)
    def _():
        m_ref[...] = jnp.full(m_ref.shape, NEG_BIG, F32)
        l_ref[...] = jnp.zeros(l_ref.shape, F32)
        acc_ref[...] = jnp.zeros(acc_ref.shape, F32)

    def bias_for(dist):
        return _bias_chain(dist, lambda b: rbr[:, b:b + 1])

    def update(s, vT):
        m_old = m_ref[...]
        m_new = jnp.maximum(m_old, jnp.max(s, axis=-1, keepdims=True))
        p = jnp.exp(s - m_new)
        alpha = jnp.exp(m_old - m_new)
        l_ref[...] = alpha * l_ref[...] + jnp.sum(p, axis=-1, keepdims=True)
        acc_ref[...] = alpha * acc_ref[...] + _dot(p.astype(BF16), vT, _NT)
        m_ref[...] = m_new

    logits, vals = [], []
    for u in range(ATT_PAGES):
        page = c * ATT_PAGES + u
        s = _dot(q, kp[u][...].astype(BF16))
        if u == ATT_PAGES - 1:
            dist = (n_pages - page) * PAGE_SIZE + rowq - lane
            bias = lax.cond(c == pl.num_programs(1) - 1, lambda: bias_for(dist),
                            lambda: jnp.broadcast_to(far_bias, (R, PAGE_SIZE)))
        else:
            bias = far_bias
        if mode == "dsa":
            pen = jnp.where(jnp.tile(sel_ref[u], (N_HEADS, 1)) > 0.5, 0.0, NEG_BIG)
        else:
            blk = page // (MOBA_BLOCK // PAGE_SIZE)
            sel = sel_ref[...]
            pick = jnp.sum(jnp.where(lax.broadcasted_iota(I32, sel.shape, 1) == blk, sel, 0.0),
                           axis=-1, keepdims=True)
            pen = jnp.where(pick > 0.5, 0.0, NEG_BIG)
        logits.append(s + (bias + pen))
        vals.append(vp[u][...].astype(BF16))
    update(jnp.concatenate(logits, axis=1), jnp.concatenate(vals, axis=1))

    @pl.when(c == pl.num_programs(1) - 1)
    def _():
        s = _dot(q, knew_ref[...].astype(BF16)) + bias_for(rowq - lane)
        keep = lane <= rowq
        if mode == "dsa":
            keep = keep & (jnp.tile(selnew_ref[...], (N_HEADS, 1)) > 0.5)
        update(s + jnp.where(keep, 0.0, NEG_BIG), vnew_ref[...].astype(BF16))
        out = acc_ref[...] / l_ref[...]
        for h in range(N_HEADS):
            o_ref[:, h * HEAD_DIM:(h + 1) * HEAD_DIM] = out[h * T:(h + 1) * T, h * HEAD_DIM:(h + 1) * HEAD_DIM]


def _smp_attn(page_table, qbd, rbr, knewT, vnewT, sel, cache_kT, cache_vT, li, *, mode):
    DB, P = page_table.shape
    R = qbd.shape[1]
    T = R // N_HEADS
    assert T & (T - 1) == 0
    if mode == "dsa":
        sel_specs = [pl.BlockSpec((None, ATT_PAGES, T, PAGE_SIZE), lambda b, c, pt: (b, c, 0, 0)),
                     pl.BlockSpec((None, None, T, PAGE_SIZE), lambda b, c, pt: (b, P, 0, 0))]
    else:
        sel_specs = [pl.BlockSpec((None, R, sel.shape[-1]), lambda b, c, pt: (b, 0, 0)),
                     pl.BlockSpec((None, R, sel.shape[-1]), lambda b, c, pt: (b, 0, 0))]
    new_spec = pl.BlockSpec((None, D_ATT, PAGE_SIZE), lambda b, c, pt: (b, 0, 0))
    grid_spec = pltpu.PrefetchScalarGridSpec(
        num_scalar_prefetch=1,
        grid=(DB, P // ATT_PAGES),
        in_specs=[pl.BlockSpec((None, R, D_ATT), lambda b, c, pt: (b, 0, 0)),
                  pl.BlockSpec(rbr.shape, lambda b, c, pt: (0, 0)),
                  new_spec, new_spec]
                 + sel_specs + _page_specs(ATT_PAGES, D_ATT, PAGE_SIZE, li) + _page_specs(ATT_PAGES, D_ATT, PAGE_SIZE, li),
        out_specs=pl.BlockSpec((None, T, D_ATT), lambda b, c, pt: (b, 0, 0)),
        scratch_shapes=[pltpu.VMEM((R, 1), F32), pltpu.VMEM((R, 1), F32), pltpu.VMEM((R, D_ATT), F32)],
    )
    return pl.pallas_call(
        functools.partial(_smp_attn_body, mode=mode, n_pages=P),
        grid_spec=grid_spec,
        out_shape=jax.ShapeDtypeStruct((DB, T, D_ATT), F32),
        compiler_params=_cparams(("parallel", "arbitrary")),
        name="smp_attn_" + mode,
    )(page_table, qbd, rbr, knewT, vnewT, sel, sel, *([cache_kT] * ATT_PAGES), *([cache_vT] * ATT_PAGES))


def _project(x2, wT, g, b):
    o = IN_OFFS
    qkv_a = _mm(x2, wT[o[0]:o[3]])
    qi, ki, wi = _proj_idx(x2, wT[o[3]:o[4]], wT[o[4]:o[5]], wT[o[5]:o[6]], g[None], b[None])
    qk_b = _mm(x2, wT[o[6]:o[8]], passes=3)
    v_b = _mm(x2, wT[o[8]:o[9]])
    gates = _mm(x2, wT[o[9]:o[10]], act="sigmoid")
    return dict(q_a=qkv_a[:, :D_ATT], k_a=qkv_a[:, D_ATT:2 * D_ATT], v_a=qkv_a[:, 2 * D_ATT:],
                q_i=qi, k_i=ki, w_i=wi, q_b=qk_b[:, :D_ATT], k_b=qk_b[:, D_ATT:], v_b=v_b, gates=gates)


PROJ_TM = 512


def _proj_bulk_body(x_ref, w_ref, qaT_ref, kaT_ref, ka16_ref, vaT_ref, vaT16_ref, vbT_ref, vbT16_ref, g_ref):
    xb = x_ref[...].astype(BF16)
    rows = lambda n: w_ref[n * D_ATT:(n + 1) * D_ATT, :]
    qaT_ref[...] = (_dot(rows(0), xb, _NT) * PROMPT_Q_SCALE).astype(BF16)
    kaT = _dot(rows(1), xb, _NT)
    kaT_ref[...] = kaT
    ka16_ref[...] = kaT.T.astype(BF16)
    vaT = _dot(rows(2), xb, _NT)
    vaT_ref[...] = vaT
    vaT16_ref[...] = vaT.astype(BF16)
    vbT = _dot(rows(3), xb, _NT)
    vbT_ref[...] = vbT
    vbT16_ref[...] = vbT.astype(BF16)
    g_ref[...] = jax.nn.sigmoid(_dot(xb, w_ref[4 * D_ATT:, :], _NT))


def _proj_select_body(x_ref, w_ref, g_ref, b_ref, gc_ref, bc_ref,
                      qiT_ref, kiT_ref, kcat_ref, wiT_ref, qbT_ref, kbT_ref, kb16_ref):
    xh, xl = _split(x_ref[...])

    def proj_T(lo, hi):
        wh, wl = _split(w_ref[lo:hi, :])
        return _dot(wl, xh, _NT) + _dot(wh, xl, _NT) + _dot(wh, xh, _NT)

    o = [v - IN_OFFS[3] for v in IN_OFFS[3:9]]
    qiT_ref[...] = proj_T(o[0], o[1])
    kT = proj_T(o[1], o[2])
    mu = jnp.mean(kT, axis=0, keepdims=True)
    kc = kT - mu
    var = jnp.mean(kc * kc, axis=0, keepdims=True)
    kiT_ref[...] = kc * lax.rsqrt(var + LN_EPS) * gc_ref[...] + bc_ref[...]
    wh, wl = _split(w_ref[o[1]:o[2], :])
    k_nat = _dot(xl, wh, _NT) + _dot(xh, wl, _NT) + _dot(xh, wh, _NT)
    kcat_ref[...] = _kcat(_layer_norm(k_nat, g_ref[...], b_ref[...]))
    wiT_ref[...] = proj_T(o[2], o[3]) * (IDX_HEADS ** -0.5) * IDX_SCALE
    qbT_ref[...] = proj_T(o[3], o[4])
    kbT = proj_T(o[4], o[5])
    kbT_ref[...] = kbT
    kb16_ref[...] = kbT.T.astype(BF16)


def _project_prompt(x2, wT, g, b, B):
    M, K = x2.shape
    S = M // B
    tm = min(S, PROJ_TM)
    nt = S // tm
    o = IN_OFFS
    x_spec = pl.BlockSpec((tm, K), lambda i: (i, 0))
    full = lambda a: pl.BlockSpec(a.shape, lambda i: (0,) * a.ndim)
    specT = lambda n: pl.BlockSpec((None, n, tm), lambda i: (i // nt, 0, i % nt))
    spec = lambda n: pl.BlockSpec((tm, n), lambda i: (i, 0))
    shapeT = lambda n, d: jax.ShapeDtypeStruct((B, n, S), d)
    shape = lambda n, d: jax.ShapeDtypeStruct((M, n), d)

    w_bulk = jnp.concatenate([wT[o[0]:o[3]], wT[o[8]:o[10]]], axis=0).astype(BF16)
    q_aT, k_aT, k_a16, v_aT, v_aT16, v_bT, v_bT16, gates = pl.pallas_call(
        _proj_bulk_body,
        grid=(M // tm,),
        in_specs=[x_spec, full(w_bulk)],
        out_specs=[specT(D_ATT), specT(D_ATT), spec(D_ATT), specT(D_ATT), specT(D_ATT), specT(D_ATT), specT(D_ATT),
                   spec(2 * D_MODEL)],
        out_shape=[shapeT(D_ATT, BF16), shapeT(D_ATT, F32), shape(D_ATT, BF16), shapeT(D_ATT, F32), shapeT(D_ATT, BF16),
                   shapeT(D_ATT, F32), shapeT(D_ATT, BF16), shape(2 * D_MODEL, F32)],
        compiler_params=_cparams(("parallel",)),
        name="proj_bulk",
    )(x2, w_bulk)

    w_sel = wT[o[3]:o[8]]
    q_iT, k_iT, kcat, w_iT, q_bT, k_bT, k_b16 = pl.pallas_call(
        _proj_select_body,
        grid=(M // tm,),
        in_specs=[x_spec, full(w_sel)] + [pl.BlockSpec((1, IDX_DIM), lambda i: (0, 0))] * 2
                 + [pl.BlockSpec((IDX_DIM, 1), lambda i: (0, 0))] * 2,
        out_specs=[specT(IDX_HEADS * IDX_DIM), specT(IDX_DIM), spec(4 * IDX_DIM), specT(IDX_HEADS),
                   specT(D_ATT), specT(D_ATT), spec(D_ATT)],
        out_shape=[shapeT(IDX_HEADS * IDX_DIM, F32), shapeT(IDX_DIM, F32), shape(4 * IDX_DIM, BF16),
                   shapeT(IDX_HEADS, F32), shapeT(D_ATT, F32), shapeT(D_ATT, F32), shape(D_ATT, BF16)],
        compiler_params=_cparams(("parallel",)),
        name="proj_select",
    )(x2, w_sel, g[None], b[None], g[:, None], b[:, None])

    return dict(q_aT=q_aT, k_aT=k_aT, k_a16=k_a16, v_aT=v_aT, v_aT16=v_aT16, q_iT=q_iT, k_iT=k_iT, kcat=kcat,
                w_iT=w_iT, q_bT=q_bT, k_bT=k_bT, k_b16=k_b16, v_bT=v_bT, v_bT16=v_bT16, gates=gates)


def _heads_major(a, B, S):
    return a.reshape(B, S, N_HEADS, HEAD_DIM).transpose(0, 2, 1, 3)


def _block_diag_rows(q):
    DB, T, H, Dh = q.shape
    eye = jnp.eye(H, dtype=q.dtype)
    return jnp.einsum("bthd,hg->bhtgd", q, eye).reshape(DB, H * T, H * Dh)


def _prompt_mixers(pr, B, S, rb_a, rb_b, T_dsa):
    bt_a = _bias_tiles(rb_a, T_dsa)
    o_a = _dsa_prompt(rb_a[N_BUCKETS - 1], bt_a, pr["q_aT"], pr["k_a16"], pr["v_aT16"],
                      pr["q_iT"], pr["w_iT"], pr["kcat"], T=T_dsa)
    T = MOBA_BLOCK
    nf = S // T
    n_slots = 128 // N_HEADS
    assert nf <= n_slots
    means = _block_means(pr["k_bT"]).reshape(B, N_HEADS, HEAD_DIM, nf)
    eye = jnp.eye(N_HEADS, dtype=F32)
    mbdT = jnp.einsum("bhdj,hg->bjghd", means, eye)
    mbdT = jnp.pad(mbdT, ((0, 0), (0, n_slots - nf), (0, 0), (0, 0), (0, 0))).reshape(B, n_slots * N_HEADS, D_ATT)
    bt_b = _bias_tiles(rb_b, T)
    o_b = _moba_prompt(rb_b[N_BUCKETS - 1], bt_b, pr["q_bT"], pr["k_b16"], pr["v_bT16"], mbdT)
    return o_a, o_b


def _sample_mixers(sm, DB, T, caches, page_table, rb_a, rb_b, li):
    ck_a, cv_a, ck_i, ck_b, cv_b = caches
    new_page = lambda a: jnp.pad(a.reshape(DB, T, -1), ((0, 0), (0, PAGE_SIZE - T), (0, 0))).transpose(0, 2, 1)
    qi_rows = _heads_major(sm["q_i"], DB, T).reshape(DB, IDX_HEADS * T, IDX_DIM)
    wi_rows = sm["w_i"].reshape(DB, T, IDX_HEADS).transpose(0, 2, 1).reshape(DB, IDX_HEADS * T, 1)
    sel_a = _smp_dsa_select(page_table, qi_rows, wi_rows, new_page(sm["k_i"]), ck_i, li)
    rbr_a = jnp.repeat(rb_a.T, T, axis=0)
    rbr_b = jnp.repeat(rb_b.T, T, axis=0)
    qbd_a = _block_diag_rows(sm["q_a"].reshape(DB, T, N_HEADS, HEAD_DIM))
    o_a = _smp_attn(page_table, qbd_a, rbr_a, new_page(sm["k_a"]), new_page(sm["v_a"]), sel_a,
                    ck_a, cv_a, li, mode="dsa")
    qbd_b = _block_diag_rows(sm["q_b"].reshape(DB, T, N_HEADS, HEAD_DIM))
    sel_b = _smp_moba_select(page_table, qbd_b, ck_b, li)
    o_b = _smp_attn(page_table, qbd_b, rbr_b, new_page(sm["k_b"]), new_page(sm["v_b"]), sel_b,
                    ck_b, cv_b, li, mode="moba")
    return o_a.reshape(DB * T, D_ATT), o_b.reshape(DB * T, D_ATT)


def kernel(x_prompt, x_sample, cache_k_a, cache_v_a, cache_kidx, cache_k_b, cache_v_b, page_table, p_prompt, p_sample, rel_bias, w_in, kidx_ln_g, kidx_ln_b, w_branch_a, w_branch_b, w_out, ln1_g, ln1_b, w_router, b_router, w_gate_up, b_gate_up, w_down, b_down, w_ple_gate, w_ple_proj, ln2_g, ln2_b):
    B, S, D = x_prompt.shape
    DB, T, _ = x_sample.shape
    depth = w_in.shape[0]
    n_pool = cache_k_a.shape[1]
    rb_a = rel_bias[:, :N_HEADS]
    rb_b = rel_bias[:, N_HEADS:]
    kv_pages = lambda c: jnp.transpose(c, (0, 1, 3, 4, 2)).reshape(depth, n_pool, D_ATT, PAGE_SIZE)
    caches = (kv_pages(cache_k_a), kv_pages(cache_v_a), jnp.transpose(cache_kidx, (0, 1, 3, 2)),
              kv_pages(cache_k_b), kv_pages(cache_v_b))
    T_dsa = min(256, S)

    xp = x_prompt.reshape(B * S, D)
    xs = x_sample.reshape(DB * T, D)
    rows_p, rows_s = [], []
    for li in range(depth):
        bf = lambda a: a[li].astype(BF16)
        merge_w = (bf(w_branch_a), bf(w_branch_b), bf(w_out), ln1_g[li][None], ln1_b[li][None],
                   w_router[li], b_router[li][None], bf(w_ple_gate), bf(w_ple_proj))
        moe_w = (bf(w_gate_up), b_gate_up[li], bf(w_down), b_down[li], ln2_g[li][None], ln2_b[li][None])

        w_inT = jnp.swapaxes(w_in[li], 0, 1)
        pr = _project_prompt(xp, w_inT, kidx_ln_g[li], kidx_ln_b[li], B)
        o_a, o_b = _prompt_mixers(pr, B, S, rb_a, rb_b, T_dsa)
        x1b, comb, res = _merge(xp, o_a, o_b, pr["gates"], p_prompt[li].reshape(B * S, -1), *merge_w)
        xp = _moe(x1b, res, comb, *moe_w)
        rows_p.append(pr)

        sm = _project(xs, w_inT, kidx_ln_g[li], kidx_ln_b[li])
        o_a, o_b = _sample_mixers(sm, DB, T, caches, page_table, rb_a, rb_b, li)
        x1b, comb, res = _merge(xs, o_a, o_b, sm["gates"], p_sample[li].reshape(DB * T, -1), *merge_w)
        xs = _moe(x1b, res, comb, *moe_w)
        rows_s.append(sm)

    def stack(rows, name, lead, tail):
        return jnp.stack([r[name].reshape(lead + tail) for r in rows])

    def stack_T(name, feat):
        a = jnp.stack([r[name] for r in rows_p])
        return jnp.moveaxis(a.reshape((depth, B) + feat + (S,)), -1, 2)

    hd = (N_HEADS, HEAD_DIM)
    outs = [xp.reshape(B, S, D), xs.reshape(DB, T, D)]
    outs += [stack_T("k_aT", hd), stack_T("v_aT", hd), stack_T("k_iT", (IDX_DIM,)),
             stack_T("k_bT", hd), stack_T("v_bT", hd)]
    lead = (DB, T)
    outs += [stack(rows_s, "k_a", lead, hd), stack(rows_s, "v_a", lead, hd), stack(rows_s, "k_i", lead, (IDX_DIM,)),
             stack(rows_s, "k_b", lead, hd), stack(rows_s, "v_b", lead, hd)]
    return tuple(outs)
```

```python
import functools
import math

import numpy as np
import jax
import jax.numpy as jnp
from jax import lax
from jax.experimental import pallas as pl
from jax.experimental.pallas import tpu as pltpu

F32 = jnp.float32
BF16 = jnp.bfloat16
I32 = jnp.int32

D_MODEL = 1024
HEAD_DIM = 64
N_HEADS = 8
D_ATT = N_HEADS * HEAD_DIM
IDX_HEADS = 8
IDX_DIM = 64
IDX_TOPK = 256
MOBA_BLOCK = 256
MOBA_TOPK = 3
N_BUCKETS = 32
MAX_DISTANCE = 128
N_EXPERTS = 32
TOP_K = 4
D_EXPERT = D_MODEL
SWIGLU_LIMIT = 7.0
SWIGLU_ALPHA = 1.702
PLE_DIM = 256
LN_EPS = 1e-5
PAGE_SIZE = 128
DEPTH = 2
ALPHA_DN = (2 * DEPTH) ** 0.25
ATTN_SCALE = HEAD_DIM ** -0.5
LOG2E = math.log2(math.e)
PROMPT_Q_SCALE = ATTN_SCALE * LOG2E
IDX_SCALE = IDX_DIM ** -0.5
IN_SIZES = (D_ATT, D_ATT, D_ATT, IDX_HEADS * IDX_DIM, IDX_DIM, IDX_HEADS, D_ATT, D_ATT, D_ATT, 2 * D_MODEL)
IN_OFFS = tuple(int(v) for v in np.cumsum((0,) + IN_SIZES))

INT_MIN = -(2 ** 31)
NEG_BIG = -1e30
VMEM_LIMIT_BYTES = 56 * 1024 * 1024


def _bucket_of(d):
    d = max(d, 0)
    max_exact = N_BUCKETS // 2
    if d < max_exact:
        return d
    ratio = math.log(d / max_exact) / math.log(MAX_DISTANCE / max_exact)
    return min(max_exact + int(ratio * (N_BUCKETS - max_exact)), N_BUCKETS - 1)


_BUCKET_HI = tuple(max(d for d in range(4 * MAX_DISTANCE) if _bucket_of(d) == b) for b in range(N_BUCKETS - 1))
FAR_DIST = _BUCKET_HI[-1] + 1


def _cparams(sem):
    return pltpu.CompilerParams(dimension_semantics=sem, vmem_limit_bytes=VMEM_LIMIT_BYTES)


def _split(x):
    hi = x.astype(BF16)
    lo = (x - hi.astype(F32)).astype(BF16)
    return hi, lo


_NN = (((1,), (0,)), ((), ()))
_NT = (((1,), (1,)), ((), ()))


def _dot(a, b, dims=_NN):
    return lax.dot_general(a, b, dims, preferred_element_type=F32)


def _dot3(a, b, dims=_NN):
    ah, al = _split(a)
    bh, bl = _split(b)
    return _dot(al, bh, dims) + _dot(ah, bl, dims) + _dot(ah, bh, dims)


def _layer_norm(x, g, b):
    mu = jnp.mean(x, axis=-1, keepdims=True)
    xc = x - mu
    var = jnp.mean(xc * xc, axis=-1, keepdims=True)
    return xc * lax.rsqrt(var + LN_EPS) * g + b


def _order_key(s):
    s = jnp.where(s == 0.0, 0.0, s)
    u = pltpu.bitcast(s, I32)
    return u ^ (jnp.right_shift(u, 31) & 0x7FFFFFFF)


def _bias_chain(d, rb_of):
    val = rb_of(N_BUCKETS - 1)
    for b in range(N_BUCKETS - 2, -1, -1):
        val = jnp.where(d <= _BUCKET_HI[b], rb_of(b), val)
    return val


def _mm_body(x_ref, wT_ref, o_ref, *, passes, act):
    x = x_ref[...]
    wT = wT_ref[...]
    y = _dot(x.astype(BF16), wT.astype(BF16), _NT) if passes == 1 else _dot3(x, wT, _NT)
    if act == "sigmoid":
        y = jax.nn.sigmoid(y)
    o_ref[...] = y.astype(o_ref.dtype)


def _mm(x, wT, *, passes=1, act=None, out_dtype=F32):
    M, K = x.shape
    N = wT.shape[0]
    tm = min(M, 512)
    tn = min(N, 512)
    return pl.pallas_call(
        functools.partial(_mm_body, passes=passes, act=act),
        grid=(M // tm, N // tn),
        in_specs=[pl.BlockSpec((tm, K), lambda i, j: (i, 0)),
                  pl.BlockSpec((tn, K), lambda i, j: (j, 0))],
        out_specs=pl.BlockSpec((tm, tn), lambda i, j: (i, j)),
        out_shape=jax.ShapeDtypeStruct((M, N), out_dtype),
        compiler_params=_cparams(("parallel", "parallel")),
        name="proj_mm",
    )(x, wT)


def _kcat(k):
    kh, kl = _split(k)
    return jnp.concatenate([kh, kl, kh, jnp.zeros_like(kh)], axis=-1)


def _proj_idx_body(x_ref, wqT_ref, wkT_ref, wwT_ref, g_ref, b_ref, qi_ref, ki_ref, wi_ref):
    x = x_ref[...]
    qi_ref[...] = _dot3(x, wqT_ref[...], _NT)
    ki_ref[...] = _layer_norm(_dot3(x, wkT_ref[...], _NT), g_ref[...], b_ref[...])
    wi_ref[...] = _dot3(x, wwT_ref[...], _NT) * (IDX_HEADS ** -0.5)


def _proj_idx(x, wqT, wkT, wwT, g, b):
    M, K = x.shape
    tm = min(M, 512)
    full = lambda a: pl.BlockSpec(a.shape, lambda i: (0,) * a.ndim)
    row = lambda n: pl.BlockSpec((tm, n), lambda i: (i, 0))
    return pl.pallas_call(
        _proj_idx_body,
        grid=(M // tm,),
        in_specs=[row(K), full(wqT), full(wkT), full(wwT), full(g), full(b)],
        out_specs=[row(IDX_HEADS * IDX_DIM), row(IDX_DIM), row(IDX_HEADS)],
        out_shape=[jax.ShapeDtypeStruct((M, IDX_HEADS * IDX_DIM), F32),
                   jax.ShapeDtypeStruct((M, IDX_DIM), F32),
                   jax.ShapeDtypeStruct((M, IDX_HEADS), F32)],
        compiler_params=_cparams(("parallel",)),
        name="proj_idx",
    )(x, wqT, wkT, wwT, g, b)


def _bias_tiles_body(rb_ref, o_ref, *, T):
    rel = pl.program_id(0)
    h = pl.program_id(1)
    key = lax.broadcasted_iota(I32, (T, T), 0)
    qry = lax.broadcasted_iota(I32, (T, T), 1)
    d = qry - key + rel * T
    o_ref[...] = _bias_chain(d, lambda b: rb_ref[b, h]) * LOG2E


def _bias_tiles(rb, T):
    H = rb.shape[1]
    return pl.pallas_call(
        functools.partial(_bias_tiles_body, T=T),
        grid=(2, H),
        in_specs=[pl.BlockSpec(memory_space=pltpu.SMEM)],
        out_specs=pl.BlockSpec((None, None, T, T), lambda r, h: (r, h, 0, 0)),
        out_shape=jax.ShapeDtypeStruct((2, H, T, T), F32),
        compiler_params=_cparams(("parallel", "parallel")),
        name="bias_tiles",
    )(rb)


ATTN_HEAD_GROUP = 8


def _attn_update(tile_of, shift_of, vT_of, m_ref, l_ref, acc_ref):
    for h0 in range(0, N_HEADS, ATTN_HEAD_GROUP):
        heads = range(h0, h0 + ATTN_HEAD_GROUP)
        s = {h: tile_of(h) for h in heads}
        shift = {h: shift_of(h) for h in heads}
        m_old = {h: m_ref[h] for h in heads}
        m_tile = {h: jnp.max(s[h], axis=0, keepdims=True) for h in heads}
        m_new = {h: jnp.maximum(m_old[h], m_tile[h] + shift[h]) for h in heads}
        ref = {h: jnp.maximum(m_new[h] - shift[h], m_tile[h]) for h in heads}
        p = {h: jnp.exp2(s[h] - ref[h]) for h in heads}
        alpha = {h: jnp.exp2(m_old[h] - m_new[h]) for h in heads}
        for h in heads:
            l_ref[h] = alpha[h] * l_ref[h] + jnp.sum(p[h], axis=0, keepdims=True)
            m_ref[h] = m_new[h]
        pv = {h: _dot(vT_of(h), p[h].astype(BF16)) for h in heads}
        for h in heads:
            acc_ref[h] = alpha[h] * acc_ref[h] + pv[h]


def _attn_init(m_ref, l_ref, acc_ref):
    m_ref[...] = jnp.full(m_ref.shape, NEG_BIG, F32)
    l_ref[...] = jnp.zeros(l_ref.shape, F32)
    acc_ref[...] = jnp.zeros(acc_ref.shape, F32)


def _attn_finish(o_ref, l_ref, acc_ref):
    outT = jnp.concatenate([acc_ref[h] / l_ref[h] for h in range(N_HEADS)], axis=0)
    o_ref[...] = outT.T


def _padded_heads(qT_of):
    out = []
    for h in range(N_HEADS):
        q = qT_of(h)
        z = jnp.zeros_like(q)
        out.append(jnp.concatenate([q, z] if h % 2 == 0 else [z, q], axis=0))
    return out


def _head_pair(k, h):
    lo = (h // 2) * 2 * HEAD_DIM
    return k[:, lo:lo + 2 * HEAD_DIM]


def _dsa_prompt_body(rbf_ref, bt_ref, qa_ref, k_ref, vT_ref, qi_ref, wi_ref, kcat_ref, o_ref,
                     key_ref, m_ref, l_ref, acc_ref, *, T, n_top, idx_bits):
    i = pl.program_id(1)
    H = N_HEADS
    kpos = lax.broadcasted_iota(I32, (T, T), 0)
    qpos = lax.broadcasted_iota(I32, (T, T), 1)

    wi = wi_ref[...]
    qcat = []
    for h in range(IDX_HEADS):
        qh, ql = _split(qi_ref[h * IDX_DIM:(h + 1) * IDX_DIM, :])
        qcat.append(jnp.concatenate([qh, qh, ql, jnp.zeros_like(qh)], axis=0))

    def score_chunk(kc, carry):
        kcat = kcat_ref[pl.ds(pl.multiple_of(kc * T, T), T), :]
        s = jnp.zeros((T, T), F32)
        for h in range(IDX_HEADS):
            s = s + jnp.maximum(_dot(kcat, qcat[h]), 0.0) * wi[h:h + 1, :]
        key = _order_key(s)
        key_ref[kc] = jnp.where((kc == i) & (kpos > qpos), INT_MIN, key)
        return carry

    lax.fori_loop(0, i + 1, score_chunk, 0)

    def count(pred):
        def body(kc, acc):
            hit = jnp.where(pred(key_ref[kc], kc * T + kpos), 1.0, 0.0)
            return acc + jnp.sum(hit.reshape(T // 8, 8, T), axis=0)
        acc = lax.fori_loop(0, i + 1, body, jnp.zeros((8, T), F32))
        return jnp.sum(acc, axis=0, keepdims=True)

    def thr_bit(bi, prefix):
        cand = prefix | jnp.left_shift(jnp.int32(1), 31 - bi)
        cs = cand ^ INT_MIN
        return jnp.where(count(lambda k, g: k >= cs) >= n_top, cand, prefix)

    thr = lax.fori_loop(0, 32, thr_bit, jnp.zeros((1, T), I32)) ^ INT_MIN
    need = n_top - count(lambda k, g: k > thr)
    n_tie = count(lambda k, g: k == thr)

    def tie_search():
        def bit(bi, j0):
            cand = j0 | jnp.left_shift(jnp.int32(1), idx_bits - 1 - bi)
            c = count(lambda k, g: (k == thr) & (g < cand))
            return jnp.where(c < need, cand, j0)
        return lax.fori_loop(0, idx_bits, bit, jnp.zeros((1, T), I32))

    any_excess = jnp.max(n_tie - need) > 0.0
    jcut = lax.cond(any_excess, tie_search, lambda: jnp.full((1, T), 2 ** 30, I32))

    _attn_init(m_ref, l_ref, acc_ref)
    q = _padded_heads(lambda h: qa_ref[h * HEAD_DIM:(h + 1) * HEAD_DIM, :])

    def attn_chunk(kc, near):
        kk = k_ref[pl.ds(pl.multiple_of(kc * T, T), T), :]
        keys = key_ref[kc]
        g = kc * T + kpos
        sel = (keys > thr) | ((keys == thr) & (g <= jcut))
        if near:
            sel = sel & (g <= i * T + qpos)
        pen = jnp.where(sel, 0.0, NEG_BIG)
        _attn_update(
            lambda h: _dot(_head_pair(kk, h), q[h]) + ((bt_ref[i - kc, h] + pen) if near else pen),
            lambda h: 0.0 if near else rbf_ref[h] * LOG2E,
            lambda h: vT_ref[h * HEAD_DIM:(h + 1) * HEAD_DIM, pl.ds(pl.multiple_of(kc * T, T), T)],
            m_ref, l_ref, acc_ref)

    n_far = jnp.maximum(i - 1, 0)
    lax.fori_loop(0, n_far, lambda kc, c: (attn_chunk(kc, False), c)[1], 0)
    lax.fori_loop(n_far, i + 1, lambda kc, c: (attn_chunk(kc, True), c)[1], 0)
    _attn_finish(o_ref, l_ref, acc_ref)


def _attn_scratch(T):
    return [pltpu.VMEM((N_HEADS, 1, T), F32), pltpu.VMEM((N_HEADS, 1, T), F32),
            pltpu.VMEM((N_HEADS, HEAD_DIM, T), F32)]


def _dsa_prompt(rb_far, btiles, qaT, k16, vT, qiT, wiT, kcat, *, T):
    B, _, S = qaT.shape
    M = B * S
    H = N_HEADS
    nc = S // T
    n_top = min(IDX_TOPK, S // 4)
    body = functools.partial(_dsa_prompt_body, T=T, n_top=n_top, idx_bits=max(1, (S - 1).bit_length()))
    tile = lambda rows: pl.BlockSpec((None, rows, T), lambda b, i: (b, 0, i))
    return pl.pallas_call(
        body,
        grid=(B, nc),
        in_specs=[pl.BlockSpec(memory_space=pltpu.SMEM),
                  pl.BlockSpec((2, H, T, T), lambda b, i: (0, 0, 0, 0)),
                  tile(D_ATT),
                  pl.BlockSpec((S, D_ATT), lambda b, i: (b, 0)),
                  pl.BlockSpec((None, D_ATT, S), lambda b, i: (b, 0, 0)),
                  tile(IDX_HEADS * IDX_DIM),
                  tile(IDX_HEADS),
                  pl.BlockSpec((S, 4 * IDX_DIM), lambda b, i: (b, 0))],
        out_specs=pl.BlockSpec((T, D_ATT), lambda b, i: (b * nc + i, 0)),
        out_shape=jax.ShapeDtypeStruct((M, D_ATT), F32),
        scratch_shapes=[pltpu.VMEM((nc, T, T), I32)] + _attn_scratch(T),
        compiler_params=_cparams(("parallel", "arbitrary")),
        name="dsa_prompt",
    )(rb_far, btiles, qaT, k16, vT, qiT, wiT, kcat)


def _block_means_body(kT_ref, o_ref):
    nf = o_ref.shape[-1]
    lane = lax.broadcasted_iota(I32, o_ref.shape, 1)
    means = jnp.zeros(o_ref.shape, F32)
    for j in range(nf):
        col = jnp.sum(kT_ref[:, j * MOBA_BLOCK:(j + 1) * MOBA_BLOCK], axis=1, keepdims=True) * (1.0 / MOBA_BLOCK)
        means = jnp.where(lane == j, col, means)
    o_ref[...] = means


def _block_means(kT):
    B, D, L = kT.shape
    nf = L // MOBA_BLOCK
    return pl.pallas_call(
        _block_means_body,
        grid=(B,),
        in_specs=[pl.BlockSpec((None, D, L), lambda b: (b, 0, 0))],
        out_specs=pl.BlockSpec((None, D, nf), lambda b: (b, 0, 0)),
        out_shape=jax.ShapeDtypeStruct((B, D, nf), F32),
        compiler_params=_cparams(("parallel",)),
        name="block_means",
    )(kT)


def _moba_prompt_body(rbf_ref, bt_ref, qbT_ref, k_ref, vT_ref, mbdT_ref, o_ref,
                      ch_ref, m_ref, l_ref, acc_ref, *, T, n_sel, n_slots):
    i = pl.program_id(1)
    H = N_HEADS
    W = n_slots * H
    kpos = lax.broadcasted_iota(I32, (T, T), 0)
    qpos = lax.broadcasted_iota(I32, (T, T), 1)
    qT = qbT_ref[...]

    g = _dot3(mbdT_ref[...], qT)
    blk = jnp.right_shift(lax.broadcasted_iota(I32, (W, T), 0), int(math.log2(H)))
    g = jnp.where(blk < i, g, -jnp.inf)
    rank = jnp.zeros((W, T), F32)
    for r in range(1, n_slots):
        other = pltpu.roll(g, r * H, axis=0)
        rank = rank + jnp.where(blk >= r, jnp.where(other >= g, 1.0, 0.0), jnp.where(other > g, 1.0, 0.0))
    ch_ref[...] = jnp.where((rank < n_sel) & (blk < i), 0.0, NEG_BIG)

    _attn_init(m_ref, l_ref, acc_ref)
    q = _padded_heads(lambda h: (qT[h * HEAD_DIM:(h + 1) * HEAD_DIM, :] * PROMPT_Q_SCALE).astype(BF16))
    causal_pen = jnp.where(kpos <= qpos, 0.0, NEG_BIG)

    def attn_chunk(kc, near):
        kk = k_ref[pl.ds(pl.multiple_of(kc * T, T), T), :]

        picked = lambda h: ch_ref[pl.ds(kc * H + h, 1), :]

        def tile_of(h):
            s = _dot(_head_pair(kk, h), q[h])
            return s + (bt_ref[i - kc, h] + jnp.where(kc == i, causal_pen, picked(h))) if near else s

        _attn_update(
            tile_of,
            lambda h: 0.0 if near else picked(h) + rbf_ref[h] * LOG2E,
            lambda h: vT_ref[h * HEAD_DIM:(h + 1) * HEAD_DIM, pl.ds(pl.multiple_of(kc * T, T), T)],
            m_ref, l_ref, acc_ref)

    n_far = jnp.maximum(i - 1, 0)
    lax.fori_loop(0, n_far, lambda kc, c: (attn_chunk(kc, False), c)[1], 0)
    lax.fori_loop(n_far, i + 1, lambda kc, c: (attn_chunk(kc, True), c)[1], 0)
    _attn_finish(o_ref, l_ref, acc_ref)


def _moba_prompt(rb_far, btiles, qbT, k16, vT, mbdT):
    B, _, S = qbT.shape
    M = B * S
    H = N_HEADS
    T = MOBA_BLOCK
    nc = S // T
    W = mbdT.shape[1]
    body = functools.partial(_moba_prompt_body, T=T, n_sel=min(MOBA_TOPK, nc), n_slots=W // H)
    return pl.pallas_call(
        body,
        grid=(B, nc),
        in_specs=[pl.BlockSpec(memory_space=pltpu.SMEM),
                  pl.BlockSpec((2, H, T, T), lambda b, i: (0, 0, 0, 0)),
                  pl.BlockSpec((None, D_ATT, T), lambda b, i: (b, 0, i)),
                  pl.BlockSpec((S, D_ATT), lambda b, i: (b, 0)),
                  pl.BlockSpec((None, D_ATT, S), lambda b, i: (b, 0, 0)),
                  pl.BlockSpec((None, W, D_ATT), lambda b, i: (b, 0, 0))],
        out_specs=pl.BlockSpec((T, D_ATT), lambda b, i: (b * nc + i, 0)),
        out_shape=jax.ShapeDtypeStruct((M, D_ATT), F32),
        scratch_shapes=[pltpu.VMEM((W, T), F32)] + _attn_scratch(T),
        compiler_params=_cparams(("parallel", "arbitrary")),
        name="moba_prompt",
    )(rb_far, btiles, qbT, k16, vT, mbdT)


def _merge_body(x_ref, oa_ref, ob_ref, g_ref, p_ref, wba_ref, wbb_ref, wo_ref, g1_ref, b1_ref,
                wr_ref, br_ref, wpg_ref, wpp_ref, x1b_ref, comb_ref, res_ref):
    bra = _dot(oa_ref[...].astype(BF16), wba_ref[...])
    brb = _dot(ob_ref[...].astype(BF16), wbb_ref[...])
    gates = g_ref[...]
    mix = gates[:, :D_MODEL] * bra + gates[:, D_MODEL:] * brb
    y = _dot(mix.astype(BF16), wo_ref[...])
    x1 = _layer_norm(ALPHA_DN * x_ref[...] + y, g1_ref[...], b1_ref[...])
    x1b = x1.astype(BF16)
    x1b_ref[...] = x1b

    logits = _dot3(x1, wr_ref[...]) + br_ref[...]
    lane = lax.broadcasted_iota(I32, logits.shape, 1)
    work = logits
    kept = jnp.zeros(logits.shape, jnp.bool_)
    for _ in range(TOP_K):
        mx = jnp.max(work, axis=-1, keepdims=True)
        first = jnp.min(jnp.where(work == mx, lane, N_EXPERTS), axis=-1, keepdims=True)
        hit = lane == first
        kept = kept | hit
        work = jnp.where(hit, -jnp.inf, work)
    top = jnp.max(logits, axis=-1, keepdims=True)
    e = jnp.where(kept, jnp.exp(logits - top), 0.0)
    comb_ref[...] = e / jnp.sum(e, axis=-1, keepdims=True)

    ple = jax.nn.sigmoid(_dot(x1b, wpg_ref[...])) * _dot(p_ref[...].astype(BF16), wpp_ref[...])
    res_ref[...] = ALPHA_DN * x1 + ple


def _merge(x, oa, ob, gates, p, wba, wbb, wo, g1, b1, wr, br, wpg, wpp):
    M = x.shape[0]
    tm = min(M, 512)
    full = lambda a: pl.BlockSpec(a.shape, lambda i: (0,) * a.ndim)
    row = lambda n: pl.BlockSpec((tm, n), lambda i: (i, 0))
    return pl.pallas_call(
        _merge_body,
        grid=(M // tm,),
        in_specs=[row(D_MODEL), row(D_ATT), row(D_ATT), row(2 * D_MODEL), row(PLE_DIM),
                  full(wba), full(wbb), full(wo), full(g1), full(b1), full(wr), full(br), full(wpg), full(wpp)],
        out_specs=[row(D_MODEL), row(N_EXPERTS), row(D_MODEL)],
        out_shape=[jax.ShapeDtypeStruct((M, D_MODEL), BF16),
                   jax.ShapeDtypeStruct((M, N_EXPERTS), F32),
                   jax.ShapeDtypeStruct((M, D_MODEL), F32)],
        compiler_params=_cparams(("parallel",)),
        name="merge",
    )(x, oa, ob, gates, p, wba, wbb, wo, g1, b1, wr, br, wpg, wpp)


MOE_TM = 1024
MOE_RB = 160
MOE_CB = 256


def _moe_body(xb_ref, res_ref, comb_ref, wgu_ref, bgu_ref, wdn_ref, bdn_ref, g2_ref, b2_ref,
              o_ref, rank_ref, rankT_ref, acc_ref):
    e = pl.program_id(1)
    TM, E = comb_ref.shape
    RB = MOE_RB

    @pl.when(e == 0)
    def _():
        routed = jnp.where(comb_ref[...] != 0.0, 1.0, 0.0)
        eye = jnp.where(lax.broadcasted_iota(I32, (E, E), 0) == lax.broadcasted_iota(I32, (E, E), 1),
                        1.0, 0.0).astype(BF16)
        routedT = _dot(eye, routed.astype(BF16), _NT)
        CB = min(MOE_CB, TM)
        r_i = lax.broadcasted_iota(I32, (CB, CB), 0)
        c_i = lax.broadcasted_iota(I32, (CB, CB), 1)
        before = jnp.where(c_i < r_i, 1.0, 0.0).astype(BF16)
        beforeT = jnp.where(r_i < c_i, 1.0, 0.0).astype(BF16)
        off = jnp.zeros((1, E), F32)
        offT = jnp.zeros((E, 1), F32)
        for blk in range(TM // CB):
            rb = routed[blk * CB:(blk + 1) * CB]
            rbT = routedT[:, blk * CB:(blk + 1) * CB]
            rank = _dot(before, rb.astype(BF16)) + off
            rankT = _dot(rbT.astype(BF16), beforeT) + offT
            rank_ref[blk * CB:(blk + 1) * CB, :] = jnp.where(rb > 0.5, rank, -1.0).astype(I32)
            rankT_ref[:, blk * CB:(blk + 1) * CB] = jnp.where(rbT > 0.5, rankT, -1.0).astype(I32)
            off = off + jnp.sum(rb, axis=0, keepdims=True)
            offT = offT + jnp.sum(rbT, axis=1, keepdims=True)
        acc_ref[...] = jnp.zeros(acc_ref.shape, F32)

    lane = lax.broadcasted_iota(I32, (TM, E), 1)
    gate_col = jnp.sum(jnp.where(lane == e, comb_ref[...], 0.0), axis=-1, keepdims=True)
    rank_col = jnp.sum(jnp.where(lane == e, rank_ref[...], 0), axis=-1, keepdims=True)
    rank_row = rankT_ref[pl.ds(e, 1), :]
    n_routed = jnp.sum(jnp.where(rank_row >= 0, 1, 0))
    n_pass = (n_routed + (RB - 1)) // RB
    slot_rows = lax.broadcasted_iota(I32, (RB, TM), 0)
    slot_lanes = lax.broadcasted_iota(I32, (TM, RB), 1)

    def one_pass(pi, carry):
        base = pi * RB
        pick = jnp.where(rank_row == slot_rows + base, 1.0, 0.0).astype(BF16)
        xg = _dot(pick, xb_ref[...]).astype(BF16)
        gu = _dot(xg, wgu_ref[...]) + bgu_ref[...]
        gt = jnp.minimum(gu[:, :D_EXPERT], SWIGLU_LIMIT)
        up = jnp.clip(gu[:, D_EXPERT:], -SWIGLU_LIMIT, SWIGLU_LIMIT)
        hid = (up + 1.0) * gt * jax.nn.sigmoid(SWIGLU_ALPHA * gt)
        down = _dot(hid.astype(BF16), wdn_ref[...])
        place = jnp.where(rank_col == slot_lanes + base, 1.0, 0.0).astype(BF16)
        acc_ref[...] += _dot(place, down.astype(BF16)) * gate_col
        return carry

    lax.fori_loop(0, n_pass, one_pass, 0)

    @pl.when(e == pl.num_programs(1) - 1)
    def _():
        y = acc_ref[...] + _dot3(comb_ref[...], bdn_ref[...])
        o_ref[...] = _layer_norm(res_ref[...] + y, g2_ref[...], b2_ref[...])


def _moe(xb, res, comb, w_gu, b_gu, w_dn, b_dn, g2, b2):
    M = xb.shape[0]
    tm = min(M, MOE_TM)
    E = w_gu.shape[0]
    b_gu3 = b_gu.reshape(E, 1, 2 * D_EXPERT)
    row = lambda n: pl.BlockSpec((tm, n), lambda i, e: (i, 0))
    return pl.pallas_call(
        _moe_body,
        grid=(M // tm, E),
        in_specs=[row(D_MODEL), row(D_MODEL), row(E),
                  pl.BlockSpec((None, D_MODEL, 2 * D_EXPERT), lambda i, e: (e, 0, 0)),
                  pl.BlockSpec((None, 1, 2 * D_EXPERT), lambda i, e: (e, 0, 0)),
                  pl.BlockSpec((None, D_EXPERT, D_MODEL), lambda i, e: (e, 0, 0)),
                  pl.BlockSpec((E, D_MODEL), lambda i, e: (0, 0)),
                  pl.BlockSpec((1, D_MODEL), lambda i, e: (0, 0)),
                  pl.BlockSpec((1, D_MODEL), lambda i, e: (0, 0))],
        out_specs=row(D_MODEL),
        out_shape=jax.ShapeDtypeStruct((M, D_MODEL), F32),
        scratch_shapes=[pltpu.VMEM((tm, E), I32), pltpu.VMEM((E, tm), I32), pltpu.VMEM((tm, D_MODEL), F32)],
        compiler_params=_cparams(("parallel", "arbitrary")),
        name="moe",
    )(xb, res, comb, w_gu, b_gu3, w_dn, b_dn, g2, b2)


SEL_PAGES = 16
ATT_PAGES = 16
IDX_PAGES = 64


def _page_specs(n, rows, width, li):
    def spec(u):
        return pl.BlockSpec((None, None, rows, width),
                            lambda b, c, pt: (li, pt[b, c * n + u], 0, 0))
    return [spec(u) for u in range(n)]


def _smp_dsa_select_body(pt_ref, qi_ref, wi_ref, knew_ref, *rest, n_pages, n_top, idx_bits):
    pages, (sel_ref, key_ref) = rest[:-2], rest[-2:]
    step_pages = len(pages)
    c = pl.program_id(1)
    T = qi_ref.shape[0] // IDX_HEADS
    qi = qi_ref[...]
    wi = wi_ref[...] * IDX_SCALE

    def page_score(kpT):
        s = jnp.maximum(_dot3(qi, kpT), 0.0) * wi
        return jnp.sum(s.reshape(IDX_HEADS, T, kpT.shape[1]), axis=0)

    keys = _order_key(page_score(jnp.concatenate([p[...] for p in pages], axis=1)))
    for u in range(step_pages):
        key_ref[c * step_pages + u] = keys[:, u * PAGE_SIZE:(u + 1) * PAGE_SIZE]

    @pl.when(c == pl.num_programs(1) - 1)
    def _():
        qrow = lax.broadcasted_iota(I32, (T, PAGE_SIZE), 0)
        lane = lax.broadcasted_iota(I32, (T, PAGE_SIZE), 1)
        key_ref[n_pages] = jnp.where(lane <= qrow, _order_key(page_score(knew_ref[...])), INT_MIN)
        keys = key_ref[...]
        shape = keys.shape
        gidx = lax.broadcasted_iota(I32, shape, 0) * PAGE_SIZE + lax.broadcasted_iota(I32, shape, 2)
        valid = (lax.broadcasted_iota(I32, shape, 0) < n_pages) | \
                (lax.broadcasted_iota(I32, shape, 2) <= lax.broadcasted_iota(I32, shape, 1))

        def count(hit):
            per_lane = jnp.sum(hit.astype(I32), axis=0)
            return jnp.sum(per_lane, axis=-1, keepdims=True)[None]

        def thr_bit(bi, prefix):
            cand = prefix | jnp.left_shift(jnp.int32(1), 31 - bi)
            cs = cand ^ INT_MIN
            return jnp.where(count(keys >= cs) >= n_top, cand, prefix)

        thr = lax.fori_loop(0, 32, thr_bit, jnp.zeros((1, T, 1), I32)) ^ INT_MIN
        need = n_top - count(keys > thr)

        def tie_bit(bi, j0):
            cand = j0 | jnp.left_shift(jnp.int32(1), idx_bits - 1 - bi)
            return jnp.where(count((keys == thr) & (gidx < cand)) < need, cand, j0)

        jcut = lax.fori_loop(0, idx_bits, tie_bit, jnp.zeros((1, T, 1), I32))
        sel = ((keys > thr) | ((keys == thr) & (gidx <= jcut))) & valid
        sel_ref[...] = sel.astype(F32)


def _smp_dsa_select(page_table, qi_rows, wi_rows, knew, cache_kidx, li):
    DB, P = page_table.shape
    T = qi_rows.shape[1] // IDX_HEADS
    L = P * PAGE_SIZE + T
    body = functools.partial(_smp_dsa_select_body, n_pages=P, n_top=min(IDX_TOPK, L // 4),
                             idx_bits=max(1, ((P + 1) * PAGE_SIZE - 1).bit_length()))
    step_pages = math.gcd(P, IDX_PAGES)
    grid_spec = pltpu.PrefetchScalarGridSpec(
        num_scalar_prefetch=1,
        grid=(DB, P // step_pages),
        in_specs=[pl.BlockSpec((None, IDX_HEADS * T, IDX_DIM), lambda b, c, pt: (b, 0, 0)),
                  pl.BlockSpec((None, IDX_HEADS * T, 1), lambda b, c, pt: (b, 0, 0)),
                  pl.BlockSpec((None, IDX_DIM, PAGE_SIZE), lambda b, c, pt: (b, 0, 0))]
                 + _page_specs(step_pages, IDX_DIM, PAGE_SIZE, li),
        out_specs=pl.BlockSpec((None, P + 1, T, PAGE_SIZE), lambda b, c, pt: (b, 0, 0, 0)),
        scratch_shapes=[pltpu.VMEM((P + 1, T, PAGE_SIZE), I32)],
    )
    return pl.pallas_call(
        body,
        grid_spec=grid_spec,
        out_shape=jax.ShapeDtypeStruct((DB, P + 1, T, PAGE_SIZE), F32),
        compiler_params=_cparams(("parallel", "arbitrary")),
        name="smp_dsa_select",
    )(page_table, qi_rows, wi_rows, knew, *([cache_kidx] * step_pages))


def _smp_moba_select_body(pt_ref, qbd_ref, *rest, n_blocks, n_sel):
    pages = rest[:SEL_PAGES]
    sel_ref = rest[SEL_PAGES]
    mean_ref = rest[SEL_PAGES + 1]
    c = pl.program_id(1)
    per_step = SEL_PAGES * PAGE_SIZE // MOBA_BLOCK
    per_block = MOBA_BLOCK // PAGE_SIZE
    W = mean_ref.shape[1]
    lane = lax.broadcasted_iota(I32, mean_ref.shape, 1)

    @pl.when(c == 0)
    def _():
        mean_ref[...] = jnp.zeros(mean_ref.shape, F32)

    means = mean_ref[...]
    for j in range(per_step):
        tot = pages[j * per_block][...]
        for u in range(1, per_block):
            tot = tot + pages[j * per_block + u][...]
        col = jnp.sum(tot, axis=1, keepdims=True) * (1.0 / MOBA_BLOCK)
        means = jnp.where(lane == c * per_step + j, col, means)
    mean_ref[...] = means

    @pl.when(c == pl.num_programs(1) - 1)
    def _():
        g = _dot3(qbd_ref[...], means)
        blk = lax.broadcasted_iota(I32, g.shape, 1)
        g = jnp.where(blk < n_blocks, g, -jnp.inf)
        rank = jnp.zeros(g.shape, F32)
        for r in range(1, n_blocks):
            lower = pltpu.roll(g, r, axis=1)
            upper = pltpu.roll(g, W - r, axis=1)
            rank = rank + jnp.where(lower >= g, 1.0, 0.0) + jnp.where(upper > g, 1.0, 0.0)
        sel_ref[...] = jnp.where((rank < n_sel) & (blk < n_blocks), 1.0, 0.0)


def _smp_moba_select(page_table, qbd, cache_kT, li):
    DB, P = page_table.shape
    n_blocks = P * PAGE_SIZE // MOBA_BLOCK
    n_steps = P // SEL_PAGES
    W = 128
    assert n_blocks < W
    R = qbd.shape[1]
    body = functools.partial(_smp_moba_select_body, n_blocks=n_blocks, n_sel=min(MOBA_TOPK, n_blocks))
    grid_spec = pltpu.PrefetchScalarGridSpec(
        num_scalar_prefetch=1,
        grid=(DB, n_steps),
        in_specs=[pl.BlockSpec((None, R, D_ATT), lambda b, c, pt: (b, 0, 0))]
                 + _page_specs(SEL_PAGES, D_ATT, PAGE_SIZE, li),
        out_specs=pl.BlockSpec((None, R, W), lambda b, c, pt: (b, 0, 0)),
        scratch_shapes=[pltpu.VMEM((D_ATT, W), F32)],
    )
    return pl.pallas_call(
        body,
        grid_spec=grid_spec,
        out_shape=jax.ShapeDtypeStruct((DB, R, W), F32),
        compiler_params=_cparams(("parallel", "arbitrary")),
        name="smp_moba_select",
    )(page_table, qbd, *([cache_kT] * SEL_PAGES))


def _smp_attn_body(pt_ref, qbd_ref, rbr_ref, knew_ref, vnew_ref, sel_ref, selnew_ref, *rest, mode, n_pages):
    kp = rest[:ATT_PAGES]
    vp = rest[ATT_PAGES:2 * ATT_PAGES]
    o_ref, m_ref, l_ref, acc_ref = rest[2 * ATT_PAGES:]
    c = pl.program_id(1)
    R = qbd_ref.shape[0]
    T = R // N_HEADS
    q = (qbd_ref[...] * ATTN_SCALE).astype(BF16)
    rbr = rbr_ref[...]
    far_bias = rbr[:, N_BUCKETS - 1:]
    rowq = lax.broadcasted_iota(I32, (R, PAGE_SIZE), 0) & (T - 1)
    lane = lax.broadcasted_iota(I32, (R, PAGE_SIZE), 1)

    @pl.when(c == 0)
    def _():
        m_ref[...] = jnp.full(m_ref.shape, NEG_BIG, F32)
        l_ref[...] = jnp.zeros(l_ref.shape, F32)
        acc_ref[...] = jnp.zeros(acc_ref.shape, F32)

    def bias_for(dist):
        return _bias_chain(dist, lambda b: rbr[:, b:b + 1])

    def update(s, vT):
        m_old = m_ref[...]
        m_new = jnp.maximum(m_old, jnp.max(s, axis=-1, keepdims=True))
        p = jnp.exp(s - m_new)
        alpha = jnp.exp(m_old - m_new)
        l_ref[...] = alpha * l_ref[...] + jnp.sum(p, axis=-1, keepdims=True)
        acc_ref[...] = alpha * acc_ref[...] + _dot(p.astype(BF16), vT, _NT)
        m_ref[...] = m_new

    logits, vals = [], []
    for u in range(ATT_PAGES):
        page = c * ATT_PAGES + u
        s = _dot(q, kp[u][...].astype(BF16))
        if u == ATT_PAGES - 1:
            dist = (n_pages - page) * PAGE_SIZE + rowq - lane
            bias = lax.cond(c == pl.num_programs(1) - 1, lambda: bias_for(dist),
                            lambda: jnp.broadcast_to(far_bias, (R, PAGE_SIZE)))
        else:
            bias = far_bias
        if mode == "dsa":
            pen = jnp.where(jnp.tile(sel_ref[u], (N_HEADS, 1)) > 0.5, 0.0, NEG_BIG)
        else:
            blk = page // (MOBA_BLOCK // PAGE_SIZE)
            sel = sel_ref[...]
            pick = jnp.sum(jnp.where(lax.broadcasted_iota(I32, sel.shape, 1) == blk, sel, 0.0),
                           axis=-1, keepdims=True)
            pen = jnp.where(pick > 0.5, 0.0, NEG_BIG)
        logits.append(s + (bias + pen))
        vals.append(vp[u][...].astype(BF16))
    update(jnp.concatenate(logits, axis=1), jnp.concatenate(vals, axis=1))

    @pl.when(c == pl.num_programs(1) - 1)
    def _():
        s = _dot(q, knew_ref[...].astype(BF16)) + bias_for(rowq - lane)
        keep = lane <= rowq
        if mode == "dsa":
            keep = keep & (jnp.tile(selnew_ref[...], (N_HEADS, 1)) > 0.5)
        update(s + jnp.where(keep, 0.0, NEG_BIG), vnew_ref[...].astype(BF16))
        out = acc_ref[...] / l_ref[...]
        for h in range(N_HEADS):
            o_ref[:, h * HEAD_DIM:(h + 1) * HEAD_DIM] = out[h * T:(h + 1) * T, h * HEAD_DIM:(h + 1) * HEAD_DIM]


def _smp_attn(page_table, qbd, rbr, knewT, vnewT, sel, cache_kT, cache_vT, li, *, mode):
    DB, P = page_table.shape
    R = qbd.shape[1]
    T = R // N_HEADS
    assert T & (T - 1) == 0
    if mode == "dsa":
        sel_specs = [pl.BlockSpec((None, ATT_PAGES, T, PAGE_SIZE), lambda b, c, pt: (b, c, 0, 0)),
                     pl.BlockSpec((None, None, T, PAGE_SIZE), lambda b, c, pt: (b, P, 0, 0))]
    else:
        sel_specs = [pl.BlockSpec((None, R, sel.shape[-1]), lambda b, c, pt: (b, 0, 0)),
                     pl.BlockSpec((None, R, sel.shape[-1]), lambda b, c, pt: (b, 0, 0))]
    new_spec = pl.BlockSpec((None, D_ATT, PAGE_SIZE), lambda b, c, pt: (b, 0, 0))
    grid_spec = pltpu.PrefetchScalarGridSpec(
        num_scalar_prefetch=1,
        grid=(DB, P // ATT_PAGES),
        in_specs=[pl.BlockSpec((None, R, D_ATT), lambda b, c, pt: (b, 0, 0)),
                  pl.BlockSpec(rbr.shape, lambda b, c, pt: (0, 0)),
                  new_spec, new_spec]
                 + sel_specs + _page_specs(ATT_PAGES, D_ATT, PAGE_SIZE, li) + _page_specs(ATT_PAGES, D_ATT, PAGE_SIZE, li),
        out_specs=pl.BlockSpec((None, T, D_ATT), lambda b, c, pt: (b, 0, 0)),
        scratch_shapes=[pltpu.VMEM((R, 1), F32), pltpu.VMEM((R, 1), F32), pltpu.VMEM((R, D_ATT), F32)],
    )
    return pl.pallas_call(
        functools.partial(_smp_attn_body, mode=mode, n_pages=P),
        grid_spec=grid_spec,
        out_shape=jax.ShapeDtypeStruct((DB, T, D_ATT), F32),
        compiler_params=_cparams(("parallel", "arbitrary")),
        name="smp_attn_" + mode,
    )(page_table, qbd, rbr, knewT, vnewT, sel, sel, *([cache_kT] * ATT_PAGES), *([cache_vT] * ATT_PAGES))


def _project(x2, wT, g, b):
    o = IN_OFFS
    qkv_a = _mm(x2, wT[o[0]:o[3]])
    qi, ki, wi = _proj_idx(x2, wT[o[3]:o[4]], wT[o[4]:o[5]], wT[o[5]:o[6]], g[None], b[None])
    qk_b = _mm(x2, wT[o[6]:o[8]], passes=3)
    v_b = _mm(x2, wT[o[8]:o[9]])
    gates = _mm(x2, wT[o[9]:o[10]], act="sigmoid")
    return dict(q_a=qkv_a[:, :D_ATT], k_a=qkv_a[:, D_ATT:2 * D_ATT], v_a=qkv_a[:, 2 * D_ATT:],
                q_i=qi, k_i=ki, w_i=wi, q_b=qk_b[:, :D_ATT], k_b=qk_b[:, D_ATT:], v_b=v_b, gates=gates)


PROJ_TM = 512


def _proj_bulk_body(x_ref, w_ref, qaT_ref, kaT_ref, ka16_ref, vaT_ref, vaT16_ref, vbT_ref, vbT16_ref, g_ref):
    xb = x_ref[...].astype(BF16)
    rows = lambda n: w_ref[n * D_ATT:(n + 1) * D_ATT, :]
    qaT_ref[...] = (_dot(rows(0), xb, _NT) * PROMPT_Q_SCALE).astype(BF16)
    kaT = _dot(rows(1), xb, _NT)
    kaT_ref[...] = kaT
    ka16_ref[...] = kaT.T.astype(BF16)
    vaT = _dot(rows(2), xb, _NT)
    vaT_ref[...] = vaT
    vaT16_ref[...] = vaT.astype(BF16)
    vbT = _dot(rows(3), xb, _NT)
    vbT_ref[...] = vbT
    vbT16_ref[...] = vbT.astype(BF16)
    g_ref[...] = jax.nn.sigmoid(_dot(xb, w_ref[4 * D_ATT:, :], _NT))


def _proj_select_body(x_ref, w_ref, g_ref, b_ref, gc_ref, bc_ref,
                      qiT_ref, kiT_ref, kcat_ref, wiT_ref, qbT_ref, kbT_ref, kb16_ref):
    xh, xl = _split(x_ref[...])

    def proj_T(lo, hi):
        wh, wl = _split(w_ref[lo:hi, :])
        return _dot(wl, xh, _NT) + _dot(wh, xl, _NT) + _dot(wh, xh, _NT)

    o = [v - IN_OFFS[3] for v in IN_OFFS[3:9]]
    qiT_ref[...] = proj_T(o[0], o[1])
    kT = proj_T(o[1], o[2])
    mu = jnp.mean(kT, axis=0, keepdims=True)
    kc = kT - mu
    var = jnp.mean(kc * kc, axis=0, keepdims=True)
    kiT_ref[...] = kc * lax.rsqrt(var + LN_EPS) * gc_ref[...] + bc_ref[...]
    wh, wl = _split(w_ref[o[1]:o[2], :])
    k_nat = _dot(xl, wh, _NT) + _dot(xh, wl, _NT) + _dot(xh, wh, _NT)
    kcat_ref[...] = _kcat(_layer_norm(k_nat, g_ref[...], b_ref[...]))
    wiT_ref[...] = proj_T(o[2], o[3]) * (IDX_HEADS ** -0.5) * IDX_SCALE
    qbT_ref[...] = proj_T(o[3], o[4])
    kbT = proj_T(o[4], o[5])
    kbT_ref[...] = kbT
    kb16_ref[...] = kbT.T.astype(BF16)


def _project_prompt(x2, wT, g, b, B):
    M, K = x2.shape
    S = M // B
    tm = min(S, PROJ_TM)
    nt = S // tm
    o = IN_OFFS
    x_spec = pl.BlockSpec((tm, K), lambda i: (i, 0))
    full = lambda a: pl.BlockSpec(a.shape, lambda i: (0,) * a.ndim)
    specT = lambda n: pl.BlockSpec((None, n, tm), lambda i: (i // nt, 0, i % nt))
    spec = lambda n: pl.BlockSpec((tm, n), lambda i: (i, 0))
    shapeT = lambda n, d: jax.ShapeDtypeStruct((B, n, S), d)
    shape = lambda n, d: jax.ShapeDtypeStruct((M, n), d)

    w_bulk = jnp.concatenate([wT[o[0]:o[3]], wT[o[8]:o[10]]], axis=0).astype(BF16)
    q_aT, k_aT, k_a16, v_aT, v_aT16, v_bT, v_bT16, gates = pl.pallas_call(
        _proj_bulk_body,
        grid=(M // tm,),
        in_specs=[x_spec, full(w_bulk)],
        out_specs=[specT(D_ATT), specT(D_ATT), spec(D_ATT), specT(D_ATT), specT(D_ATT), specT(D_ATT), specT(D_ATT),
                   spec(2 * D_MODEL)],
        out_shape=[shapeT(D_ATT, BF16), shapeT(D_ATT, F32), shape(D_ATT, BF16), shapeT(D_ATT, F32), shapeT(D_ATT, BF16),
                   shapeT(D_ATT, F32), shapeT(D_ATT, BF16), shape(2 * D_MODEL, F32)],
        compiler_params=_cparams(("parallel",)),
        name="proj_bulk",
    )(x2, w_bulk)

    w_sel = wT[o[3]:o[8]]
    q_iT, k_iT, kcat, w_iT, q_bT, k_bT, k_b16 = pl.pallas_call(
        _proj_select_body,
        grid=(M // tm,),
        in_specs=[x_spec, full(w_sel)] + [pl.BlockSpec((1, IDX_DIM), lambda i: (0, 0))] * 2
                 + [pl.BlockSpec((IDX_DIM, 1), lambda i: (0, 0))] * 2,
        out_specs=[specT(IDX_HEADS * IDX_DIM), specT(IDX_DIM), spec(4 * IDX_DIM), specT(IDX_HEADS),
                   specT(D_ATT), specT(D_ATT), spec(D_ATT)],
        out_shape=[shapeT(IDX_HEADS * IDX_DIM, F32), shapeT(IDX_DIM, F32), shape(4 * IDX_DIM, BF16),
                   shapeT(IDX_HEADS, F32), shapeT(D_ATT, F32), shapeT(D_ATT, F32), shape(D_ATT, BF16)],
        compiler_params=_cparams(("parallel",)),
        name="proj_select",
    )(x2, w_sel, g[None], b[None], g[:, None], b[:, None])

    return dict(q_aT=q_aT, k_aT=k_aT, k_a16=k_a16, v_aT=v_aT, v_aT16=v_aT16, q_iT=q_iT, k_iT=k_iT, kcat=kcat,
                w_iT=w_iT, q_bT=q_bT, k_bT=k_bT, k_b16=k_b16, v_bT=v_bT, v_bT16=v_bT16, gates=gates)


def _heads_major(a, B, S):
    return a.reshape(B, S, N_HEADS, HEAD_DIM).transpose(0, 2, 1, 3)


def _block_diag_rows(q):
    DB, T, H, Dh = q.shape
    eye = jnp.eye(H, dtype=q.dtype)
    return jnp.einsum("bthd,hg->bhtgd", q, eye).reshape(DB, H * T, H * Dh)


def _prompt_mixers(pr, B, S, rb_a, rb_b, T_dsa):
    bt_a = _bias_tiles(rb_a, T_dsa)
    o_a = _dsa_prompt(rb_a[N_BUCKETS - 1], bt_a, pr["q_aT"], pr["k_a16"], pr["v_aT16"],
                      pr["q_iT"], pr["w_iT"], pr["kcat"], T=T_dsa)
    T = MOBA_BLOCK
    nf = S // T
    n_slots = 128 // N_HEADS
    assert nf <= n_slots
    means = _block_means(pr["k_bT"]).reshape(B, N_HEADS, HEAD_DIM, nf)
    eye = jnp.eye(N_HEADS, dtype=F32)
    mbdT = jnp.einsum("bhdj,hg->bjghd", means, eye)
    mbdT = jnp.pad(mbdT, ((0, 0), (0, n_slots - nf), (0, 0), (0, 0), (0, 0))).reshape(B, n_slots * N_HEADS, D_ATT)
    bt_b = _bias_tiles(rb_b, T)
    o_b = _moba_prompt(rb_b[N_BUCKETS - 1], bt_b, pr["q_bT"], pr["k_b16"], pr["v_bT16"], mbdT)
    return o_a, o_b


def _sample_mixers(sm, DB, T, caches, page_table, rb_a, rb_b, li):
    ck_a, cv_a, ck_i, ck_b, cv_b = caches
    new_page = lambda a: jnp.pad(a.reshape(DB, T, -1), ((0, 0), (0, PAGE_SIZE - T), (0, 0))).transpose(0, 2, 1)
    qi_rows = _heads_major(sm["q_i"], DB, T).reshape(DB, IDX_HEADS * T, IDX_DIM)
    wi_rows = sm["w_i"].reshape(DB, T, IDX_HEADS).transpose(0, 2, 1).reshape(DB, IDX_HEADS * T, 1)
    sel_a = _smp_dsa_select(page_table, qi_rows, wi_rows, new_page(sm["k_i"]), ck_i, li)
    rbr_a = jnp.repeat(rb_a.T, T, axis=0)
    rbr_b = jnp.repeat(rb_b.T, T, axis=0)
    qbd_a = _block_diag_rows(sm["q_a"].reshape(DB, T, N_HEADS, HEAD_DIM))
    o_a = _smp_attn(page_table, qbd_a, rbr_a, new_page(sm["k_a"]), new_page(sm["v_a"]), sel_a,
                    ck_a, cv_a, li, mode="dsa")
    qbd_b = _block_diag_rows(sm["q_b"].reshape(DB, T, N_HEADS, HEAD_DIM))
    sel_b = _smp_moba_select(page_table, qbd_b, ck_b, li)
    o_b = _smp_attn(page_table, qbd_b, rbr_b, new_page(sm["k_b"]), new_page(sm["v_b"]), sel_b,
                    ck_b, cv_b, li, mode="moba")
    return o_a.reshape(DB * T, D_ATT), o_b.reshape(DB * T, D_ATT)


def kernel(x_prompt, x_sample, cache_k_a, cache_v_a, cache_kidx, cache_k_b, cache_v_b, page_table, p_prompt, p_sample, rel_bias, w_in, kidx_ln_g, kidx_ln_b, w_branch_a, w_branch_b, w_out, ln1_g, ln1_b, w_router, b_router, w_gate_up, b_gate_up, w_down, b_down, w_ple_gate, w_ple_proj, ln2_g, ln2_b):
    B, S, D = x_prompt.shape
    DB, T, _ = x_sample.shape
    depth = w_in.shape[0]
    n_pool = cache_k_a.shape[1]
    rb_a = rel_bias[:, :N_HEADS]
    rb_b = rel_bias[:, N_HEADS:]
    kv_pages = lambda c: jnp.transpose(c, (0, 1, 3, 4, 2)).reshape(depth, n_pool, D_ATT, PAGE_SIZE)
    caches = (kv_pages(cache_k_a), kv_pages(cache_v_a), jnp.transpose(cache_kidx, (0, 1, 3, 2)),
              kv_pages(cache_k_b), kv_pages(cache_v_b))
    T_dsa = min(256, S)

    xp = x_prompt.reshape(B * S, D)
    xs = x_sample.reshape(DB * T, D)
    rows_p, rows_s = [], []
    for li in range(depth):
        bf = lambda a: a[li].astype(BF16)
        merge_w = (bf(w_branch_a), bf(w_branch_b), bf(w_out), ln1_g[li][None], ln1_b[li][None],
                   w_router[li], b_router[li][None], bf(w_ple_gate), bf(w_ple_proj))
        moe_w = (bf(w_gate_up), b_gate_up[li], bf(w_down), b_down[li], ln2_g[li][None], ln2_b[li][None])

        w_inT = jnp.swapaxes(w_in[li], 0, 1)
        pr = _project_prompt(xp, w_inT, kidx_ln_g[li], kidx_ln_b[li], B)
        o_a, o_b = _prompt_mixers(pr, B, S, rb_a, rb_b, T_dsa)
        x1b, comb, res = _merge(xp, o_a, o_b, pr["gates"], p_prompt[li].reshape(B * S, -1), *merge_w)
        xp = _moe(x1b, res, comb, *moe_w)
        rows_p.append(pr)

        sm = _project(xs, w_inT, kidx_ln_g[li], kidx_ln_b[li])
        o_a, o_b = _sample_mixers(sm, DB, T, caches, page_table, rb_a, rb_b, li)
        x1b, comb, res = _merge(xs, o_a, o_b, sm["gates"], p_sample[li].reshape(DB * T, -1), *merge_w)
        xs = _moe(x1b, res, comb, *moe_w)
        rows_s.append(sm)

    def stack(rows, name, lead, tail):
        return jnp.stack([r[name].reshape(lead + tail) for r in rows])

    def stack_T(name, feat):
        a = jnp.stack([r[name] for r in rows_p])
        return jnp.moveaxis(a.reshape((depth, B) + feat + (S,)), -1, 2)

    hd = (N_HEADS, HEAD_DIM)
    outs = [xp.reshape(B, S, D), xs.reshape(DB, T, D)]
    outs += [stack_T("k_aT", hd), stack_T("v_aT", hd), stack_T("k_iT", (IDX_DIM,)),
             stack_T("k_bT", hd), stack_T("v_bT", hd)]
    lead = (DB, T)
    outs += [stack(rows_s, "k_a", lead, hd), stack(rows_s, "v_a", lead, hd), stack(rows_s, "k_i", lead, (IDX_DIM,)),
             stack(rows_s, "k_b", lead, hd), stack(rows_s, "v_b", lead, hd)]
    return tuple(outs)
```
